```python
import math
import jax, jax.numpy as jnp
from jax import lax
import numpy as np

D_MODEL = 1024
BATCH = 8
SEQ = 2048
DEPTH = 4

GRID_W = 64
WIN_ROWS = 8
WIN_COLS = 16
NA_HEADS = 8
HEAD_DIM = 64
NA_WIDTH = NA_HEADS * HEAD_DIM
CONV_WIDTH = 512
CONV_TAPS = 3
MIX_IN = 3 * NA_WIDTH + 3 * CONV_WIDTH
MIX_WIDTH = NA_WIDTH + CONV_WIDTH
FOURIER_GROUPS = 4
N_EXPERTS = 16
EXPERT_FF = 1024
CAPACITY_FACTOR = 2
N_EVEN = (DEPTH + 1) // 2
N_ODD = DEPTH // 2
N_MOD = 6
RMS_EPS = 1e-6

kernel_name = "hybrid_natten_shortconv_fnet_ecmoe_encoder"


def _rmsnorm(x, g):
    xf = x.astype(jnp.float32)
    y = xf * lax.rsqrt(jnp.mean(xf * xf, axis=-1, keepdims=True) + RMS_EPS)
    return (y * g.astype(jnp.float32)).astype(x.dtype)


def _neighbourhood_attention(q, k, v, rpb):
    b, h, s, hd = q.shape
    rows = s // GRID_W
    kr = min(WIN_ROWS, rows)
    kc = WIN_COLS
    n_keys = kr * kc
    r = jnp.arange(rows, dtype=jnp.int32)
    col = jnp.arange(GRID_W, dtype=jnp.int32)
    rs = jnp.clip(r - kr // 2, 0, rows - kr)
    cs = jnp.clip(col - kc // 2, 0, GRID_W - kc)
    key_r = rs[:, None] + jnp.arange(kr, dtype=jnp.int32)
    key_c = cs[:, None] + jnp.arange(kc, dtype=jnp.int32)
    idx = key_r[:, None, :, None] * GRID_W + key_c[None, :, None, :]
    idx = idx.reshape(rows, GRID_W, n_keys)
    dr = key_r - r[:, None]
    dc = key_c - col[:, None]
    bias = rpb[:, dr[:, None, :, None] + (WIN_ROWS - 1), dc[None, :, None, :] + (WIN_COLS - 1)]
    bias = bias.reshape(h, rows, GRID_W, n_keys).transpose(1, 0, 2, 3)
    q_rows = q.reshape(b, h, rows, GRID_W, hd).transpose(2, 0, 1, 3, 4)
    scale = 1.0 / math.sqrt(hd)

    def row_block(args):
        q_blk, idx_blk, bias_blk = args
        k_blk = k[:, :, idx_blk]
        v_blk = v[:, :, idx_blk]
        sc = jnp.einsum('bhqd,bhqnd->bhqn', q_blk, k_blk).astype(jnp.float32) * scale
        sc = sc + bias_blk.astype(jnp.float32)[None]
        p = jax.nn.softmax(sc, axis=-1).astype(v.dtype)
        return jnp.einsum('bhqn,bhqnd->bhqd', p, v_blk)

    out = lax.map(row_block, (q_rows, idx, bias))
    return out.transpose(1, 2, 0, 3, 4).reshape(b, h, s, hd)


def _short_gated_conv(bg, cg, xv, conv_w):
    u = cg * xv
    u_prev = jnp.pad(u, ((0, 0), (1, 0), (0, 0)))[:, :-1]
    u_next = jnp.pad(u, ((0, 0), (0, 1), (0, 0)))[:, 1:]
    z = u_prev * conv_w[0] + u * conv_w[1] + u_next * conv_w[2]
    return bg * z


def _even_mixer(h, w_in, qk_g, rpb, conv_w, w_out):
    b, s, _ = h.shape
    proj = h @ w_in
    q, k, v, bg, cg, xv = jnp.split(
        proj,
        [NA_WIDTH, 2 * NA_WIDTH, 3 * NA_WIDTH, 3 * NA_WIDTH + CONV_WIDTH, 3 * NA_WIDTH + 2 * CONV_WIDTH],
        axis=-1)
    q = _rmsnorm(q.reshape(b, s, NA_HEADS, HEAD_DIM), qk_g[0]).transpose(0, 2, 1, 3)
    k = _rmsnorm(k.reshape(b, s, NA_HEADS, HEAD_DIM), qk_g[1]).transpose(0, 2, 1, 3)
    v = v.reshape(b, s, NA_HEADS, HEAD_DIM).transpose(0, 2, 1, 3)
    attn = _neighbourhood_attention(q, k, v, rpb)
    attn = attn.transpose(0, 2, 1, 3).reshape(b, s, NA_WIDTH)
    conv = _short_gated_conv(bg, cg, xv, conv_w)
    return jnp.concatenate([attn, conv], axis=-1) @ w_out


def _fourier_mixer(h, w_out):
    b, s, d = h.shape
    hg = h.astype(jnp.float32).reshape(b, s, FOURIER_GROUPS, d // FOURIER_GROUPS)
    f = jnp.real(jnp.fft.fft2(hg, axes=(1, 3), norm="ortho"))
    return f.reshape(b, s, d).astype(h.dtype) @ w_out


def _expert_choice_moe(h, router_w, w_gate, w_up, w_down):
    b, s, d = h.shape
    cap = max(1, CAPACITY_FACTOR * s // N_EXPERTS)
    logits = (h @ router_w).astype(jnp.float32)
    aff = jax.nn.softmax(logits, axis=-1)
    gate, tok = lax.top_k(aff.transpose(0, 2, 1), cap)
    bidx = jnp.arange(b, dtype=jnp.int32)[:, None, None]
    xe = h[bidx, tok]
    a = jnp.einsum('becd,edf->becf', xe, w_gate)
    u = jnp.einsum('becd,edf->becf', xe, w_up)
    y = jnp.einsum('becf,efd->becd', jax.nn.silu(a) * u, w_down)
    y = y * gate[..., None].astype(h.dtype)
    return jnp.zeros_like(h).at[bidx, tok].add(y)


def setup_inputs(seed: int = 0) -> dict:
    key = jax.random.key(seed)
    ks = jax.random.split(key, 16)
    f32 = jnp.float32
    d = D_MODEL

    def nrm(k, shape, scale):
        return jax.random.normal(k, shape, f32) * scale

    return {
        "x": nrm(ks[0], (BATCH, SEQ, d), 1.0),
        "c": nrm(ks[1], (BATCH, d), 1.0),
        "mod_w": nrm(ks[2], (DEPTH, d, N_MOD * d), 0.5 * d ** -0.5),
        "mod_b": nrm(ks[3], (DEPTH, N_MOD * d), 0.02),
        "norm_g": 1.0 + nrm(ks[4], (DEPTH, 2, d), 0.02),
        "mix_w_in": nrm(ks[5], (N_EVEN, d, MIX_IN), d ** -0.5),
        "qk_g": 1.0 + nrm(ks[6], (N_EVEN, 2, HEAD_DIM), 0.02),
        "rpb": nrm(ks[7], (N_EVEN, NA_HEADS, 2 * WIN_ROWS - 1, 2 * WIN_COLS - 1), 0.1),
        "conv_w": nrm(ks[8], (N_EVEN, CONV_TAPS, CONV_WIDTH), CONV_TAPS ** -0.5),
        "mix_w_out": nrm(ks[9], (N_EVEN, MIX_WIDTH, d), MIX_WIDTH ** -0.5),
        "fnet_w_out": nrm(ks[10], (N_ODD, d, d), d ** -0.5),
        "router_w": nrm(ks[11], (DEPTH, d, N_EXPERTS), d ** -0.5),
        "exp_w_gate": nrm(ks[12], (DEPTH, N_EXPERTS, d, EXPERT_FF), d ** -0.5),
        "exp_w_up": nrm(ks[13], (DEPTH, N_EXPERTS, d, EXPERT_FF), d ** -0.5),
        "exp_w_down": nrm(ks[14], (DEPTH, N_EXPERTS, EXPERT_FF, d), EXPERT_FF ** -0.5),
    }


def reference(x, c, mod_w, mod_b, norm_g, mix_w_in, qk_g, rpb, conv_w, mix_w_out,
              fnet_w_out, router_w, exp_w_gate, exp_w_up, exp_w_down):
    sc = jax.nn.silu(c)
    for l in range(DEPTH):
        mod = sc @ mod_w[l] + mod_b[l]
        sh1, sc1, g1, sh2, sc2, g2 = [m[:, None, :] for m in jnp.split(mod, N_MOD, axis=-1)]
        h = _rmsnorm(x, norm_g[l, 0]) * (1.0 + sc1) + sh1
        if l % 2 == 0:
            j = l // 2
            m = _even_mixer(h, mix_w_in[j], qk_g[j], rpb[j], conv_w[j], mix_w_out[j])
        else:
            m = _fourier_mixer(h, fnet_w_out[l // 2])
        x = x + g1 * m
        h = _rmsnorm(x, norm_g[l, 1]) * (1.0 + sc2) + sh2
        x = x + g2 * _expert_choice_moe(h, router_w[l], exp_w_gate[l], exp_w_up[l], exp_w_down[l])
    return x
```

```python
import functools
import math

import numpy as np
import jax
import jax.numpy as jnp
from jax import lax
from jax.experimental import pallas as pl
from jax.experimental.pallas import tpu as pltpu

F32 = jnp.float32
BF16 = jnp.bfloat16
HIGHEST = lax.Precision.HIGHEST

GRID_W = 64
WIN_ROWS = 8
WIN_COLS = 16
HEAD_DIM = 64
FOURIER_GROUPS = 4
N_MOD = 6
CAPACITY_FACTOR = 2
RMS_EPS = 1e-6
NEG_BIAS = -1e30
LANES = 128
VMEM_LIMIT = 56 * 1024 * 1024

_NT = (((1,), (1,)), ((), ()))
_TN = (((0,), (0,)), ((), ()))


def _params(*sem):
    return pltpu.CompilerParams(dimension_semantics=sem, vmem_limit_bytes=VMEM_LIMIT)


def _ln_mod(x, g, sc, sh):
    y = x * lax.rsqrt(jnp.mean(x * x, axis=-1, keepdims=True) + RMS_EPS)
    return (y * g) * (1.0 + sc) + sh


def _mod_kernel(c_ref, w_ref, b_ref, o_ref):
    c = c_ref[...]
    s = c * jax.nn.sigmoid(c)
    o_ref[...] = jnp.dot(s, w_ref[...], precision=HIGHEST, preferred_element_type=F32) + b_ref[...]


def _modulation(c, mod_w, mod_b):
    depth, d, n = mod_w.shape
    b = c.shape[0]
    tn = 1536
    return pl.pallas_call(
        _mod_kernel,
        grid=(depth, n // tn),
        in_specs=[
            pl.BlockSpec((b, d), lambda l, j: (0, 0)),
            pl.BlockSpec((None, d, tn), lambda l, j: (l, 0, j)),
            pl.BlockSpec((None, 1, tn), lambda l, j: (l, 0, j)),
        ],
        out_specs=pl.BlockSpec((None, b, tn), lambda l, j: (l, 0, j)),
        out_shape=jax.ShapeDtypeStruct((depth, b, n), F32),
        compiler_params=_params("parallel", "parallel"),
        name="adaln_mod",
    )(c, mod_w, mod_b.reshape(depth, 1, n))


def _row_spec(d, tiles_per_seq, k):
    return pl.BlockSpec((None, None, 1, d), lambda i: (i // tiles_per_seq, k, 0, 0))


def _inproj_kernel(x_ref, g_ref, sc_ref, sh_ref, w_ref, o_ref):
    h = _ln_mod(x_ref[...], g_ref[...], sc_ref[...], sh_ref[...]).astype(BF16)
    o_ref[...] = jnp.dot(h, w_ref[...], preferred_element_type=F32)


def _in_projection(x2, g, modl, w_bf, seq, tm=512):
    t, d = x2.shape
    n = w_bf.shape[1]
    tps = seq // tm
    return pl.pallas_call(
        _inproj_kernel,
        grid=(t // tm,),
        in_specs=[
            pl.BlockSpec((tm, d), lambda i: (i, 0)),
            pl.BlockSpec((1, d), lambda i: (0, 0)),
            _row_spec(d, tps, 1),
            _row_spec(d, tps, 0),
            pl.BlockSpec((d, n), lambda i: (0, 0)),
        ],
        out_specs=pl.BlockSpec((tm, n), lambda i: (i, 0)),
        out_shape=jax.ShapeDtypeStruct((t, n), F32),
        compiler_params=_params("parallel"),
        name="in_proj",
    )(x2, g, modl, modl, w_bf)


def _bias_table(rpb):
    heads, nr, nc = rpb.shape
    left = GRID_W - WIN_COLS
    w = jnp.pad(rpb, ((0, 0), (0, 0), (left, 2 * GRID_W - left - nc)))
    flat = jnp.tile(w, (1, 1, GRID_W))[..., :GRID_W * (2 * GRID_W - 1)]
    toe = flat.reshape(heads, nr, GRID_W, 2 * GRID_W - 1)[..., GRID_W - 1:]
    bands = jnp.stack([toe[:, WIN_ROWS - 1 - d:2 * WIN_ROWS - 1 - d] for d in range(WIN_ROWS)], axis=1)
    tbl = bands.transpose(0, 1, 3, 2, 4)
    qc = np.arange(GRID_W)[:, None, None]
    kc = np.arange(GRID_W)[None, None, :]
    cs = np.clip(qc - WIN_COLS // 2, 0, GRID_W - WIN_COLS)
    valid = np.broadcast_to((kc >= cs) & (kc < cs + WIN_COLS), (GRID_W, WIN_ROWS, GRID_W))
    tbl = jnp.where(valid[None, None], tbl, NEG_BIAS)
    return tbl.reshape(heads, WIN_ROWS, GRID_W, WIN_ROWS * GRID_W)


def _mixers_kernel(q_ref, k_ref, v_ref, bg_ref, cg_ref, xv_ref, qkg_ref, tbl_ref, cw_ref,
                   attn_ref, conv_ref, q0_s, q1_s, k_s, v_s, pad_s, *, rows):
    seq = q_ref.shape[0]
    lane = lax.broadcasted_iota(jnp.int32, (1, LANES), 1)
    first = lane < HEAD_DIM

    def head_norm(x, g):
        xx = x * x
        s0 = jnp.sum(jnp.where(first, xx, 0.0), axis=-1, keepdims=True)
        s1 = jnp.sum(jnp.where(first, 0.0, xx), axis=-1, keepdims=True)
        ms = jnp.where(first, s0, s1) * (1.0 / HEAD_DIM)
        return x * lax.rsqrt(ms + RMS_EPS) * g

    qn = head_norm(q_ref[...], qkg_ref[0:1, :]) * (1.0 / math.sqrt(HEAD_DIM))
    q0_s[...] = jnp.where(first, qn, 0.0).astype(BF16)
    q1_s[...] = jnp.where(first, 0.0, qn).astype(BF16)
    k_s[...] = head_norm(k_ref[...], qkg_ref[1:2, :]).astype(BF16)
    v_s[...] = v_ref[...].astype(BF16)

    kr = min(WIN_ROWS, rows)
    band = kr * GRID_W

    def row_body(r, carry):
        rs = jnp.clip(r - kr // 2, 0, rows - kr)
        d = r - rs
        q_off = pl.multiple_of(r * GRID_W, GRID_W)
        k_off = pl.multiple_of(rs * GRID_W, GRID_W)
        kb = k_s[pl.ds(k_off, band), :]
        vb = v_s[pl.ds(k_off, band), :]
        outs = []
        for h, qs in enumerate((q0_s, q1_s)):
            s = lax.dot_general(qs[pl.ds(q_off, GRID_W), :], kb, _NT, preferred_element_type=F32)
            s = s + tbl_ref[h, d]
            p = jnp.exp(s - jnp.max(s, axis=-1, keepdims=True))
            l = jnp.sum(p, axis=-1, keepdims=True)
            outs.append(jnp.dot(p.astype(BF16), vb, preferred_element_type=F32) / l)
        attn_ref[pl.ds(q_off, GRID_W), :] = jnp.where(first, outs[0], outs[1])
        return carry

    lax.fori_loop(0, rows, row_body, 0)

    zeros = jnp.zeros((8, LANES), F32)
    pad_s[0:8, :] = zeros
    pad_s[seq + 8:seq + 16, :] = zeros
    pad_s[8:seq + 8, :] = cg_ref[...] * xv_ref[...]
    z = (pad_s[7:seq + 7, :] * cw_ref[0:1, :] + pad_s[8:seq + 8, :] * cw_ref[1:2, :]
         + pad_s[9:seq + 9, :] * cw_ref[2:3, :])
    conv_ref[...] = bg_ref[...] * z


def _mixers(proj3, qk_g, rpb, conv_w):
    b, seq, n = proj3.shape
    width = n // 6
    nblk = width // LANES
    heads = width // HEAD_DIM
    rows = seq // GRID_W
    assert rows >= WIN_ROWS and rpb.shape[0] == heads
    tbl = _bias_table(rpb)
    qkg = jnp.tile(qk_g, (1, LANES // HEAD_DIM))

    def col(k):
        return pl.BlockSpec((None, seq, LANES), lambda i, j, k=k: (i, 0, k * nblk + j))

    out_spec = pl.BlockSpec((None, seq, LANES), lambda i, j: (i, 0, j))
    hp = LANES // HEAD_DIM
    return pl.pallas_call(
        functools.partial(_mixers_kernel, rows=rows),
        grid=(b, nblk),
        in_specs=[col(0), col(1), col(2), col(3), col(4), col(5),
                  pl.BlockSpec((2, LANES), lambda i, j: (0, 0)),
                  pl.BlockSpec((hp, WIN_ROWS, GRID_W, WIN_ROWS * GRID_W), lambda i, j: (j, 0, 0, 0)),
                  pl.BlockSpec((conv_w.shape[0], LANES), lambda i, j: (0, j))],
        out_specs=[out_spec, out_spec],
        out_shape=[jax.ShapeDtypeStruct((b, seq, width), F32)] * 2,
        scratch_shapes=[pltpu.VMEM((seq, LANES), BF16)] * 4 + [pltpu.VMEM((seq + 16, LANES), F32)],
        compiler_params=_params("parallel", "parallel"),
        name="attn_conv",
    )(proj3, proj3, proj3, proj3, proj3, proj3, qkg, tbl, conv_w)


def _outproj_kernel(a_ref, c_ref, wa_ref, wc_ref, x_ref, gate_ref, o_ref):
    m = jnp.dot(a_ref[...].astype(BF16), wa_ref[...], preferred_element_type=F32)
    m = m + jnp.dot(c_ref[...].astype(BF16), wc_ref[...], preferred_element_type=F32)
    o_ref[...] = x_ref[...] + gate_ref[...] * m


def _out_projection(attn2, conv2, w_bf, x2, modl, seq, tm=512):
    t, d = x2.shape
    wa = attn2.shape[1]
    wc = conv2.shape[1]
    tps = seq // tm
    return pl.pallas_call(
        _outproj_kernel,
        grid=(t // tm,),
        in_specs=[
            pl.BlockSpec((tm, wa), lambda i: (i, 0)),
            pl.BlockSpec((tm, wc), lambda i: (i, 0)),
            pl.BlockSpec((wa, d), lambda i: (0, 0)),
            pl.BlockSpec((wc, d), lambda i: (1, 0)),
            pl.BlockSpec((tm, d), lambda i: (i, 0)),
            _row_spec(d, tps, 2),
        ],
        out_specs=pl.BlockSpec((tm, d), lambda i: (i, 0)),
        out_shape=jax.ShapeDtypeStruct((t, d), F32),
        compiler_params=_params("parallel"),
        name="out_proj",
    )(attn2, conv2, w_bf, w_bf, x2, modl)


def _dft_tables(n):
    jk = (np.arange(n)[:, None] * np.arange(n)[None, :]) % n
    ang = 2.0 * np.pi * jk.astype(np.float64) / n
    return np.cos(ang).astype(np.float32), np.sin(ang).astype(np.float32)


def _fnet_chan_kernel(x_ref, g_ref, sc_ref, sh_ref, cs_ref, a_ref, b_ref, *, gw):
    h = _ln_mod(x_ref[...], g_ref[...], sc_ref[...], sh_ref[...]).astype(BF16)
    for g in range(h.shape[1] // gw):
        ab = jnp.dot(h[:, g * gw:(g + 1) * gw], cs_ref[...], preferred_element_type=F32)
        a_ref[:, g * gw:(g + 1) * gw] = ab[:, :gw].astype(BF16)
        b_ref[:, g * gw:(g + 1) * gw] = ab[:, gw:].astype(BF16)


def _fnet_seq_kernel(cs_ref, ss_ref, a_ref, b_ref, w_ref, x_ref, gate_ref, o_ref, *, norm):
    f = jnp.dot(cs_ref[...], a_ref[...], preferred_element_type=F32)
    f = f - jnp.dot(ss_ref[...], b_ref[...], preferred_element_type=F32)
    m = jnp.dot((f * norm).astype(BF16), w_ref[...], preferred_element_type=F32)
    o_ref[...] = x_ref[...] + gate_ref[...] * m


def _fnet_layer(x3, g, modl, w_bf, tm=512):
    b, seq, d = x3.shape
    gw = d // FOURIER_GROUPS
    cc, sc = _dft_tables(gw)
    cs_chan = jnp.concatenate([jnp.asarray(cc), jnp.asarray(sc)], axis=1).astype(BF16)
    cseq, sseq = _dft_tables(seq)
    cseq = jnp.asarray(cseq).astype(BF16)
    sseq = jnp.asarray(sseq).astype(BF16)
    x2 = x3.reshape(b * seq, d)
    tps = seq // tm
    a, bm = pl.pallas_call(
        functools.partial(_fnet_chan_kernel, gw=gw),
        grid=(b * seq // tm,),
        in_specs=[
            pl.BlockSpec((tm, d), lambda i: (i, 0)),
            pl.BlockSpec((1, d), lambda i: (0, 0)),
            _row_spec(d, tps, 1),
            _row_spec(d, tps, 0),
            pl.BlockSpec((gw, 2 * gw), lambda i: (0, 0)),
        ],
        out_specs=[pl.BlockSpec((tm, d), lambda i: (i, 0))] * 2,
        out_shape=[jax.ShapeDtypeStruct((b * seq, d), BF16)] * 2,
        compiler_params=_params("parallel"),
        name="fnet_chan",
    )(x2, g, modl, modl, cs_chan)
    a3 = a.reshape(b, seq, d)
    b3 = bm.reshape(b, seq, d)
    norm = 1.0 / math.sqrt(seq * gw)
    return pl.pallas_call(
        functools.partial(_fnet_seq_kernel, norm=norm),
        grid=(b, seq // tm),
        in_specs=[
            pl.BlockSpec((tm, seq), lambda i, j: (j, 0)),
            pl.BlockSpec((tm, seq), lambda i, j: (j, 0)),
            pl.BlockSpec((None, seq, d), lambda i, j: (i, 0, 0)),
            pl.BlockSpec((None, seq, d), lambda i, j: (i, 0, 0)),
            pl.BlockSpec((d, d), lambda i, j: (0, 0)),
            pl.BlockSpec((None, tm, d), lambda i, j: (i, j, 0)),
            pl.BlockSpec((None, None, 1, d), lambda i, j: (i, 2, 0, 0)),
        ],
        out_specs=pl.BlockSpec((None, tm, d), lambda i, j: (i, j, 0)),
        out_shape=jax.ShapeDtypeStruct((b, seq, d), F32),
        compiler_params=_params("parallel", "parallel"),
        name="fnet_seq",
    )(cseq, sseq, a3, b3, w_bf, x3, modl)


def _router_kernel(x_ref, g_ref, sc_ref, sh_ref, rwt_ref, h_ref, aff_ref):
    h = _ln_mod(x_ref[...], g_ref[...], sc_ref[...], sh_ref[...])
    h_ref[...] = h.astype(BF16)
    logits = lax.dot_general(rwt_ref[...], h, _NT, precision=HIGHEST, preferred_element_type=F32)
    e = jnp.exp(logits - jnp.max(logits, axis=0, keepdims=True))
    aff_ref[...] = e / jnp.sum(e, axis=0, keepdims=True)


def _router(x3, g, modl, router_wt, tm=512):
    b, seq, d = x3.shape
    e = router_wt.shape[0]
    tps = seq // tm
    return pl.pallas_call(
        _router_kernel,
        grid=(b, tps),
        in_specs=[
            pl.BlockSpec((None, tm, d), lambda i, j: (i, j, 0)),
            pl.BlockSpec((1, d), lambda i, j: (0, 0)),
            pl.BlockSpec((None, None, 1, d), lambda i, j: (i, 4, 0, 0)),
            pl.BlockSpec((None, None, 1, d), lambda i, j: (i, 3, 0, 0)),
            pl.BlockSpec((e, d), lambda i, j: (0, 0)),
        ],
        out_specs=[pl.BlockSpec((None, tm, d), lambda i, j: (i, j, 0)),
                   pl.BlockSpec((None, e, tm), lambda i, j: (i, 0, j))],
        out_shape=[jax.ShapeDtypeStruct((b, seq, d), BF16), jax.ShapeDtypeStruct((b, e, seq), F32)],
        compiler_params=_params("parallel", "parallel"),
        name="router",
    )(x3, g, modl, modl, router_wt)


def _topk_kernel(aff_ref, gate_ref, rank_ref, tri_s, *, cap):
    seq = aff_ref.shape[1]

    @pl.when(pl.program_id(0) == 0)
    def _():
        r = lax.broadcasted_iota(jnp.int32, (seq, seq), 0)
        c = lax.broadcasted_iota(jnp.int32, (seq, seq), 1)
        tri_s[...] = jnp.where(r < c, 1.0, 0.0).astype(BF16)

    a = aff_ref[...]

    def search(i, t):
        cand = t | lax.shift_left(jnp.int32(1), 30 - i)
        cnt = jnp.sum((a >= pltpu.bitcast(cand, F32)).astype(jnp.int32), axis=1, keepdims=True)
        return jnp.where(cnt >= cap, cand, t)

    thr = lax.fori_loop(0, 31, search, jnp.zeros((a.shape[0], 1), jnp.int32))
    thr = pltpu.bitcast(thr, F32)
    gt = a > thr
    eq = a == thr
    need = cap - jnp.sum(gt.astype(jnp.int32), axis=1, keepdims=True)
    tri = tri_s[...]
    eq_before = jnp.dot(jnp.where(eq, 1.0, 0.0).astype(BF16), tri, preferred_element_type=F32)
    sel = gt | (eq & (eq_before < need.astype(F32)))
    before = jnp.dot(jnp.where(sel, 1.0, 0.0).astype(BF16), tri, preferred_element_type=F32)
    gate_ref[...] = jnp.where(sel, a, 0.0)
    rank_ref[...] = jnp.where(sel, before.astype(jnp.int32), -1)


def _topk(aff, cap):
    b, e, seq = aff.shape
    spec = pl.BlockSpec((None, e, seq), lambda i: (i, 0, 0))
    return pl.pallas_call(
        functools.partial(_topk_kernel, cap=cap),
        grid=(b,),
        in_specs=[spec],
        out_specs=[spec, spec],
        out_shape=[jax.ShapeDtypeStruct((b, e, seq), F32), jax.ShapeDtypeStruct((b, e, seq), jnp.int32)],
        scratch_shapes=[pltpu.VMEM((seq, seq), BF16)],
        compiler_params=_params("arbitrary"),
        name="topk",
    )(aff)


def _expert_kernel(h_ref, rank_ref, gate_ref, wg_ref, wu_ref, wd_ref, y_ref, *, cap):
    seq = h_ref.shape[0]
    slot = lax.broadcasted_iota(jnp.int32, (cap, seq), 0)
    pick = slot == rank_ref[...]
    xe = jnp.dot(jnp.where(pick, 1.0, 0.0).astype(BF16), h_ref[...], preferred_element_type=F32)
    xe = xe.astype(BF16)
    a = jnp.dot(xe, wg_ref[...], preferred_element_type=F32)
    u = jnp.dot(xe, wu_ref[...], preferred_element_type=F32)
    hm = (a * jax.nn.sigmoid(a) * u).astype(BF16)
    y = jnp.dot(hm, wd_ref[...], preferred_element_type=F32)
    gate = jnp.sum(jnp.where(pick, gate_ref[...], 0.0), axis=1, keepdims=True)
    y_ref[...] = (y * gate).astype(BF16)


def _experts(h3, rank4, gate4, wg, wu, wd, cap):
    b, seq, d = h3.shape
    e, _, ff = wg.shape
    row = pl.BlockSpec((None, None, 1, seq), lambda i, j: (j, i, 0, 0))
    return pl.pallas_call(
        functools.partial(_expert_kernel, cap=cap),
        grid=(e, b),
        in_specs=[
            pl.BlockSpec((None, seq, d), lambda i, j: (j, 0, 0)),
            row, row,
            pl.BlockSpec((None, d, ff), lambda i, j: (i, 0, 0)),
            pl.BlockSpec((None, d, ff), lambda i, j: (i, 0, 0)),
            pl.BlockSpec((None, ff, d), lambda i, j: (i, 0, 0)),
        ],
        out_specs=pl.BlockSpec((None, None, cap, d), lambda i, j: (j, i, 0, 0)),
        out_shape=jax.ShapeDtypeStruct((b, e, cap, d), BF16),
        compiler_params=_params("parallel", "parallel"),
        name="experts",
    )(h3, rank4, gate4, wg, wu, wd)


def _combine_kernel(y_ref, rank_ref, x_ref, gate_ref, o_ref, acc_s, *, cap):
    e = pl.program_id(2)

    @pl.when(e == 0)
    def _():
        acc_s[...] = jnp.zeros_like(acc_s)

    ts = rank_ref.shape[1]
    slot = lax.broadcasted_iota(jnp.int32, (cap, ts), 0)
    pick = jnp.where(slot == rank_ref[...], 1.0, 0.0).astype(BF16)
    acc_s[...] += lax.dot_general(pick, y_ref[...], _TN, preferred_element_type=F32)

    @pl.when(e == pl.num_programs(2) - 1)
    def _():
        o_ref[...] = x_ref[...] + gate_ref[...] * acc_s[...]


def _combine(y4, rank4, x3, modl, ts=512):
    b, e, cap, d = y4.shape
    seq = x3.shape[1]
    return pl.pallas_call(
        functools.partial(_combine_kernel, cap=cap),
        grid=(b, seq // ts, e),
        in_specs=[
            pl.BlockSpec((None, None, cap, d), lambda i, j, k: (i, k, 0, 0)),
            pl.BlockSpec((None, None, 1, ts), lambda i, j, k: (i, k, 0, j)),
            pl.BlockSpec((None, ts, d), lambda i, j, k: (i, j, 0)),
            pl.BlockSpec((None, None, 1, d), lambda i, j, k: (i, 5, 0, 0)),
        ],
        out_specs=pl.BlockSpec((None, ts, d), lambda i, j, k: (i, j, 0)),
        out_shape=jax.ShapeDtypeStruct((b, seq, d), F32),
        scratch_shapes=[pltpu.VMEM((ts, d), F32)],
        compiler_params=_params("parallel", "parallel", "arbitrary"),
        name="combine",
    )(y4, rank4, x3, modl)


def _moe_layer(x3, g, modl, router_w, wg, wu, wd):
    b, seq, d = x3.shape
    e = router_w.shape[1]
    cap = max(1, CAPACITY_FACTOR * seq // e)
    h3, aff = _router(x3, g, modl, router_w.T)
    gate, rank = _topk(aff, cap)
    rank4 = rank.reshape(b, e, 1, seq)
    gate4 = gate.reshape(b, e, 1, seq)
    y4 = _experts(h3, rank4, gate4, wg.astype(BF16), wu.astype(BF16), wd.astype(BF16), cap)
    return _combine(y4, rank4, x3, modl)


def kernel(x, c, mod_w, mod_b, norm_g, mix_w_in, qk_g, rpb, conv_w, mix_w_out, fnet_w_out,
           router_w, exp_w_gate, exp_w_up, exp_w_down):
    b, seq, d = x.shape
    depth = mod_w.shape[0]
    mod = _modulation(c, mod_w, mod_b).reshape(depth, b, N_MOD, 1, d)
    for l in range(depth):
        modl = mod[l]
        g1 = norm_g[l, 0].reshape(1, d)
        g2 = norm_g[l, 1].reshape(1, d)
        if l % 2 == 0:
            j = l // 2
            x2 = x.reshape(b * seq, d)
            proj = _in_projection(x2, g1, modl, mix_w_in[j].astype(BF16), seq)
            attn, conv = _mixers(proj.reshape(b, seq, -1), qk_g[j], rpb[j], conv_w[j])
            width = attn.shape[-1]
            x = _out_projection(attn.reshape(b * seq, width), conv.reshape(b * seq, width),
                                mix_w_out[j].astype(BF16), x2, modl, seq).reshape(b, seq, d)
        else:
            x = _fnet_layer(x, g1, modl, fnet_w_out[l // 2].astype(BF16))
        x = _moe_layer(x, g2, modl, router_w[l], exp_w_gate[l], exp_w_up[l], exp_w_down[l])
    return x
```

```python
import functools
import math

import numpy as np
import jax
import jax.numpy as jnp
from jax import lax
from jax.experimental import pallas as pl
from jax.experimental.pallas import tpu as pltpu

F32 = jnp.float32
BF16 = jnp.bfloat16
HIGHEST = lax.Precision.HIGHEST

GRID_W = 64
WIN_ROWS = 8
WIN_COLS = 16
HEAD_DIM = 64
FOURIER_GROUPS = 4
N_MOD = 6
CAPACITY_FACTOR = 2
RMS_EPS = 1e-6
NEG_BIAS = -1e30
LANES = 128
ROWS_PER_STEP = 4
SCATTER_BATCH = 8
VMEM_LIMIT = 56 * 1024 * 1024

_NT = (((1,), (1,)), ((), ()))
_TN = (((0,), (0,)), ((), ()))


def _params(*sem):
    return pltpu.CompilerParams(dimension_semantics=sem, vmem_limit_bytes=VMEM_LIMIT)


def _ln_mod(x, g, sc, sh):
    y = x * lax.rsqrt(jnp.mean(x * x, axis=-1, keepdims=True) + RMS_EPS)
    return (y * g) * (1.0 + sc) + sh


def _mod_kernel(c_ref, w_ref, b_ref, o_ref):
    c = c_ref[...]
    s = c * jax.nn.sigmoid(c)
    o_ref[...] = jnp.dot(s, w_ref[...], precision=HIGHEST, preferred_element_type=F32) + b_ref[...]


def _modulation(c, mod_w, mod_b):
    depth, d, n = mod_w.shape
    b = c.shape[0]
    tn = 1536
    return pl.pallas_call(
        _mod_kernel,
        grid=(depth, n // tn),
        in_specs=[
            pl.BlockSpec((b, d), lambda l, j: (0, 0)),
            pl.BlockSpec((None, d, tn), lambda l, j: (l, 0, j)),
            pl.BlockSpec((None, 1, tn), lambda l, j: (l, 0, j)),
        ],
        out_specs=pl.BlockSpec((None, b, tn), lambda l, j: (l, 0, j)),
        out_shape=jax.ShapeDtypeStruct((depth, b, n), F32),
        compiler_params=_params("parallel", "parallel"),
        name="adaln_mod",
    )(c, mod_w, mod_b.reshape(depth, 1, n))


def _row_spec(d, tiles_per_seq, k):
    return pl.BlockSpec((None, None, 1, d), lambda i: (i // tiles_per_seq, k, 0, 0))


def _inproj_kernel(x_ref, g_ref, sc_ref, sh_ref, w_ref, o_ref):
    h = _ln_mod(x_ref[...], g_ref[...], sc_ref[...], sh_ref[...]).astype(BF16)
    o_ref[...] = jnp.dot(h, w_ref[...], preferred_element_type=F32)


def _in_projection(x2, g, modl, w_bf, seq, tm=512):
    t, d = x2.shape
    n = w_bf.shape[1]
    tps = seq // tm
    return pl.pallas_call(
        _inproj_kernel,
        grid=(t // tm,),
        in_specs=[
            pl.BlockSpec((tm, d), lambda i: (i, 0)),
            pl.BlockSpec((1, d), lambda i: (0, 0)),
            _row_spec(d, tps, 1),
            _row_spec(d, tps, 0),
            pl.BlockSpec((d, n), lambda i: (0, 0)),
        ],
        out_specs=pl.BlockSpec((tm, n), lambda i: (i, 0)),
        out_shape=jax.ShapeDtypeStruct((t, n), F32),
        compiler_params=_params("parallel"),
        name="in_proj",
    )(x2, g, modl, modl, w_bf)


def _bias_table(rpb):
    heads, nr, nc = rpb.shape
    left = GRID_W - WIN_COLS
    w = jnp.pad(rpb, ((0, 0), (0, 0), (left, 2 * GRID_W - left - nc)))
    flat = jnp.tile(w, (1, 1, GRID_W))[..., :GRID_W * (2 * GRID_W - 1)]
    toe = flat.reshape(heads, nr, GRID_W, 2 * GRID_W - 1)[..., GRID_W - 1:]
    bands = jnp.stack([toe[:, WIN_ROWS - 1 - d:2 * WIN_ROWS - 1 - d] for d in range(WIN_ROWS)], axis=1)
    tbl = bands.transpose(0, 1, 3, 2, 4)
    qc = np.arange(GRID_W)[:, None, None]
    kc = np.arange(GRID_W)[None, None, :]
    cs = np.clip(qc - WIN_COLS // 2, 0, GRID_W - WIN_COLS)
    valid = np.broadcast_to((kc >= cs) & (kc < cs + WIN_COLS), (GRID_W, WIN_ROWS, GRID_W))
    tbl = jnp.where(valid[None, None], tbl, NEG_BIAS)
    return tbl.reshape(heads, WIN_ROWS, GRID_W, WIN_ROWS * GRID_W)


def _mixers_kernel(q_ref, k_ref, v_ref, bg_ref, cg_ref, xv_ref, qkg_ref, tbl_ref, cw_ref,
                   attn_ref, conv_ref, q0_s, q1_s, k_s, v_s, pad_s, *, rows):
    seq = q_ref.shape[0]
    lane = lax.broadcasted_iota(jnp.int32, (1, LANES), 1)
    first = lane < HEAD_DIM

    def head_norm(x, g):
        xx = x * x
        s0 = jnp.sum(jnp.where(first, xx, 0.0), axis=-1, keepdims=True)
        s1 = jnp.sum(jnp.where(first, 0.0, xx), axis=-1, keepdims=True)
        ms = jnp.where(first, s0, s1) * (1.0 / HEAD_DIM)
        return x * lax.rsqrt(ms + RMS_EPS) * g

    qn = head_norm(q_ref[...], qkg_ref[0:1, :]) * (1.0 / math.sqrt(HEAD_DIM))
    q0_s[...] = jnp.where(first, qn, 0.0).astype(BF16)
    q1_s[...] = jnp.where(first, 0.0, qn).astype(BF16)
    k_s[...] = head_norm(k_ref[...], qkg_ref[1:2, :]).astype(BF16)
    v_s[...] = v_ref[...].astype(BF16)

    kr = min(WIN_ROWS, rows)
    band = kr * GRID_W

    def one_row(r):
        rs = jnp.clip(r - kr // 2, 0, rows - kr)
        d = r - rs
        q_off = pl.multiple_of(r * GRID_W, GRID_W)
        k_off = pl.multiple_of(rs * GRID_W, GRID_W)
        kb = k_s[pl.ds(k_off, band), :]
        vb = v_s[pl.ds(k_off, band), :]
        outs = []
        for h, qs in enumerate((q0_s, q1_s)):
            s = lax.dot_general(qs[pl.ds(q_off, GRID_W), :], kb, _NT, preferred_element_type=F32)
            s = s + tbl_ref[h, d]
            p = jnp.exp(s - jnp.max(s, axis=-1, keepdims=True))
            l = jnp.sum(p, axis=-1, keepdims=True)
            outs.append(jnp.dot(p.astype(BF16), vb, preferred_element_type=F32) / l)
        attn_ref[pl.ds(q_off, GRID_W), :] = jnp.where(first, outs[0], outs[1])

    def rows_body(i, carry):
        for u in range(ROWS_PER_STEP):
            one_row(i * ROWS_PER_STEP + u)
        return carry

    lax.fori_loop(0, rows // ROWS_PER_STEP, rows_body, 0)

    zeros = jnp.zeros((8, LANES), F32)
    pad_s[0:8, :] = zeros
    pad_s[seq + 8:seq + 16, :] = zeros
    pad_s[8:seq + 8, :] = cg_ref[...] * xv_ref[...]
    z = (pad_s[7:seq + 7, :] * cw_ref[0:1, :] + pad_s[8:seq + 8, :] * cw_ref[1:2, :]
         + pad_s[9:seq + 9, :] * cw_ref[2:3, :])
    conv_ref[...] = bg_ref[...] * z


def _mixers(proj3, qk_g, rpb, conv_w):
    b, seq, n = proj3.shape
    width = n // 6
    nblk = width // LANES
    heads = width // HEAD_DIM
    rows = seq // GRID_W
    assert rows >= WIN_ROWS and rows % ROWS_PER_STEP == 0 and rpb.shape[0] == heads
    tbl = _bias_table(rpb)
    qkg = jnp.tile(qk_g, (1, LANES // HEAD_DIM))

    def col(k):
        return pl.BlockSpec((None, seq, LANES), lambda i, j, k=k: (i, 0, k * nblk + j))

    out_spec = pl.BlockSpec((None, seq, LANES), lambda i, j: (i, 0, j))
    hp = LANES // HEAD_DIM
    return pl.pallas_call(
        functools.partial(_mixers_kernel, rows=rows),
        grid=(b, nblk),
        in_specs=[col(0), col(1), col(2), col(3), col(4), col(5),
                  pl.BlockSpec((2, LANES), lambda i, j: (0, 0)),
                  pl.BlockSpec((hp, WIN_ROWS, GRID_W, WIN_ROWS * GRID_W), lambda i, j: (j, 0, 0, 0)),
                  pl.BlockSpec((conv_w.shape[0], LANES), lambda i, j: (0, j))],
        out_specs=[out_spec, out_spec],
        out_shape=[jax.ShapeDtypeStruct((b, seq, width), F32)] * 2,
        scratch_shapes=[pltpu.VMEM((seq, LANES), BF16)] * 4 + [pltpu.VMEM((seq + 16, LANES), F32)],
        compiler_params=_params("parallel", "parallel"),
        name="attn_conv",
    )(proj3, proj3, proj3, proj3, proj3, proj3, qkg, tbl, conv_w)


def _outproj_kernel(a_ref, c_ref, wa_ref, wc_ref, x_ref, gate_ref, o_ref):
    m = jnp.dot(a_ref[...].astype(BF16), wa_ref[...], preferred_element_type=F32)
    m = m + jnp.dot(c_ref[...].astype(BF16), wc_ref[...], preferred_element_type=F32)
    o_ref[...] = x_ref[...] + gate_ref[...] * m


def _out_projection(attn2, conv2, w_bf, x2, modl, seq, tm=512):
    t, d = x2.shape
    wa = attn2.shape[1]
    wc = conv2.shape[1]
    tps = seq // tm
    return pl.pallas_call(
        _outproj_kernel,
        grid=(t // tm,),
        in_specs=[
            pl.BlockSpec((tm, wa), lambda i: (i, 0)),
            pl.BlockSpec((tm, wc), lambda i: (i, 0)),
            pl.BlockSpec((wa, d), lambda i: (0, 0)),
            pl.BlockSpec((wc, d), lambda i: (1, 0)),
            pl.BlockSpec((tm, d), lambda i: (i, 0)),
            _row_spec(d, tps, 2),
        ],
        out_specs=pl.BlockSpec((tm, d), lambda i: (i, 0)),
        out_shape=jax.ShapeDtypeStruct((t, d), F32),
        compiler_params=_params("parallel"),
        name="out_proj",
    )(attn2, conv2, w_bf, w_bf, x2, modl)


def _dft_tables(n):
    jk = (np.arange(n)[:, None] * np.arange(n)[None, :]) % n
    ang = 2.0 * np.pi * jk.astype(np.float64) / n
    return np.cos(ang).astype(np.float32), np.sin(ang).astype(np.float32)


def _fnet_chan_kernel(x_ref, g_ref, sc_ref, sh_ref, cs_ref, a_ref, b_ref, *, gw):
    h = _ln_mod(x_ref[...], g_ref[...], sc_ref[...], sh_ref[...]).astype(BF16)
    for g in range(h.shape[1] // gw):
        ab = jnp.dot(h[:, g * gw:(g + 1) * gw], cs_ref[...], preferred_element_type=F32)
        a_ref[:, g * gw:(g + 1) * gw] = ab[:, :gw].astype(BF16)
        b_ref[:, g * gw:(g + 1) * gw] = ab[:, gw:].astype(BF16)


def _fnet_seq_kernel(cs_ref, ss_ref, a_ref, b_ref, w_ref, x_ref, gate_ref, o_ref, *, norm):
    f = jnp.dot(cs_ref[...], a_ref[...], preferred_element_type=F32)
    f = f - jnp.dot(ss_ref[...], b_ref[...], preferred_element_type=F32)
    m = jnp.dot((f * norm).astype(BF16), w_ref[...], preferred_element_type=F32)
    o_ref[...] = x_ref[...] + gate_ref[...] * m


def _fnet_layer(x3, g, modl, w_bf, tm=512):
    b, seq, d = x3.shape
    gw = d // FOURIER_GROUPS
    cc, sc = _dft_tables(gw)
    cs_chan = jnp.concatenate([jnp.asarray(cc), jnp.asarray(sc)], axis=1).astype(BF16)
    cseq, sseq = _dft_tables(seq)
    cseq = jnp.asarray(cseq).astype(BF16)
    sseq = jnp.asarray(sseq).astype(BF16)
    x2 = x3.reshape(b * seq, d)
    tps = seq // tm
    a, bm = pl.pallas_call(
        functools.partial(_fnet_chan_kernel, gw=gw),
        grid=(b * seq // tm,),
        in_specs=[
            pl.BlockSpec((tm, d), lambda i: (i, 0)),
            pl.BlockSpec((1, d), lambda i: (0, 0)),
            _row_spec(d, tps, 1),
            _row_spec(d, tps, 0),
            pl.BlockSpec((gw, 2 * gw), lambda i: (0, 0)),
        ],
        out_specs=[pl.BlockSpec((tm, d), lambda i: (i, 0))] * 2,
        out_shape=[jax.ShapeDtypeStruct((b * seq, d), BF16)] * 2,
        compiler_params=_params("parallel"),
        name="fnet_chan",
    )(x2, g, modl, modl, cs_chan)
    a3 = a.reshape(b, seq, d)
    b3 = bm.reshape(b, seq, d)
    norm = 1.0 / math.sqrt(seq * gw)
    return pl.pallas_call(
        functools.partial(_fnet_seq_kernel, norm=norm),
        grid=(b, seq // tm),
        in_specs=[
            pl.BlockSpec((tm, seq), lambda i, j: (j, 0)),
            pl.BlockSpec((tm, seq), lambda i, j: (j, 0)),
            pl.BlockSpec((None, seq, d), lambda i, j: (i, 0, 0)),
            pl.BlockSpec((None, seq, d), lambda i, j: (i, 0, 0)),
            pl.BlockSpec((d, d), lambda i, j: (0, 0)),
            pl.BlockSpec((None, tm, d), lambda i, j: (i, j, 0)),
            pl.BlockSpec((None, None, 1, d), lambda i, j: (i, 2, 0, 0)),
        ],
        out_specs=pl.BlockSpec((None, tm, d), lambda i, j: (i, j, 0)),
        out_shape=jax.ShapeDtypeStruct((b, seq, d), F32),
        compiler_params=_params("parallel", "parallel"),
        name="fnet_seq",
    )(cseq, sseq, a3, b3, w_bf, x3, modl)


def _router_kernel(x_ref, g_ref, sc_ref, sh_ref, rwt_ref, h_ref, aff_ref):
    h = _ln_mod(x_ref[...], g_ref[...], sc_ref[...], sh_ref[...])
    tm, d = h.shape
    nch = d // LANES
    for j in range(nch):
        h_ref[pl.ds(j, tm, stride=nch), :] = h[:, j * LANES:(j + 1) * LANES]
    logits = lax.dot_general(rwt_ref[...], h, _NT, precision=HIGHEST, preferred_element_type=F32)
    e = jnp.exp(logits - jnp.max(logits, axis=0, keepdims=True))
    aff_ref[...] = e / jnp.sum(e, axis=0, keepdims=True)


def _router(x3, g, modl, router_wt, tm=512):
    b, seq, d = x3.shape
    e = router_wt.shape[0]
    tps = seq // tm
    nch = d // LANES
    return pl.pallas_call(
        _router_kernel,
        grid=(b, tps),
        in_specs=[
            pl.BlockSpec((None, tm, d), lambda i, j: (i, j, 0)),
            pl.BlockSpec((1, d), lambda i, j: (0, 0)),
            pl.BlockSpec((None, None, 1, d), lambda i, j: (i, 4, 0, 0)),
            pl.BlockSpec((None, None, 1, d), lambda i, j: (i, 3, 0, 0)),
            pl.BlockSpec((e, d), lambda i, j: (0, 0)),
        ],
        out_specs=[pl.BlockSpec((None, tm * nch, LANES), lambda i, j: (i, j, 0)),
                   pl.BlockSpec((None, e, tm), lambda i, j: (i, 0, j))],
        out_shape=[jax.ShapeDtypeStruct((b, seq * nch, LANES), F32),
                   jax.ShapeDtypeStruct((b, e, seq), F32)],
        compiler_params=_params("parallel", "parallel"),
        name="router",
    )(x3, g, modl, modl, router_wt)


def _topk_kernel(aff_ref, tok_ref, gs_ref, tri_s, *, cap):
    n_exp, seq = aff_ref.shape

    @pl.when(pl.program_id(0) == 0)
    def _():
        r = lax.broadcasted_iota(jnp.int32, (seq, seq), 0)
        c = lax.broadcasted_iota(jnp.int32, (seq, seq), 1)
        tri_s[...] = jnp.where(r < c, 1.0, 0.0).astype(BF16)

    a = aff_ref[...]

    def search(i, t):
        cand = t | lax.shift_left(jnp.int32(1), 30 - i)
        cnt = jnp.sum((a >= pltpu.bitcast(cand, F32)).astype(jnp.int32), axis=1, keepdims=True)
        return jnp.where(cnt >= cap, cand, t)

    thr = lax.fori_loop(0, 31, search, jnp.zeros((a.shape[0], 1), jnp.int32))
    thr = pltpu.bitcast(thr, F32)
    gt = a > thr
    eq = a == thr
    need = cap - jnp.sum(gt.astype(jnp.int32), axis=1, keepdims=True)
    tri = tri_s[...]
    eq_before = jnp.dot(jnp.where(eq, 1.0, 0.0).astype(BF16), tri, preferred_element_type=F32)
    sel = gt | (eq & (eq_before < need.astype(F32)))
    before = jnp.dot(jnp.where(sel, 1.0, 0.0).astype(BF16), tri, preferred_element_type=F32)
    rank = jnp.where(sel, before.astype(jnp.int32), -1)

    slot = lax.broadcasted_iota(jnp.int32, (cap, seq), 0)
    tok_ids = lax.broadcasted_iota(jnp.int32, (1, seq), 1).astype(F32)
    pad = jnp.zeros((6, seq), F32)
    for e in range(n_exp):
        pick = jnp.where(slot == rank[e:e + 1, :], 1.0, 0.0)
        lhs = jnp.concatenate([tok_ids, a[e:e + 1, :], pad], axis=0)
        res = lax.dot_general(lhs, pick, _NT, precision=HIGHEST, preferred_element_type=F32)
        tok_ref[e:e + 1, :] = jnp.floor(res[0:1, :] + 0.5).astype(jnp.int32)
        gs_ref[e:e + 1, :] = res[1:2, :]


def _topk(aff, cap):
    b, e, seq = aff.shape
    out_spec = pl.BlockSpec((None, e, cap), lambda i: (i, 0, 0))
    return pl.pallas_call(
        functools.partial(_topk_kernel, cap=cap),
        grid=(b,),
        in_specs=[pl.BlockSpec((None, e, seq), lambda i: (i, 0, 0))],
        out_specs=[out_spec, out_spec],
        out_shape=[jax.ShapeDtypeStruct((b, e, cap), jnp.int32), jax.ShapeDtypeStruct((b, e, cap), F32)],
        scratch_shapes=[pltpu.VMEM((seq, seq), BF16)],
        compiler_params=_params("arbitrary"),
        name="topk",
    )(aff)


def _expert_kernel(tok_ref, gs_ref, h_ref, wg_ref, wu_ref, wd_ref, o_ref, x_s, y_s, *, cap, stride):
    nch = x_s.shape[0] // stride

    @pl.when(pl.program_id(1) == 0)
    def _():
        o_ref[...] = jnp.zeros_like(o_ref)

    def slab(i):
        return pl.ds(pl.multiple_of(tok_ref[0, i] * nch, nch), nch)

    for i in range(cap):
        x_s[pl.ds(i, nch, stride=stride), :] = h_ref[slab(i), :]
    xe = jnp.concatenate([x_s[j * stride:j * stride + cap, :] for j in range(nch)], axis=-1)
    xe = xe.astype(BF16)
    a = jnp.dot(xe, wg_ref[...], preferred_element_type=F32)
    u = jnp.dot(xe, wu_ref[...], preferred_element_type=F32)
    hm = (a * jax.nn.sigmoid(a) * u).astype(BF16)
    y = jnp.dot(hm, wd_ref[...], preferred_element_type=F32)
    for j in range(nch):
        y_s[j * stride:j * stride + cap, :] = y[:, j * LANES:(j + 1) * LANES]
    for i0 in range(0, cap, SCATTER_BATCH):
        vals = [o_ref[slab(i), :] + y_s[pl.ds(i, nch, stride=stride), :] * gs_ref[0, i]
                for i in range(i0, i0 + SCATTER_BATCH)]
        for i, v in zip(range(i0, i0 + SCATTER_BATCH), vals):
            o_ref[slab(i), :] = v


def _experts(h_slab, tok, gs, wg, wu, wd):
    b, rows, _ = h_slab.shape
    e, d, ff = wg.shape
    cap = tok.shape[-1]
    nch = d // LANES
    assert cap % SCATTER_BATCH == 0
    stride = cap + 8
    smem_row = pl.BlockSpec((None, None, 1, cap), lambda i, j: (i, j, 0, 0), memory_space=pltpu.SMEM)
    return pl.pallas_call(
        functools.partial(_expert_kernel, cap=cap, stride=stride),
        grid=(b, e),
        in_specs=[
            smem_row, smem_row,
            pl.BlockSpec((None, rows, LANES), lambda i, j: (i, 0, 0)),
            pl.BlockSpec((None, d, ff), lambda i, j: (j, 0, 0)),
            pl.BlockSpec((None, d, ff), lambda i, j: (j, 0, 0)),
            pl.BlockSpec((None, ff, d), lambda i, j: (j, 0, 0)),
        ],
        out_specs=pl.BlockSpec((None, rows, LANES), lambda i, j: (i, 0, 0)),
        out_shape=jax.ShapeDtypeStruct((b, rows, LANES), F32),
        scratch_shapes=[pltpu.VMEM((nch * stride, LANES), F32)] * 2,
        compiler_params=_params("parallel", "arbitrary"),
        name="experts",
    )(tok.reshape(b, e, 1, cap), gs.reshape(b, e, 1, cap), h_slab, wg, wu, wd)


def _residual_kernel(m_ref, x_ref, gate_ref, o_ref):
    tm, d = x_ref.shape
    nch = d // LANES
    for j in range(nch):
        lanes = slice(j * LANES, (j + 1) * LANES)
        o_ref[:, lanes] = x_ref[:, lanes] + gate_ref[:, lanes] * m_ref[pl.ds(j, tm, stride=nch), :]


def _residual(m_slab, x3, modl, tm=512):
    b, seq, d = x3.shape
    nch = d // LANES
    return pl.pallas_call(
        _residual_kernel,
        grid=(b, seq // tm),
        in_specs=[
            pl.BlockSpec((None, tm * nch, LANES), lambda i, j: (i, j, 0)),
            pl.BlockSpec((None, tm, d), lambda i, j: (i, j, 0)),
            pl.BlockSpec((None, None, 1, d), lambda i, j: (i, 5, 0, 0)),
        ],
        out_specs=pl.BlockSpec((None, tm, d), lambda i, j: (i, j, 0)),
        out_shape=jax.ShapeDtypeStruct((b, seq, d), F32),
        compiler_params=_params("parallel", "parallel"),
        name="moe_residual",
    )(m_slab, x3, modl)


def _moe_layer(x3, g, modl, router_w, wg, wu, wd):
    seq = x3.shape[1]
    cap = max(1, CAPACITY_FACTOR * seq // router_w.shape[1])
    h_slab, aff = _router(x3, g, modl, router_w.T)
    tok, gs = _topk(aff, cap)
    m_slab = _experts(h_slab, tok, gs, wg.astype(BF16), wu.astype(BF16), wd.astype(BF16))
    return _residual(m_slab, x3, modl)


def kernel(x, c, mod_w, mod_b, norm_g, mix_w_in, qk_g, rpb, conv_w, mix_w_out, fnet_w_out,
           router_w, exp_w_gate, exp_w_up, exp_w_down):
    b, seq, d = x.shape
    depth = mod_w.shape[0]
    mod = _modulation(c, mod_w, mod_b).reshape(depth, b, N_MOD, 1, d)
    for l in range(depth):
        modl = mod[l]
        g1 = norm_g[l, 0].reshape(1, d)
        g2 = norm_g[l, 1].reshape(1, d)
        if l % 2 == 0:
            j = l // 2
            x2 = x.reshape(b * seq, d)
            proj = _in_projection(x2, g1, modl, mix_w_in[j].astype(BF16), seq)
            attn, conv = _mixers(proj.reshape(b, seq, -1), qk_g[j], rpb[j], conv_w[j])
            width = attn.shape[-1]
            x = _out_projection(attn.reshape(b * seq, width), conv.reshape(b * seq, width),
                                mix_w_out[j].astype(BF16), x2, modl, seq).reshape(b, seq, d)
        else:
            x = _fnet_layer(x, g1, modl, fnet_w_out[l // 2].astype(BF16))
        x = _moe_layer(x, g2, modl, router_w[l], exp_w_gate[l], exp_w_up[l], exp_w_down[l])
    return x
```

```python
import functools
import math

import numpy as np
import jax
import jax.numpy as jnp
from jax import lax
from jax.experimental import pallas as pl
from jax.experimental.pallas import tpu as pltpu

F32 = jnp.float32
BF16 = jnp.bfloat16
HIGHEST = lax.Precision.HIGHEST

GRID_W = 64
WIN_ROWS = 8
WIN_COLS = 16
HEAD_DIM = 64
FOURIER_GROUPS = 4
N_MOD = 6
CAPACITY_FACTOR = 2
RMS_EPS = 1e-6
NEG_BIAS = -1e30
LANES = 128
ROWS_PER_STEP = WIN_ROWS // 2
BAND_ROWS = WIN_ROWS + ROWS_PER_STEP
SCATTER_BATCH = 8
VMEM_LIMIT = 56 * 1024 * 1024

_NT = (((1,), (1,)), ((), ()))
_TN = (((0,), (0,)), ((), ()))


def _params(*sem):
    return pltpu.CompilerParams(dimension_semantics=sem, vmem_limit_bytes=VMEM_LIMIT)


def _ln_mod(x, g, sc, sh):
    y = x * lax.rsqrt(jnp.mean(x * x, axis=-1, keepdims=True) + RMS_EPS)
    return (y * g) * (1.0 + sc) + sh


def _mod_kernel(c_ref, w_ref, b_ref, o_ref):
    c = c_ref[...]
    s = c * jax.nn.sigmoid(c)
    o_ref[...] = jnp.dot(s, w_ref[...], precision=HIGHEST, preferred_element_type=F32) + b_ref[...]


def _modulation(c, mod_w, mod_b):
    depth, d, n = mod_w.shape
    b = c.shape[0]
    tn = 1536
    return pl.pallas_call(
        _mod_kernel,
        grid=(depth, n // tn),
        in_specs=[
            pl.BlockSpec((b, d), lambda l, j: (0, 0)),
            pl.BlockSpec((None, d, tn), lambda l, j: (l, 0, j)),
            pl.BlockSpec((None, 1, tn), lambda l, j: (l, 0, j)),
        ],
        out_specs=pl.BlockSpec((None, b, tn), lambda l, j: (l, 0, j)),
        out_shape=jax.ShapeDtypeStruct((depth, b, n), F32),
        compiler_params=_params("parallel", "parallel"),
        name="adaln_mod",
    )(c, mod_w, mod_b.reshape(depth, 1, n))


def _row_spec(d, tiles_per_seq, k):
    return pl.BlockSpec((None, None, 1, d), lambda i: (i // tiles_per_seq, k, 0, 0))


def _inproj_kernel(x_ref, g_ref, sc_ref, sh_ref, w_ref, o_ref):
    h = _ln_mod(x_ref[...], g_ref[...], sc_ref[...], sh_ref[...]).astype(BF16)
    o_ref[...] = jnp.dot(h, w_ref[...], preferred_element_type=F32)


def _in_projection(x2, g, modl, w_bf, seq, tm=512):
    t, d = x2.shape
    n = w_bf.shape[1]
    tps = seq // tm
    return pl.pallas_call(
        _inproj_kernel,
        grid=(t // tm,),
        in_specs=[
            pl.BlockSpec((tm, d), lambda i: (i, 0)),
            pl.BlockSpec((1, d), lambda i: (0, 0)),
            _row_spec(d, tps, 1),
            _row_spec(d, tps, 0),
            pl.BlockSpec((d, n), lambda i: (0, 0)),
        ],
        out_specs=pl.BlockSpec((tm, n), lambda i: (i, 0)),
        out_shape=jax.ShapeDtypeStruct((t, n), F32),
        compiler_params=_params("parallel"),
        name="in_proj",
    )(x2, g, modl, modl, w_bf)


def _bias_table(rpb):
    heads, nr, nc = rpb.shape
    assert nr == 2 * WIN_ROWS - 1 and nc == 2 * WIN_COLS - 1
    left = GRID_W - WIN_COLS
    w = jnp.pad(rpb, ((0, 0), (0, 0), (left, 2 * GRID_W - left - nc)))
    flat = jnp.tile(w, (1, 1, GRID_W))[..., :GRID_W * (2 * GRID_W - 1)]
    toe = flat.reshape(heads, nr, GRID_W, 2 * GRID_W - 1)[..., GRID_W - 1:]
    qc = np.arange(GRID_W)[:, None]
    kc = np.arange(GRID_W)[None, :]
    cs = np.clip(qc - WIN_COLS // 2, 0, GRID_W - WIN_COLS)
    toe = jnp.where((kc >= cs) & (kc < cs + WIN_COLS), toe, NEG_BIAS)
    place = [[(u, 0) for u in range(ROWS_PER_STEP)],
             [(WIN_ROWS // 2 + u, u) for u in range(ROWS_PER_STEP)],
             [(BAND_ROWS - ROWS_PER_STEP + u, BAND_ROWS - WIN_ROWS) for u in range(ROWS_PER_STEP)]]
    cases = []
    for per_u in place:
        blocks = []
        for off, lo in per_u:
            first = lo - off + WIN_ROWS - 1
            blk = jnp.pad(toe[:, first:first + WIN_ROWS],
                          ((0, 0), (lo, BAND_ROWS - WIN_ROWS - lo), (0, 0), (0, 0)),
                          constant_values=NEG_BIAS)
            blocks.append(blk.transpose(0, 2, 1, 3))
        cases.append(jnp.stack(blocks, axis=1))
    tbl = jnp.stack(cases, axis=1)
    return tbl.reshape(heads, 3, ROWS_PER_STEP * GRID_W, BAND_ROWS * GRID_W)


def _mixers_kernel(q_ref, k_ref, v_ref, bg_ref, cg_ref, xv_ref, qkg_ref, tbl_ref, cw_ref,
                   attn_ref, conv_ref, q0_s, q1_s, k_s, v_s, pad_s, *, rows):
    seq = q_ref.shape[0]
    lane = lax.broadcasted_iota(jnp.int32, (1, LANES), 1)
    first = lane < HEAD_DIM

    def head_norm(x, g):
        xx = x * x
        s0 = jnp.sum(jnp.where(first, xx, 0.0), axis=-1, keepdims=True)
        s1 = jnp.sum(jnp.where(first, 0.0, xx), axis=-1, keepdims=True)
        ms = jnp.where(first, s0, s1) * (1.0 / HEAD_DIM)
        return x * lax.rsqrt(ms + RMS_EPS) * g

    qn = head_norm(q_ref[...], qkg_ref[0:1, :]) * (1.0 / math.sqrt(HEAD_DIM))
    q0_s[...] = jnp.where(first, qn, 0.0).astype(BF16)
    q1_s[...] = jnp.where(first, 0.0, qn).astype(BF16)
    k_s[...] = head_norm(k_ref[...], qkg_ref[1:2, :]).astype(BF16)
    v_s[...] = v_ref[...].astype(BF16)

    n_groups = rows // ROWS_PER_STEP
    nq = ROWS_PER_STEP * GRID_W
    for g in range(n_groups):
        b0 = min(max(g * ROWS_PER_STEP - WIN_ROWS // 2, 0), rows - BAND_ROWS)
        case = 0 if g == 0 else (2 if g == n_groups - 1 else 1)
        kb = k_s[b0 * GRID_W:(b0 + BAND_ROWS) * GRID_W, :]
        vb = v_s[b0 * GRID_W:(b0 + BAND_ROWS) * GRID_W, :]
        outs = []
        for h, qs in enumerate((q0_s, q1_s)):
            s = lax.dot_general(qs[g * nq:(g + 1) * nq, :], kb, _NT, preferred_element_type=F32)
            s = s + tbl_ref[h, case]
            p = jnp.exp(s - jnp.max(s, axis=-1, keepdims=True))
            l = jnp.sum(p, axis=-1, keepdims=True)
            outs.append(jnp.dot(p.astype(BF16), vb, preferred_element_type=F32) / l)
        attn_ref[g * nq:(g + 1) * nq, :] = jnp.where(first, outs[0], outs[1])

    zeros = jnp.zeros((8, LANES), F32)
    pad_s[0:8, :] = zeros
    pad_s[seq + 8:seq + 16, :] = zeros
    pad_s[8:seq + 8, :] = cg_ref[...] * xv_ref[...]
    z = (pad_s[7:seq + 7, :] * cw_ref[0:1, :] + pad_s[8:seq + 8, :] * cw_ref[1:2, :]
         + pad_s[9:seq + 9, :] * cw_ref[2:3, :])
    conv_ref[...] = bg_ref[...] * z


def _mixers(proj3, qk_g, rpb, conv_w):
    b, seq, n = proj3.shape
    width = n // 6
    nblk = width // LANES
    heads = width // HEAD_DIM
    rows = seq // GRID_W
    assert rows >= BAND_ROWS + ROWS_PER_STEP and rows % ROWS_PER_STEP == 0 and rpb.shape[0] == heads
    tbl = _bias_table(rpb)
    qkg = jnp.tile(qk_g, (1, LANES // HEAD_DIM))

    def col(k):
        return pl.BlockSpec((None, seq, LANES), lambda i, j, k=k: (i, 0, k * nblk + j))

    out_spec = pl.BlockSpec((None, seq, LANES), lambda i, j: (i, 0, j))
    hp = LANES // HEAD_DIM
    return pl.pallas_call(
        functools.partial(_mixers_kernel, rows=rows),
        grid=(b, nblk),
        in_specs=[col(0), col(1), col(2), col(3), col(4), col(5),
                  pl.BlockSpec((2, LANES), lambda i, j: (0, 0)),
                  pl.BlockSpec((hp,) + tbl.shape[1:], lambda i, j: (j, 0, 0, 0)),
                  pl.BlockSpec((conv_w.shape[0], LANES), lambda i, j: (0, j))],
        out_specs=[out_spec, out_spec],
        out_shape=[jax.ShapeDtypeStruct((b, seq, width), F32)] * 2,
        scratch_shapes=[pltpu.VMEM((seq, LANES), BF16)] * 4 + [pltpu.VMEM((seq + 16, LANES), F32)],
        compiler_params=_params("parallel", "parallel"),
        name="attn_conv",
    )(proj3, proj3, proj3, proj3, proj3, proj3, qkg, tbl, conv_w)


def _outproj_kernel(a_ref, c_ref, wa_ref, wc_ref, x_ref, gate_ref, o_ref):
    m = jnp.dot(a_ref[...].astype(BF16), wa_ref[...], preferred_element_type=F32)
    m = m + jnp.dot(c_ref[...].astype(BF16), wc_ref[...], preferred_element_type=F32)
    o_ref[...] = x_ref[...] + gate_ref[...] * m


def _out_projection(attn2, conv2, w_bf, x2, modl, seq, tm=512):
    t, d = x2.shape
    wa = attn2.shape[1]
    wc = conv2.shape[1]
    tps = seq // tm
    return pl.pallas_call(
        _outproj_kernel,
        grid=(t // tm,),
        in_specs=[
            pl.BlockSpec((tm, wa), lambda i: (i, 0)),
            pl.BlockSpec((tm, wc), lambda i: (i, 0)),
            pl.BlockSpec((wa, d), lambda i: (0, 0)),
            pl.BlockSpec((wc, d), lambda i: (1, 0)),
            pl.BlockSpec((tm, d), lambda i: (i, 0)),
            _row_spec(d, tps, 2),
        ],
        out_specs=pl.BlockSpec((tm, d), lambda i: (i, 0)),
        out_shape=jax.ShapeDtypeStruct((t, d), F32),
        compiler_params=_params("parallel"),
        name="out_proj",
    )(attn2, conv2, w_bf, w_bf, x2, modl)


def _dft_tables(n):
    jk = (np.arange(n)[:, None] * np.arange(n)[None, :]) % n
    ang = 2.0 * np.pi * jk.astype(np.float64) / n
    return np.cos(ang).astype(np.float32), np.sin(ang).astype(np.float32)


def _fnet_chan_kernel(x_ref, g_ref, sc_ref, sh_ref, cs_ref, a_ref, b_ref, *, gw):
    h = _ln_mod(x_ref[...], g_ref[...], sc_ref[...], sh_ref[...]).astype(BF16)
    for g in range(h.shape[1] // gw):
        ab = jnp.dot(h[:, g * gw:(g + 1) * gw], cs_ref[...], preferred_element_type=F32)
        a_ref[:, g * gw:(g + 1) * gw] = ab[:, :gw].astype(BF16)
        b_ref[:, g * gw:(g + 1) * gw] = ab[:, gw:].astype(BF16)


def _fnet_seq_kernel(cs_ref, ss_ref, a_ref, b_ref, w_ref, x_ref, gate_ref, o_ref, *, norm):
    f = jnp.dot(cs_ref[...], a_ref[...], preferred_element_type=F32)
    f = f - jnp.dot(ss_ref[...], b_ref[...], preferred_element_type=F32)
    m = jnp.dot((f * norm).astype(BF16), w_ref[...], preferred_element_type=F32)
    o_ref[...] = x_ref[...] + gate_ref[...] * m


def _fnet_layer(x3, g, modl, w_bf, tm=512):
    b, seq, d = x3.shape
    gw = d // FOURIER_GROUPS
    cc, sc = _dft_tables(gw)
    cs_chan = jnp.concatenate([jnp.asarray(cc), jnp.asarray(sc)], axis=1).astype(BF16)
    cseq, sseq = _dft_tables(seq)
    cseq = jnp.asarray(cseq).astype(BF16)
    sseq = jnp.asarray(sseq).astype(BF16)
    x2 = x3.reshape(b * seq, d)
    tps = seq // tm
    a, bm = pl.pallas_call(
        functools.partial(_fnet_chan_kernel, gw=gw),
        grid=(b * seq // tm,),
        in_specs=[
            pl.BlockSpec((tm, d), lambda i: (i, 0)),
            pl.BlockSpec((1, d), lambda i: (0, 0)),
            _row_spec(d, tps, 1),
            _row_spec(d, tps, 0),
            pl.BlockSpec((gw, 2 * gw), lambda i: (0, 0)),
        ],
        out_specs=[pl.BlockSpec((tm, d), lambda i: (i, 0))] * 2,
        out_shape=[jax.ShapeDtypeStruct((b * seq, d), BF16)] * 2,
        compiler_params=_params("parallel"),
        name="fnet_chan",
    )(x2, g, modl, modl, cs_chan)
    a3 = a.reshape(b, seq, d)
    b3 = bm.reshape(b, seq, d)
    norm = 1.0 / math.sqrt(seq * gw)
    return pl.pallas_call(
        functools.partial(_fnet_seq_kernel, norm=norm),
        grid=(b, seq // tm),
        in_specs=[
            pl.BlockSpec((tm, seq), lambda i, j: (j, 0)),
            pl.BlockSpec((tm, seq), lambda i, j: (j, 0)),
            pl.BlockSpec((None, seq, d), lambda i, j: (i, 0, 0)),
            pl.BlockSpec((None, seq, d), lambda i, j: (i, 0, 0)),
            pl.BlockSpec((d, d), lambda i, j: (0, 0)),
            pl.BlockSpec((None, tm, d), lambda i, j: (i, j, 0)),
            pl.BlockSpec((None, None, 1, d), lambda i, j: (i, 2, 0, 0)),
        ],
        out_specs=pl.BlockSpec((None, tm, d), lambda i, j: (i, j, 0)),
        out_shape=jax.ShapeDtypeStruct((b, seq, d), F32),
        compiler_params=_params("parallel", "parallel"),
        name="fnet_seq",
    )(cseq, sseq, a3, b3, w_bf, x3, modl)


def _router_kernel(x_ref, g_ref, sc_ref, sh_ref, rwt_ref, h_ref, aff_ref):
    h = _ln_mod(x_ref[...], g_ref[...], sc_ref[...], sh_ref[...])
    tm, d = h.shape
    nch = d // LANES
    for j in range(nch):
        h_ref[pl.ds(j, tm, stride=nch), :] = h[:, j * LANES:(j + 1) * LANES]
    logits = lax.dot_general(rwt_ref[...], h, _NT, precision=HIGHEST, preferred_element_type=F32)
    e = jnp.exp(logits - jnp.max(logits, axis=0, keepdims=True))
    aff_ref[...] = e / jnp.sum(e, axis=0, keepdims=True)


def _router(x3, g, modl, router_wt, tm=512):
    b, seq, d = x3.shape
    e = router_wt.shape[0]
    tps = seq // tm
    nch = d // LANES
    return pl.pallas_call(
        _router_kernel,
        grid=(b, tps),
        in_specs=[
            pl.BlockSpec((None, tm, d), lambda i, j: (i, j, 0)),
            pl.BlockSpec((1, d), lambda i, j: (0, 0)),
            pl.BlockSpec((None, None, 1, d), lambda i, j: (i, 4, 0, 0)),
            pl.BlockSpec((None, None, 1, d), lambda i, j: (i, 3, 0, 0)),
            pl.BlockSpec((e, d), lambda i, j: (0, 0)),
        ],
        out_specs=[pl.BlockSpec((None, tm * nch, LANES), lambda i, j: (i, j, 0)),
                   pl.BlockSpec((None, e, tm), lambda i, j: (i, 0, j))],
        out_shape=[jax.ShapeDtypeStruct((b, seq * nch, LANES), F32),
                   jax.ShapeDtypeStruct((b, e, seq), F32)],
        compiler_params=_params("parallel", "parallel"),
        name="router",
    )(x3, g, modl, modl, router_wt)


def _topk_kernel(aff_ref, tok_ref, gs_ref, tri_s, *, cap):
    n_exp, seq = aff_ref.shape

    @pl.when(pl.program_id(0) == 0)
    def _():
        r = lax.broadcasted_iota(jnp.int32, (seq, seq), 0)
        c = lax.broadcasted_iota(jnp.int32, (seq, seq), 1)
        tri_s[...] = jnp.where(r < c, 1.0, 0.0).astype(BF16)

    a = aff_ref[...]

    def search(i, t):
        cand = t | lax.shift_left(jnp.int32(1), 30 - i)
        cnt = jnp.sum((a >= pltpu.bitcast(cand, F32)).astype(jnp.int32), axis=1, keepdims=True)
        return jnp.where(cnt >= cap, cand, t)

    thr = lax.fori_loop(0, 31, search, jnp.zeros((a.shape[0], 1), jnp.int32))
    thr = pltpu.bitcast(thr, F32)
    gt = a > thr
    eq = a == thr
    need = cap - jnp.sum(gt.astype(jnp.int32), axis=1, keepdims=True)
    tri = tri_s[...]
    eq_before = jnp.dot(jnp.where(eq, 1.0, 0.0).astype(BF16), tri, preferred_element_type=F32)
    sel = gt | (eq & (eq_before < need.astype(F32)))
    before = jnp.dot(jnp.where(sel, 1.0, 0.0).astype(BF16), tri, preferred_element_type=F32)
    rank = jnp.where(sel, before.astype(jnp.int32), -1)

    slot = lax.broadcasted_iota(jnp.int32, (cap, seq), 0)
    tok_ids = lax.broadcasted_iota(jnp.int32, (1, seq), 1)
    t_hi = lax.shift_right_logical(tok_ids, 6).astype(F32)
    t_lo = (tok_ids & 63).astype(F32)
    pad = jnp.zeros((11, seq), F32)
    for e in range(n_exp):
        pick = jnp.where(slot == rank[e:e + 1, :], 1.0, 0.0).astype(BF16)
        g = a[e:e + 1, :]
        g0 = g.astype(BF16).astype(F32)
        g1 = (g - g0).astype(BF16).astype(F32)
        g2 = (g - g0) - g1
        lhs = jnp.concatenate([t_hi, t_lo, g0, g1, g2, pad], axis=0).astype(BF16)
        res = lax.dot_general(lhs, pick, _NT, preferred_element_type=F32)
        tok_ref[e:e + 1, :] = (res[0:1, :] * 64.0 + res[1:2, :]).astype(jnp.int32)
        gs_ref[e:e + 1, :] = (res[2:3, :] + res[3:4, :]) + res[4:5, :]


def _topk(aff, cap):
    b, e, seq = aff.shape
    out_spec = pl.BlockSpec((None, e, cap), lambda i: (i, 0, 0))
    return pl.pallas_call(
        functools.partial(_topk_kernel, cap=cap),
        grid=(b,),
        in_specs=[pl.BlockSpec((None, e, seq), lambda i: (i, 0, 0))],
        out_specs=[out_spec, out_spec],
        out_shape=[jax.ShapeDtypeStruct((b, e, cap), jnp.int32), jax.ShapeDtypeStruct((b, e, cap), F32)],
        scratch_shapes=[pltpu.VMEM((seq, seq), BF16)],
        compiler_params=_params("arbitrary"),
        name="topk",
    )(aff)


def _gather_kernel(tok_ref, h_ref, o_ref, x_s, *, stride):
    n_exp, cap = tok_ref.shape
    nch = x_s.shape[0] // stride

    def expert(e, carry):
        for i in range(cap):
            t = pl.multiple_of(tok_ref[e, i] * nch, nch)
            x_s[pl.ds(i, nch, stride=stride), :] = h_ref[pl.ds(t, nch), :]
        xe = jnp.concatenate([x_s[j * stride:j * stride + cap, :] for j in range(nch)], axis=-1)
        o_ref[e] = xe.astype(BF16)
        return carry

    lax.fori_loop(0, n_exp, expert, 0)


def _gather_tokens(h_slab, tok, d):
    b, rows, _ = h_slab.shape
    _, e, cap = tok.shape
    nch = d // LANES
    stride = cap + 8
    return pl.pallas_call(
        functools.partial(_gather_kernel, stride=stride),
        grid=(b,),
        in_specs=[
            pl.BlockSpec((None, e, cap), lambda i: (i, 0, 0), memory_space=pltpu.SMEM),
            pl.BlockSpec((None, rows, LANES), lambda i: (i, 0, 0)),
        ],
        out_specs=pl.BlockSpec((e, None, cap, d), lambda i: (0, i, 0, 0)),
        out_shape=jax.ShapeDtypeStruct((e, b, cap, d), BF16),
        scratch_shapes=[pltpu.VMEM((nch * stride, LANES), F32)],
        compiler_params=_params("parallel"),
        name="moe_gather",
    )(tok, h_slab)


def _ffn_kernel(x_ref, wg_ref, wu_ref, wd_ref, y_ref, wg_s, wu_s, wd_s):
    @pl.when(pl.program_id(1) == 0)
    def _():
        wg_s[...] = wg_ref[...].astype(BF16)
        wu_s[...] = wu_ref[...].astype(BF16)
        wd_s[...] = wd_ref[...].astype(BF16)

    x = x_ref[...]
    a = jnp.dot(x, wg_s[...], preferred_element_type=F32)
    u = jnp.dot(x, wu_s[...], preferred_element_type=F32)
    hm = (a * jax.nn.sigmoid(a) * u).astype(BF16)
    y_ref[...] = jnp.dot(hm, wd_s[...], preferred_element_type=F32)


def _expert_ffn(xe, wg, wu, wd, tm=512):
    e, m, d = xe.shape
    ff = wg.shape[2]
    return pl.pallas_call(
        _ffn_kernel,
        grid=(e, m // tm),
        in_specs=[
            pl.BlockSpec((None, tm, d), lambda i, j: (i, j, 0)),
            pl.BlockSpec((None, d, ff), lambda i, j: (i, 0, 0)),
            pl.BlockSpec((None, d, ff), lambda i, j: (i, 0, 0)),
            pl.BlockSpec((None, ff, d), lambda i, j: (i, 0, 0)),
        ],
        out_specs=pl.BlockSpec((None, tm, d), lambda i, j: (i, j, 0)),
        out_shape=jax.ShapeDtypeStruct((e, m, d), F32),
        scratch_shapes=[pltpu.VMEM((d, ff), BF16), pltpu.VMEM((d, ff), BF16), pltpu.VMEM((ff, d), BF16)],
        compiler_params=_params("parallel", "arbitrary"),
        name="moe_ffn",
    )(xe, wg, wu, wd)


def _scatter_kernel(tok_ref, gs_ref, y_ref, x_ref, gate_ref, o_ref, acc_s, y_s, *, stride):
    cap = tok_ref.shape[1]
    seq, d = x_ref.shape
    nch = d // LANES
    e = pl.program_id(1)

    @pl.when(e == 0)
    def _():
        acc_s[...] = jnp.zeros_like(acc_s)

    for j in range(nch):
        y_s[j * stride:j * stride + cap, :] = y_ref[:, j * LANES:(j + 1) * LANES]

    def slab(i):
        return pl.ds(pl.multiple_of(tok_ref[0, i] * nch, nch), nch)

    for i0 in range(0, cap, SCATTER_BATCH):
        vals = [acc_s[slab(i), :] + y_s[pl.ds(i, nch, stride=stride), :] * gs_ref[0, i]
                for i in range(i0, i0 + SCATTER_BATCH)]
        for i, v in zip(range(i0, i0 + SCATTER_BATCH), vals):
            acc_s[slab(i), :] = v

    @pl.when(e == pl.num_programs(1) - 1)
    def _():
        for j in range(nch):
            lanes = slice(j * LANES, (j + 1) * LANES)
            o_ref[:, lanes] = x_ref[:, lanes] + gate_ref[:, lanes] * acc_s[pl.ds(j, seq, stride=nch), :]


def _scatter_residual(y4, tok, gs, x3, modl):
    e, b, cap, d = y4.shape
    seq = x3.shape[1]
    nch = d // LANES
    assert cap % SCATTER_BATCH == 0
    stride = cap + 8
    smem_row = pl.BlockSpec((None, None, 1, cap), lambda i, j: (i, j, 0, 0), memory_space=pltpu.SMEM)
    return pl.pallas_call(
        functools.partial(_scatter_kernel, stride=stride),
        grid=(b, e),
        in_specs=[
            smem_row, smem_row,
            pl.BlockSpec((None, None, cap, d), lambda i, j: (j, i, 0, 0)),
            pl.BlockSpec((None, seq, d), lambda i, j: (i, 0, 0)),
            pl.BlockSpec((None, None, 1, d), lambda i, j: (i, 5, 0, 0)),
        ],
        out_specs=pl.BlockSpec((None, seq, d), lambda i, j: (i, 0, 0)),
        out_shape=jax.ShapeDtypeStruct((b, seq, d), F32),
        scratch_shapes=[pltpu.VMEM((seq * nch, LANES), F32), pltpu.VMEM((nch * stride, LANES), F32)],
        compiler_params=_params("parallel", "arbitrary"),
        name="moe_scatter",
    )(tok.reshape(b, e, 1, cap), gs.reshape(b, e, 1, cap), y4, x3, modl)


def _moe_layer(x3, g, modl, router_w, wg, wu, wd):
    b, seq, d = x3.shape
    e = router_w.shape[1]
    cap = max(1, CAPACITY_FACTOR * seq // e)
    h_slab, aff = _router(x3, g, modl, router_w.T)
    tok, gs = _topk(aff, cap)
    xe = _gather_tokens(h_slab, tok, d)
    y = _expert_ffn(xe.reshape(e, b * cap, d), wg, wu, wd, tm=min(512, b * cap))
    return _scatter_residual(y.reshape(e, b, cap, d), tok, gs, x3, modl)


def kernel(x, c, mod_w, mod_b, norm_g, mix_w_in, qk_g, rpb, conv_w, mix_w_out, fnet_w_out,
           router_w, exp_w_gate, exp_w_up, exp_w_down):
    b, seq, d = x.shape
    depth = mod_w.shape[0]
    mod = _modulation(c, mod_w, mod_b).reshape(depth, b, N_MOD, 1, d)
    for l in range(depth):
        modl = mod[l]
        g1 = norm_g[l, 0].reshape(1, d)
        g2 = norm_g[l, 1].reshape(1, d)
        if l % 2 == 0:
            j = l // 2
            x2 = x.reshape(b * seq, d)
            proj = _in_projection(x2, g1, modl, mix_w_in[j].astype(BF16), seq)
            attn, conv = _mixers(proj.reshape(b, seq, -1), qk_g[j], rpb[j], conv_w[j])
            width = attn.shape[-1]
            x = _out_projection(attn.reshape(b * seq, width), conv.reshape(b * seq, width),
                                mix_w_out[j].astype(BF16), x2, modl, seq).reshape(b, seq, d)
        else:
            x = _fnet_layer(x, g1, modl, fnet_w_out[l // 2].astype(BF16))
        x = _moe_layer(x, g2, modl, router_w[l], exp_w_gate[l], exp_w_up[l], exp_w_down[l])
    return x
```

```python
import functools
import math

import numpy as np
import jax
import jax.numpy as jnp
from jax import lax
from jax.experimental import pallas as pl
from jax.experimental.pallas import tpu as pltpu

F32 = jnp.float32
BF16 = jnp.bfloat16
HIGHEST = lax.Precision.HIGHEST

GRID_W = 64
WIN_ROWS = 8
WIN_COLS = 16
HEAD_DIM = 64
FOURIER_GROUPS = 4
N_MOD = 6
CAPACITY_FACTOR = 2
RMS_EPS = 1e-6
NEG_BIAS = -1e30
LANES = 128
ROWS_PER_STEP = WIN_ROWS // 2
BAND_ROWS = WIN_ROWS + ROWS_PER_STEP
SCATTER_BATCH = 8
VMEM_LIMIT = 56 * 1024 * 1024

_NT = (((1,), (1,)), ((), ()))
_TN = (((0,), (0,)), ((), ()))


def _params(*sem):
    return pltpu.CompilerParams(dimension_semantics=sem, vmem_limit_bytes=VMEM_LIMIT)


def _ln_mod(x, g, sc, sh):
    y = x * lax.rsqrt(jnp.mean(x * x, axis=-1, keepdims=True) + RMS_EPS)
    return (y * g) * (1.0 + sc) + sh


def _mod_kernel(c_ref, w_ref, b_ref, o_ref):
    c = c_ref[...]
    s = c * jax.nn.sigmoid(c)
    o_ref[...] = jnp.dot(s, w_ref[...], precision=HIGHEST, preferred_element_type=F32) + b_ref[...]


def _modulation(c, mod_w, mod_b):
    depth, d, n = mod_w.shape
    b = c.shape[0]
    tn = 1536
    return pl.pallas_call(
        _mod_kernel,
        grid=(depth, n // tn),
        in_specs=[
            pl.BlockSpec((b, d), lambda l, j: (0, 0)),
            pl.BlockSpec((None, d, tn), lambda l, j: (l, 0, j)),
            pl.BlockSpec((None, 1, tn), lambda l, j: (l, 0, j)),
        ],
        out_specs=pl.BlockSpec((None, b, tn), lambda l, j: (l, 0, j)),
        out_shape=jax.ShapeDtypeStruct((depth, b, n), F32),
        compiler_params=_params("parallel", "parallel"),
        name="adaln_mod",
    )(c, mod_w, mod_b.reshape(depth, 1, n))


def _row_spec(d, tiles_per_seq, k):
    return pl.BlockSpec((None, None, 1, d), lambda i: (i // tiles_per_seq, k, 0, 0))


def _inproj_kernel(x_ref, g_ref, sc_ref, sh_ref, w_ref, o_ref):
    h = _ln_mod(x_ref[...], g_ref[...], sc_ref[...], sh_ref[...]).astype(BF16)
    o_ref[...] = jnp.dot(h, w_ref[...], preferred_element_type=F32)


def _in_projection(x2, g, modl, w_bf, seq, tm=512):
    t, d = x2.shape
    n = w_bf.shape[1]
    tps = seq // tm
    return pl.pallas_call(
        _inproj_kernel,
        grid=(t // tm,),
        in_specs=[
            pl.BlockSpec((tm, d), lambda i: (i, 0)),
            pl.BlockSpec((1, d), lambda i: (0, 0)),
            _row_spec(d, tps, 1),
            _row_spec(d, tps, 0),
            pl.BlockSpec((d, n), lambda i: (0, 0)),
        ],
        out_specs=pl.BlockSpec((tm, n), lambda i: (i, 0)),
        out_shape=jax.ShapeDtypeStruct((t, n), F32),
        compiler_params=_params("parallel"),
        name="in_proj",
    )(x2, g, modl, modl, w_bf)


def _bias_table(rpb):
    heads, nr, nc = rpb.shape
    assert nr == 2 * WIN_ROWS - 1 and nc == 2 * WIN_COLS - 1
    left = GRID_W - WIN_COLS
    w = jnp.pad(rpb, ((0, 0), (0, 0), (left, 2 * GRID_W - left - nc)))
    flat = jnp.tile(w, (1, 1, GRID_W))[..., :GRID_W * (2 * GRID_W - 1)]
    toe = flat.reshape(heads, nr, GRID_W, 2 * GRID_W - 1)[..., GRID_W - 1:]
    qc = np.arange(GRID_W)[:, None]
    kc = np.arange(GRID_W)[None, :]
    cs = np.clip(qc - WIN_COLS // 2, 0, GRID_W - WIN_COLS)
    toe = jnp.where((kc >= cs) & (kc < cs + WIN_COLS), toe, NEG_BIAS)
    place = [[(u, 0) for u in range(ROWS_PER_STEP)],
             [(WIN_ROWS // 2 + u, u) for u in range(ROWS_PER_STEP)],
             [(BAND_ROWS - ROWS_PER_STEP + u, BAND_ROWS - WIN_ROWS) for u in range(ROWS_PER_STEP)]]
    cases = []
    for per_u in place:
        blocks = []
        for off, lo in per_u:
            first = lo - off + WIN_ROWS - 1
            blk = jnp.pad(toe[:, first:first + WIN_ROWS],
                          ((0, 0), (lo, BAND_ROWS - WIN_ROWS - lo), (0, 0), (0, 0)),
                          constant_values=NEG_BIAS)
            blocks.append(blk.transpose(0, 2, 1, 3))
        cases.append(jnp.stack(blocks, axis=1))
    tbl = jnp.stack(cases, axis=1)
    return tbl.reshape(heads, 3, ROWS_PER_STEP * GRID_W, BAND_ROWS * GRID_W)


def _mixers_kernel(q_ref, k_ref, v_ref, bg_ref, cg_ref, xv_ref, qkg_ref, tbl_ref, cw_ref,
                   attn_ref, conv_ref, q0_s, q1_s, k_s, v_s, pad_s, *, rows):
    seq = q_ref.shape[0]
    lane = lax.broadcasted_iota(jnp.int32, (1, LANES), 1)
    first = lane < HEAD_DIM

    def head_norm(x, g):
        xx = x * x
        s0 = jnp.sum(jnp.where(first, xx, 0.0), axis=-1, keepdims=True)
        s1 = jnp.sum(jnp.where(first, 0.0, xx), axis=-1, keepdims=True)
        ms = jnp.where(first, s0, s1) * (1.0 / HEAD_DIM)
        return x * lax.rsqrt(ms + RMS_EPS) * g

    qn = head_norm(q_ref[...], qkg_ref[0:1, :]) * (1.0 / math.sqrt(HEAD_DIM))
    q0_s[...] = jnp.where(first, qn, 0.0).astype(BF16)
    q1_s[...] = jnp.where(first, 0.0, qn).astype(BF16)
    k_s[...] = head_norm(k_ref[...], qkg_ref[1:2, :]).astype(BF16)
    v_s[...] = v_ref[...].astype(BF16)

    n_groups = rows // ROWS_PER_STEP
    nq = ROWS_PER_STEP * GRID_W
    for g in range(n_groups):
        b0 = min(max(g * ROWS_PER_STEP - WIN_ROWS // 2, 0), rows - BAND_ROWS)
        case = 0 if g == 0 else (2 if g == n_groups - 1 else 1)
        kb = k_s[b0 * GRID_W:(b0 + BAND_ROWS) * GRID_W, :]
        vb = v_s[b0 * GRID_W:(b0 + BAND_ROWS) * GRID_W, :]
        outs = []
        for h, qs in enumerate((q0_s, q1_s)):
            s = lax.dot_general(qs[g * nq:(g + 1) * nq, :], kb, _NT, preferred_element_type=F32)
            s = s + tbl_ref[h, case]
            p = jnp.exp(s - jnp.max(s, axis=-1, keepdims=True))
            l = jnp.sum(p, axis=-1, keepdims=True)
            outs.append(jnp.dot(p.astype(BF16), vb, preferred_element_type=F32) / l)
        attn_ref[g * nq:(g + 1) * nq, :] = jnp.where(first, outs[0], outs[1])

    zeros = jnp.zeros((8, LANES), F32)
    pad_s[0:8, :] = zeros
    pad_s[seq + 8:seq + 16, :] = zeros
    pad_s[8:seq + 8, :] = cg_ref[...] * xv_ref[...]
    z = (pad_s[7:seq + 7, :] * cw_ref[0:1, :] + pad_s[8:seq + 8, :] * cw_ref[1:2, :]
         + pad_s[9:seq + 9, :] * cw_ref[2:3, :])
    conv_ref[...] = bg_ref[...] * z


def _mixers(proj3, qk_g, rpb, conv_w):
    b, seq, n = proj3.shape
    width = n // 6
    nblk = width // LANES
    heads = width // HEAD_DIM
    rows = seq // GRID_W
    assert rows >= BAND_ROWS + ROWS_PER_STEP and rows % ROWS_PER_STEP == 0 and rpb.shape[0] == heads
    tbl = _bias_table(rpb)
    qkg = jnp.tile(qk_g, (1, LANES // HEAD_DIM))

    def col(k):
        return pl.BlockSpec((None, seq, LANES), lambda i, j, k=k: (i, 0, k * nblk + j))

    out_spec = pl.BlockSpec((None, seq, LANES), lambda i, j: (i, 0, j))
    hp = LANES // HEAD_DIM
    return pl.pallas_call(
        functools.partial(_mixers_kernel, rows=rows),
        grid=(b, nblk),
        in_specs=[col(0), col(1), col(2), col(3), col(4), col(5),
                  pl.BlockSpec((2, LANES), lambda i, j: (0, 0)),
                  pl.BlockSpec((hp,) + tbl.shape[1:], lambda i, j: (j, 0, 0, 0)),
                  pl.BlockSpec((conv_w.shape[0], LANES), lambda i, j: (0, j))],
        out_specs=[out_spec, out_spec],
        out_shape=[jax.ShapeDtypeStruct((b, seq, width), F32)] * 2,
        scratch_shapes=[pltpu.VMEM((seq, LANES), BF16)] * 4 + [pltpu.VMEM((seq + 16, LANES), F32)],
        compiler_params=_params("parallel", "parallel"),
        name="attn_conv",
    )(proj3, proj3, proj3, proj3, proj3, proj3, qkg, tbl, conv_w)


def _outproj_kernel(a_ref, c_ref, wa_ref, wc_ref, x_ref, gate_ref, o_ref):
    m = jnp.dot(a_ref[...].astype(BF16), wa_ref[...], preferred_element_type=F32)
    m = m + jnp.dot(c_ref[...].astype(BF16), wc_ref[...], preferred_element_type=F32)
    o_ref[...] = x_ref[...] + gate_ref[...] * m


def _out_projection(attn2, conv2, w_bf, x2, modl, seq, tm=512):
    t, d = x2.shape
    wa = attn2.shape[1]
    wc = conv2.shape[1]
    tps = seq // tm
    return pl.pallas_call(
        _outproj_kernel,
        grid=(t // tm,),
        in_specs=[
            pl.BlockSpec((tm, wa), lambda i: (i, 0)),
            pl.BlockSpec((tm, wc), lambda i: (i, 0)),
            pl.BlockSpec((wa, d), lambda i: (0, 0)),
            pl.BlockSpec((wc, d), lambda i: (1, 0)),
            pl.BlockSpec((tm, d), lambda i: (i, 0)),
            _row_spec(d, tps, 2),
        ],
        out_specs=pl.BlockSpec((tm, d), lambda i: (i, 0)),
        out_shape=jax.ShapeDtypeStruct((t, d), F32),
        compiler_params=_params("parallel"),
        name="out_proj",
    )(attn2, conv2, w_bf, w_bf, x2, modl)


def _dft_tables(n):
    jk = (np.arange(n)[:, None] * np.arange(n)[None, :]) % n
    ang = 2.0 * np.pi * jk.astype(np.float64) / n
    return np.cos(ang).astype(np.float32), np.sin(ang).astype(np.float32)


def _fnet_chan_kernel(x_ref, g_ref, sc_ref, sh_ref, cs_ref, a_ref, b_ref, *, gw):
    h = _ln_mod(x_ref[...], g_ref[...], sc_ref[...], sh_ref[...]).astype(BF16)
    for g in range(h.shape[1] // gw):
        ab = jnp.dot(h[:, g * gw:(g + 1) * gw], cs_ref[...], preferred_element_type=F32)
        a_ref[:, g * gw:(g + 1) * gw] = ab[:, :gw].astype(BF16)
        b_ref[:, g * gw:(g + 1) * gw] = ab[:, gw:].astype(BF16)


def _fnet_seq_kernel(cs_ref, ss_ref, a_ref, b_ref, w_ref, x_ref, gate_ref, o_ref, *, norm):
    f = jnp.dot(cs_ref[...], a_ref[...], preferred_element_type=F32)
    f = f - jnp.dot(ss_ref[...], b_ref[...], preferred_element_type=F32)
    m = jnp.dot((f * norm).astype(BF16), w_ref[...], preferred_element_type=F32)
    o_ref[...] = x_ref[...] + gate_ref[...] * m


def _fnet_layer(x3, g, modl, w_bf, tm=512):
    b, seq, d = x3.shape
    gw = d // FOURIER_GROUPS
    cc, sc = _dft_tables(gw)
    cs_chan = jnp.concatenate([jnp.asarray(cc), jnp.asarray(sc)], axis=1).astype(BF16)
    cseq, sseq = _dft_tables(seq)
    cseq = jnp.asarray(cseq).astype(BF16)
    sseq = jnp.asarray(sseq).astype(BF16)
    x2 = x3.reshape(b * seq, d)
    tps = seq // tm
    a, bm = pl.pallas_call(
        functools.partial(_fnet_chan_kernel, gw=gw),
        grid=(b * seq // tm,),
        in_specs=[
            pl.BlockSpec((tm, d), lambda i: (i, 0)),
            pl.BlockSpec((1, d), lambda i: (0, 0)),
            _row_spec(d, tps, 1),
            _row_spec(d, tps, 0),
            pl.BlockSpec((gw, 2 * gw), lambda i: (0, 0)),
        ],
        out_specs=[pl.BlockSpec((tm, d), lambda i: (i, 0))] * 2,
        out_shape=[jax.ShapeDtypeStruct((b * seq, d), BF16)] * 2,
        compiler_params=_params("parallel"),
        name="fnet_chan",
    )(x2, g, modl, modl, cs_chan)
    a3 = a.reshape(b, seq, d)
    b3 = bm.reshape(b, seq, d)
    norm = 1.0 / math.sqrt(seq * gw)
    return pl.pallas_call(
        functools.partial(_fnet_seq_kernel, norm=norm),
        grid=(b, seq // tm),
        in_specs=[
            pl.BlockSpec((tm, seq), lambda i, j: (j, 0)),
            pl.BlockSpec((tm, seq), lambda i, j: (j, 0)),
            pl.BlockSpec((None, seq, d), lambda i, j: (i, 0, 0)),
            pl.BlockSpec((None, seq, d), lambda i, j: (i, 0, 0)),
            pl.BlockSpec((d, d), lambda i, j: (0, 0)),
            pl.BlockSpec((None, tm, d), lambda i, j: (i, j, 0)),
            pl.BlockSpec((None, None, 1, d), lambda i, j: (i, 2, 0, 0)),
        ],
        out_specs=pl.BlockSpec((None, tm, d), lambda i, j: (i, j, 0)),
        out_shape=jax.ShapeDtypeStruct((b, seq, d), F32),
        compiler_params=_params("parallel", "parallel"),
        name="fnet_seq",
    )(cseq, sseq, a3, b3, w_bf, x3, modl)


def _router_kernel(x_ref, g_ref, sc_ref, sh_ref, rwt_ref, h_ref, aff_ref):
    h = _ln_mod(x_ref[...], g_ref[...], sc_ref[...], sh_ref[...])
    tm, d = h.shape
    nch = d // LANES
    for j in range(nch):
        h_ref[pl.ds(j, tm, stride=nch), :] = h[:, j * LANES:(j + 1) * LANES]
    logits = lax.dot_general(rwt_ref[...], h, _NT, precision=HIGHEST, preferred_element_type=F32)
    e = jnp.exp(logits - jnp.max(logits, axis=0, keepdims=True))
    aff_ref[...] = e / jnp.sum(e, axis=0, keepdims=True)


def _router(x3, g, modl, router_wt, tm=512):
    b, seq, d = x3.shape
    e = router_wt.shape[0]
    tps = seq // tm
    nch = d // LANES
    return pl.pallas_call(
        _router_kernel,
        grid=(b, tps),
        in_specs=[
            pl.BlockSpec((None, tm, d), lambda i, j: (i, j, 0)),
            pl.BlockSpec((1, d), lambda i, j: (0, 0)),
            pl.BlockSpec((None, None, 1, d), lambda i, j: (i, 4, 0, 0)),
            pl.BlockSpec((None, None, 1, d), lambda i, j: (i, 3, 0, 0)),
            pl.BlockSpec((e, d), lambda i, j: (0, 0)),
        ],
        out_specs=[pl.BlockSpec((None, tm * nch, LANES), lambda i, j: (i, j, 0)),
                   pl.BlockSpec((None, e, tm), lambda i, j: (i, 0, j))],
        out_shape=[jax.ShapeDtypeStruct((b, seq * nch, LANES), F32),
                   jax.ShapeDtypeStruct((b, e, seq), F32)],
        compiler_params=_params("parallel", "parallel"),
        name="router",
    )(x3, g, modl, modl, router_wt)


def _topk_kernel(aff_ref, tok_ref, gs_ref, tri_s, *, cap):
    n_exp, seq = aff_ref.shape

    @pl.when(pl.program_id(0) == 0)
    def _():
        r = lax.broadcasted_iota(jnp.int32, (seq, seq), 0)
        c = lax.broadcasted_iota(jnp.int32, (seq, seq), 1)
        tri_s[...] = jnp.where(r < c, 1.0, 0.0).astype(BF16)

    a = aff_ref[...]

    def search(i, t):
        cand = t | lax.shift_left(jnp.int32(1), 30 - i)
        cnt = jnp.sum((a >= pltpu.bitcast(cand, F32)).astype(jnp.int32), axis=1, keepdims=True)
        return jnp.where(cnt >= cap, cand, t)

    thr = lax.fori_loop(0, 31, search, jnp.zeros((a.shape[0], 1), jnp.int32))
    thr = pltpu.bitcast(thr, F32)
    gt = a > thr
    eq = a == thr
    need = cap - jnp.sum(gt.astype(jnp.int32), axis=1, keepdims=True)
    tri = tri_s[...]
    eq_before = jnp.dot(jnp.where(eq, 1.0, 0.0).astype(BF16), tri, preferred_element_type=F32)
    sel = gt | (eq & (eq_before < need.astype(F32)))
    before = jnp.dot(jnp.where(sel, 1.0, 0.0).astype(BF16), tri, preferred_element_type=F32)
    rank = jnp.where(sel, before.astype(jnp.int32), -1)

    slot = lax.broadcasted_iota(jnp.int32, (cap, seq), 0)
    tok_ids = lax.broadcasted_iota(jnp.int32, (1, seq), 1)
    t_hi = lax.shift_right_logical(tok_ids, 6).astype(F32)
    t_lo = (tok_ids & 63).astype(F32)
    pad = jnp.zeros((11, seq), F32)
    for e in range(n_exp):
        pick = jnp.where(slot == rank[e:e + 1, :], 1.0, 0.0).astype(BF16)
        g = a[e:e + 1, :]
        g0 = g.astype(BF16).astype(F32)
        g1 = (g - g0).astype(BF16).astype(F32)
        g2 = (g - g0) - g1
        lhs = jnp.concatenate([t_hi, t_lo, g0, g1, g2, pad], axis=0).astype(BF16)
        res = lax.dot_general(lhs, pick, _NT, preferred_element_type=F32)
        tok_ref[e:e + 1, :] = (res[0:1, :] * 64.0 + res[1:2, :]).astype(jnp.int32)
        gs_ref[e:e + 1, :] = (res[2:3, :] + res[3:4, :]) + res[4:5, :]


def _topk(aff, cap):
    b, e, seq = aff.shape
    out_spec = pl.BlockSpec((None, e, cap), lambda i: (i, 0, 0))
    return pl.pallas_call(
        functools.partial(_topk_kernel, cap=cap),
        grid=(b,),
        in_specs=[pl.BlockSpec((None, e, seq), lambda i: (i, 0, 0))],
        out_specs=[out_spec, out_spec],
        out_shape=[jax.ShapeDtypeStruct((b, e, cap), jnp.int32), jax.ShapeDtypeStruct((b, e, cap), F32)],
        scratch_shapes=[pltpu.VMEM((seq, seq), BF16)],
        compiler_params=_params("arbitrary"),
        name="topk",
    )(aff)


def _gather_kernel(tok_ref, h_ref, o_ref, x_s, *, stride):
    n_exp, cap = tok_ref.shape
    nch = x_s.shape[0] // stride

    def expert(e, carry):
        for i in range(cap):
            t = pl.multiple_of(tok_ref[e, i] * nch, nch)
            x_s[pl.ds(i, nch, stride=stride), :] = h_ref[pl.ds(t, nch), :]
        xe = jnp.concatenate([x_s[j * stride:j * stride + cap, :] for j in range(nch)], axis=-1)
        o_ref[e] = xe.astype(BF16)
        return carry

    lax.fori_loop(0, n_exp, expert, 0)


def _gather_tokens(h_slab, tok, d, stride):
    b, rows, _ = h_slab.shape
    _, e, cap = tok.shape
    nch = d // LANES
    return pl.pallas_call(
        functools.partial(_gather_kernel, stride=stride),
        grid=(b,),
        in_specs=[
            pl.BlockSpec((None, e, cap), lambda i: (i, 0, 0), memory_space=pltpu.SMEM),
            pl.BlockSpec((None, rows, LANES), lambda i: (i, 0, 0)),
        ],
        out_specs=pl.BlockSpec((e, None, cap, d), lambda i: (0, i, 0, 0)),
        out_shape=jax.ShapeDtypeStruct((e, b, cap, d), BF16),
        scratch_shapes=[pltpu.VMEM((nch * stride, LANES), F32)],
        compiler_params=_params("parallel"),
        name="moe_gather",
    )(tok, h_slab)


def _ffn_kernel(x_ref, wg_ref, wu_ref, wd_ref, y_ref, wg_s, wu_s, wd_s, *, cap, ff_chunk):
    @pl.when(pl.program_id(1) == 0)
    def _():
        wg_s[...] = wg_ref[...].astype(BF16)
        wu_s[...] = wu_ref[...].astype(BF16)
        wd_s[...] = wd_ref[...].astype(BF16)

    x = x_ref[...]
    y = None
    for c in range(wg_s.shape[1] // ff_chunk):
        cols = slice(c * ff_chunk, (c + 1) * ff_chunk)
        a = jnp.dot(x, wg_s[:, cols], preferred_element_type=F32)
        u = jnp.dot(x, wu_s[:, cols], preferred_element_type=F32)
        hm = (a * jax.nn.sigmoid(a) * u).astype(BF16)
        yc = jnp.dot(hm, wd_s[cols, :], preferred_element_type=F32)
        y = yc if y is None else y + yc
    bpb, nch, stride, _ = y_ref.shape
    for bb in range(bpb):
        for j in range(nch):
            y_ref[bb, j, 0:cap, :] = y[bb * cap:(bb + 1) * cap, j * LANES:(j + 1) * LANES]
        y_ref[bb, :, cap:, :] = jnp.zeros((nch, stride - cap, LANES), F32)


def _expert_ffn(xe, layer, wg, wu, wd, cap, stride, tm):
    e, m, d = xe.shape
    ff = wg.shape[3]
    nch = d // LANES
    bpb = tm // cap
    return pl.pallas_call(
        functools.partial(_ffn_kernel, cap=cap, ff_chunk=min(512, ff)),
        grid=(e, m // tm),
        in_specs=[
            pl.BlockSpec((None, tm, d), lambda i, j: (i, j, 0)),
            pl.BlockSpec((None, None, d, ff), lambda i, j: (layer, i, 0, 0)),
            pl.BlockSpec((None, None, d, ff), lambda i, j: (layer, i, 0, 0)),
            pl.BlockSpec((None, None, ff, d), lambda i, j: (layer, i, 0, 0)),
        ],
        out_specs=pl.BlockSpec((None, bpb, nch, stride, LANES), lambda i, j: (i, j, 0, 0, 0)),
        out_shape=jax.ShapeDtypeStruct((e, m // cap, nch, stride, LANES), F32),
        scratch_shapes=[pltpu.VMEM((d, ff), BF16), pltpu.VMEM((d, ff), BF16), pltpu.VMEM((ff, d), BF16)],
        compiler_params=_params("parallel", "arbitrary"),
        name="moe_ffn",
    )(xe, wg, wu, wd)


def _scatter_kernel(tok_ref, gs_ref, y_ref, x_ref, gate_ref, o_ref, acc_s, *, stride):
    cap = tok_ref.shape[1]
    seq, d = x_ref.shape
    nch = d // LANES
    e = pl.program_id(1)

    @pl.when(e == 0)
    def _():
        acc_s[...] = jnp.zeros_like(acc_s)

    def slab(i):
        return pl.ds(pl.multiple_of(tok_ref[0, i] * nch, nch), nch)

    for i0 in range(0, cap, SCATTER_BATCH):
        vals = [acc_s[slab(i), :] + y_ref[pl.ds(i, nch, stride=stride), :] * gs_ref[0, i]
                for i in range(i0, i0 + SCATTER_BATCH)]
        for i, v in zip(range(i0, i0 + SCATTER_BATCH), vals):
            acc_s[slab(i), :] = v

    @pl.when(e == pl.num_programs(1) - 1)
    def _():
        for j in range(nch):
            lanes = slice(j * LANES, (j + 1) * LANES)
            o_ref[:, lanes] = x_ref[:, lanes] + gate_ref[:, lanes] * acc_s[pl.ds(j, seq, stride=nch), :]


def _scatter_residual(y_cm, tok, gs, x3, modl, stride):
    e, b, _, _ = y_cm.shape
    _, seq, d = x3.shape
    cap = tok.shape[-1]
    nch = d // LANES
    assert cap % SCATTER_BATCH == 0
    smem_row = pl.BlockSpec((None, None, 1, cap), lambda i, j: (i, j, 0, 0), memory_space=pltpu.SMEM)
    return pl.pallas_call(
        functools.partial(_scatter_kernel, stride=stride),
        grid=(b, e),
        in_specs=[
            smem_row, smem_row,
            pl.BlockSpec((None, None, nch * stride, LANES), lambda i, j: (j, i, 0, 0)),
            pl.BlockSpec((None, seq, d), lambda i, j: (i, 0, 0)),
            pl.BlockSpec((None, None, 1, d), lambda i, j: (i, 5, 0, 0)),
        ],
        out_specs=pl.BlockSpec((None, seq, d), lambda i, j: (i, 0, 0)),
        out_shape=jax.ShapeDtypeStruct((b, seq, d), F32),
        scratch_shapes=[pltpu.VMEM((seq * nch, LANES), F32)],
        compiler_params=_params("parallel", "arbitrary"),
        name="moe_scatter",
    )(tok.reshape(b, e, 1, cap), gs.reshape(b, e, 1, cap), y_cm, x3, modl)


def _moe_layer(x3, g, modl, router_w, layer, wg, wu, wd):
    b, seq, d = x3.shape
    e = router_w.shape[1]
    cap = max(1, CAPACITY_FACTOR * seq // e)
    stride = cap + 8
    h_slab, aff = _router(x3, g, modl, router_w.T)
    tok, gs = _topk(aff, cap)
    xe = _gather_tokens(h_slab, tok, d, stride)
    y = _expert_ffn(xe.reshape(e, b * cap, d), layer, wg, wu, wd, cap, stride, tm=min(1024, b * cap))
    return _scatter_residual(y.reshape(e, b, -1, LANES), tok, gs, x3, modl, stride)


def kernel(x, c, mod_w, mod_b, norm_g, mix_w_in, qk_g, rpb, conv_w, mix_w_out, fnet_w_out,
           router_w, exp_w_gate, exp_w_up, exp_w_down):
    b, seq, d = x.shape
    depth = mod_w.shape[0]
    mod = _modulation(c, mod_w, mod_b).reshape(depth, b, N_MOD, 1, d)
    for l in range(depth):
        modl = mod[l]
        g1 = norm_g[l, 0].reshape(1, d)
        g2 = norm_g[l, 1].reshape(1, d)
        if l % 2 == 0:
            j = l // 2
            x2 = x.reshape(b * seq, d)
            proj = _in_projection(x2, g1, modl, mix_w_in[j].astype(BF16), seq)
            attn, conv = _mixers(proj.reshape(b, seq, -1), qk_g[j], rpb[j], conv_w[j])
            width = attn.shape[-1]
            x = _out_projection(attn.reshape(b * seq, width), conv.reshape(b * seq, width),
                                mix_w_out[j].astype(BF16), x2, modl, seq).reshape(b, seq, d)
        else:
            x = _fnet_layer(x, g1, modl, fnet_w_out[l // 2].astype(BF16))
        x = _moe_layer(x, g2, modl, router_w[l], l, exp_w_gate, exp_w_up, exp_w_down)
    return x
```

```python
import functools
import math

import numpy as np
import jax
import jax.numpy as jnp
from jax import lax
from jax.experimental import pallas as pl
from jax.experimental.pallas import tpu as pltpu

F32 = jnp.float32
BF16 = jnp.bfloat16
HIGHEST = lax.Precision.HIGHEST

GRID_W = 64
WIN_ROWS = 8
WIN_COLS = 16
HEAD_DIM = 64
FOURIER_GROUPS = 4
N_MOD = 6
CAPACITY_FACTOR = 2
RMS_EPS = 1e-6
NEG_BIAS = -1e30
LANES = 128
ROWS_PER_STEP = WIN_ROWS // 2
BAND_ROWS = WIN_ROWS + ROWS_PER_STEP
SCATTER_BATCH = 8
VMEM_LIMIT = 56 * 1024 * 1024

_NT = (((1,), (1,)), ((), ()))
_TN = (((0,), (0,)), ((), ()))


def _params(*sem):
    return pltpu.CompilerParams(dimension_semantics=sem, vmem_limit_bytes=VMEM_LIMIT)


def _ln_mod(x, g, sc, sh):
    y = x * lax.rsqrt(jnp.mean(x * x, axis=-1, keepdims=True) + RMS_EPS)
    return (y * g) * (1.0 + sc) + sh


def _mod_kernel(c_ref, w_ref, b_ref, o_ref):
    c = c_ref[...]
    s = c * jax.nn.sigmoid(c)
    o_ref[...] = jnp.dot(s, w_ref[...], precision=HIGHEST, preferred_element_type=F32) + b_ref[...]


def _modulation(c, mod_w, mod_b):
    depth, d, n = mod_w.shape
    b = c.shape[0]
    tn = 1536
    return pl.pallas_call(
        _mod_kernel,
        grid=(depth, n // tn),
        in_specs=[
            pl.BlockSpec((b, d), lambda l, j: (0, 0)),
            pl.BlockSpec((None, d, tn), lambda l, j: (l, 0, j)),
            pl.BlockSpec((None, 1, tn), lambda l, j: (l, 0, j)),
        ],
        out_specs=pl.BlockSpec((None, b, tn), lambda l, j: (l, 0, j)),
        out_shape=jax.ShapeDtypeStruct((depth, b, n), F32),
        compiler_params=_params("parallel", "parallel"),
        name="adaln_mod",
    )(c, mod_w, mod_b.reshape(depth, 1, n))


def _row_spec(d, tiles_per_seq, k):
    return pl.BlockSpec((None, None, 1, d), lambda i: (i // tiles_per_seq, k, 0, 0))


def _inproj_kernel(x_ref, g_ref, sc_ref, sh_ref, w_ref, o_ref):
    h = _ln_mod(x_ref[...], g_ref[...], sc_ref[...], sh_ref[...]).astype(BF16)
    o_ref[...] = jnp.dot(h, w_ref[...], preferred_element_type=F32)


def _in_projection(x2, g, modl, w_bf, seq, tm=512):
    t, d = x2.shape
    n = w_bf.shape[1]
    tps = seq // tm
    return pl.pallas_call(
        _inproj_kernel,
        grid=(t // tm,),
        in_specs=[
            pl.BlockSpec((tm, d), lambda i: (i, 0)),
            pl.BlockSpec((1, d), lambda i: (0, 0)),
            _row_spec(d, tps, 1),
            _row_spec(d, tps, 0),
            pl.BlockSpec((d, n), lambda i: (0, 0)),
        ],
        out_specs=pl.BlockSpec((tm, n), lambda i: (i, 0)),
        out_shape=jax.ShapeDtypeStruct((t, n), F32),
        compiler_params=_params("parallel"),
        name="in_proj",
    )(x2, g, modl, modl, w_bf)


def _bias_table(rpb):
    heads, nr, nc = rpb.shape
    assert nr == 2 * WIN_ROWS - 1 and nc == 2 * WIN_COLS - 1
    left = GRID_W - WIN_COLS
    w = jnp.pad(rpb, ((0, 0), (0, 0), (left, 2 * GRID_W - left - nc)))
    flat = jnp.tile(w, (1, 1, GRID_W))[..., :GRID_W * (2 * GRID_W - 1)]
    toe = flat.reshape(heads, nr, GRID_W, 2 * GRID_W - 1)[..., GRID_W - 1:]
    qc = np.arange(GRID_W)[:, None]
    kc = np.arange(GRID_W)[None, :]
    cs = np.clip(qc - WIN_COLS // 2, 0, GRID_W - WIN_COLS)
    toe = jnp.where((kc >= cs) & (kc < cs + WIN_COLS), toe, NEG_BIAS)
    place = [[(u, 0) for u in range(ROWS_PER_STEP)],
             [(WIN_ROWS // 2 + u, u) for u in range(ROWS_PER_STEP)],
             [(BAND_ROWS - ROWS_PER_STEP + u, BAND_ROWS - WIN_ROWS) for u in range(ROWS_PER_STEP)]]
    masked = jnp.full((heads, GRID_W, GRID_W), NEG_BIAS, F32)
    cases = []
    for per_u in place:
        blocks = []
        for off, lo in per_u:
            first = lo - off + WIN_ROWS - 1
            row = [toe[:, first + i - lo] if lo <= i < lo + WIN_ROWS else masked
                   for i in range(BAND_ROWS)]
            blocks.append(jnp.concatenate(row, axis=-1))
        cases.append(jnp.concatenate(blocks, axis=1))
    return jnp.stack(cases, axis=1)


def _mixers_kernel(q_ref, k_ref, v_ref, bg_ref, cg_ref, xv_ref, qkg_ref, tbl_ref, cw_ref,
                   attn_ref, conv_ref, q0_s, q1_s, k_s, v_s, pad_s, *, rows):
    seq = q_ref.shape[0]
    lane = lax.broadcasted_iota(jnp.int32, (1, LANES), 1)
    first = lane < HEAD_DIM

    def head_norm(x, g):
        xx = x * x
        s0 = jnp.sum(jnp.where(first, xx, 0.0), axis=-1, keepdims=True)
        s1 = jnp.sum(jnp.where(first, 0.0, xx), axis=-1, keepdims=True)
        ms = jnp.where(first, s0, s1) * (1.0 / HEAD_DIM)
        return x * lax.rsqrt(ms + RMS_EPS) * g

    qn = head_norm(q_ref[...], qkg_ref[0:1, :]) * (1.0 / math.sqrt(HEAD_DIM))
    q0_s[...] = jnp.where(first, qn, 0.0).astype(BF16)
    q1_s[...] = jnp.where(first, 0.0, qn).astype(BF16)
    k_s[...] = head_norm(k_ref[...], qkg_ref[1:2, :]).astype(BF16)
    v_s[...] = v_ref[...].astype(BF16)

    n_groups = rows // ROWS_PER_STEP
    nq = ROWS_PER_STEP * GRID_W
    for g in range(n_groups):
        b0 = min(max(g * ROWS_PER_STEP - WIN_ROWS // 2, 0), rows - BAND_ROWS)
        case = 0 if g == 0 else (2 if g == n_groups - 1 else 1)
        kb = k_s[b0 * GRID_W:(b0 + BAND_ROWS) * GRID_W, :]
        vb = v_s[b0 * GRID_W:(b0 + BAND_ROWS) * GRID_W, :]
        outs = []
        for h, qs in enumerate((q0_s, q1_s)):
            s = lax.dot_general(qs[g * nq:(g + 1) * nq, :], kb, _NT, preferred_element_type=F32)
            s = s + tbl_ref[h, case]
            p = jnp.exp(s - jnp.max(s, axis=-1, keepdims=True))
            l = jnp.sum(p, axis=-1, keepdims=True)
            outs.append(jnp.dot(p.astype(BF16), vb, preferred_element_type=F32) / l)
        attn_ref[g * nq:(g + 1) * nq, :] = jnp.where(first, outs[0], outs[1])

    zeros = jnp.zeros((8, LANES), F32)
    pad_s[0:8, :] = zeros
    pad_s[seq + 8:seq + 16, :] = zeros
    pad_s[8:seq + 8, :] = cg_ref[...] * xv_ref[...]
    z = (pad_s[7:seq + 7, :] * cw_ref[0:1, :] + pad_s[8:seq + 8, :] * cw_ref[1:2, :]
         + pad_s[9:seq + 9, :] * cw_ref[2:3, :])
    conv_ref[...] = bg_ref[...] * z


def _mixers(proj3, qk_g, rpb, conv_w):
    b, seq, n = proj3.shape
    width = n // 6
    nblk = width // LANES
    heads = width // HEAD_DIM
    rows = seq // GRID_W
    assert rows >= BAND_ROWS + ROWS_PER_STEP and rows % ROWS_PER_STEP == 0 and rpb.shape[0] == heads
    tbl = _bias_table(rpb)
    qkg = jnp.tile(qk_g, (1, LANES // HEAD_DIM))

    def col(k):
        return pl.BlockSpec((None, seq, LANES), lambda i, j, k=k: (i, 0, k * nblk + j))

    out_spec = pl.BlockSpec((None, seq, LANES), lambda i, j: (i, 0, j))
    hp = LANES // HEAD_DIM
    return pl.pallas_call(
        functools.partial(_mixers_kernel, rows=rows),
        grid=(b, nblk),
        in_specs=[col(0), col(1), col(2), col(3), col(4), col(5),
                  pl.BlockSpec((2, LANES), lambda i, j: (0, 0)),
                  pl.BlockSpec((hp,) + tbl.shape[1:], lambda i, j: (j, 0, 0, 0)),
                  pl.BlockSpec((conv_w.shape[0], LANES), lambda i, j: (0, j))],
        out_specs=[out_spec, out_spec],
        out_shape=[jax.ShapeDtypeStruct((b, seq, width), F32)] * 2,
        scratch_shapes=[pltpu.VMEM((seq, LANES), BF16)] * 4 + [pltpu.VMEM((seq + 16, LANES), F32)],
        compiler_params=_params("parallel", "parallel"),
        name="attn_conv",
    )(proj3, proj3, proj3, proj3, proj3, proj3, qkg, tbl, conv_w)


def _route(x, g_ref, sc_ref, sh_ref, rw_ref, h_ref, aff_ref):
    h = _ln_mod(x, g_ref[...], sc_ref[...], sh_ref[...])
    tm, d = h.shape
    nch = d // LANES
    n_exp = aff_ref.shape[0]
    for j in range(nch):
        h_ref[pl.ds(j, tm, stride=nch), :] = h[:, j * LANES:(j + 1) * LANES]
    h0 = h.astype(BF16)
    h1 = (h - h0.astype(F32)).astype(BF16)
    lg = jnp.dot(h0, rw_ref[...], preferred_element_type=F32)
    lg = (lg[:, :LANES] + lg[:, LANES:]) + jnp.dot(h1, rw_ref[:, :LANES], preferred_element_type=F32)
    lane = lax.broadcasted_iota(jnp.int32, (1, LANES), 1)
    lg = jnp.where(lane < n_exp, lg, NEG_BIAS)
    p = jnp.exp(lg - jnp.max(lg, axis=-1, keepdims=True))
    aff = p / jnp.sum(p, axis=-1, keepdims=True)
    aff_ref[...] = aff.T[:n_exp, :]


def _router_weights(router_w):
    w = jnp.pad(router_w, ((0, 0), (0, LANES - router_w.shape[1])))
    w0 = w.astype(BF16)
    w1 = (w - w0.astype(F32)).astype(BF16)
    return jnp.concatenate([w0, w1], axis=1)


def _route_specs(b, seq, d, e, tm):
    nch = d // LANES
    in_specs = [pl.BlockSpec((1, d), lambda i, j: (0, 0)),
                pl.BlockSpec((None, None, 1, d), lambda i, j: (i, 4, 0, 0)),
                pl.BlockSpec((None, None, 1, d), lambda i, j: (i, 3, 0, 0)),
                pl.BlockSpec((d, 2 * LANES), lambda i, j: (0, 0))]
    out_specs = [pl.BlockSpec((None, tm * nch, LANES), lambda i, j: (i, j, 0)),
                 pl.BlockSpec((None, e, tm), lambda i, j: (i, 0, j))]
    out_shape = [jax.ShapeDtypeStruct((b, seq * nch, LANES), F32),
                 jax.ShapeDtypeStruct((b, e, seq), F32)]
    return in_specs, out_specs, out_shape


def _outproj_kernel(a_ref, c_ref, wa_ref, wc_ref, x_ref, gate_ref, g2_ref, sc2_ref, sh2_ref, rw_ref,
                    o_ref, h_ref, aff_ref):
    m = jnp.dot(a_ref[...].astype(BF16), wa_ref[...], preferred_element_type=F32)
    m = m + jnp.dot(c_ref[...].astype(BF16), wc_ref[...], preferred_element_type=F32)
    x = x_ref[...] + gate_ref[...] * m
    o_ref[...] = x
    _route(x, g2_ref, sc2_ref, sh2_ref, rw_ref, h_ref, aff_ref)


def _out_projection(attn3, conv3, w_bf, x3, modl, g2, rw01, n_exp, tm=512):
    b, seq, d = x3.shape
    wa = attn3.shape[2]
    wc = conv3.shape[2]
    r_in, r_out, r_shape = _route_specs(b, seq, d, n_exp, tm)
    tile = pl.BlockSpec((None, tm, d), lambda i, j: (i, j, 0))
    return pl.pallas_call(
        _outproj_kernel,
        grid=(b, seq // tm),
        in_specs=[
            pl.BlockSpec((None, tm, wa), lambda i, j: (i, j, 0)),
            pl.BlockSpec((None, tm, wc), lambda i, j: (i, j, 0)),
            pl.BlockSpec((wa, d), lambda i, j: (0, 0)),
            pl.BlockSpec((wc, d), lambda i, j: (1, 0)),
            tile,
            pl.BlockSpec((None, None, 1, d), lambda i, j: (i, 2, 0, 0)),
        ] + r_in,
        out_specs=[tile] + r_out,
        out_shape=[jax.ShapeDtypeStruct((b, seq, d), F32)] + r_shape,
        compiler_params=_params("parallel", "parallel"),
        name="out_proj",
    )(attn3, conv3, w_bf, w_bf, x3, modl, g2, modl, modl, rw01)


def _dft_tables(n):
    jk = (np.arange(n)[:, None] * np.arange(n)[None, :]) % n
    ang = 2.0 * np.pi * jk.astype(np.float64) / n
    return np.cos(ang).astype(np.float32), np.sin(ang).astype(np.float32)


def _fnet_chan_kernel(x_ref, g_ref, sc_ref, sh_ref, cs_ref, a_ref, b_ref, *, gw):
    h = _ln_mod(x_ref[...], g_ref[...], sc_ref[...], sh_ref[...]).astype(BF16)
    for g in range(h.shape[1] // gw):
        ab = jnp.dot(h[:, g * gw:(g + 1) * gw], cs_ref[...], preferred_element_type=F32)
        a_ref[:, g * gw:(g + 1) * gw] = ab[:, :gw].astype(BF16)
        b_ref[:, g * gw:(g + 1) * gw] = ab[:, gw:].astype(BF16)


def _fnet_seq_kernel(cs_ref, ss_ref, a_ref, b_ref, w_ref, x_ref, gate_ref, g2_ref, sc2_ref, sh2_ref,
                     rw_ref, o_ref, h_ref, aff_ref, *, norm):
    f = jnp.dot(cs_ref[...], a_ref[...], preferred_element_type=F32)
    f = f - jnp.dot(ss_ref[...], b_ref[...], preferred_element_type=F32)
    m = jnp.dot((f * norm).astype(BF16), w_ref[...], preferred_element_type=F32)
    x = x_ref[...] + gate_ref[...] * m
    o_ref[...] = x
    _route(x, g2_ref, sc2_ref, sh2_ref, rw_ref, h_ref, aff_ref)


def _fnet_layer(x3, g, modl, w_bf, g2, rw01, n_exp, tm=512):
    b, seq, d = x3.shape
    gw = d // FOURIER_GROUPS
    cc, sc = _dft_tables(gw)
    cs_chan = jnp.concatenate([jnp.asarray(cc), jnp.asarray(sc)], axis=1).astype(BF16)
    cseq, sseq = _dft_tables(seq)
    cseq = jnp.asarray(cseq).astype(BF16)
    sseq = jnp.asarray(sseq).astype(BF16)
    x2 = x3.reshape(b * seq, d)
    tps = seq // tm
    a, bm = pl.pallas_call(
        functools.partial(_fnet_chan_kernel, gw=gw),
        grid=(b * seq // tm,),
        in_specs=[
            pl.BlockSpec((tm, d), lambda i: (i, 0)),
            pl.BlockSpec((1, d), lambda i: (0, 0)),
            _row_spec(d, tps, 1),
            _row_spec(d, tps, 0),
            pl.BlockSpec((gw, 2 * gw), lambda i: (0, 0)),
        ],
        out_specs=[pl.BlockSpec((tm, d), lambda i: (i, 0))] * 2,
        out_shape=[jax.ShapeDtypeStruct((b * seq, d), BF16)] * 2,
        compiler_params=_params("parallel"),
        name="fnet_chan",
    )(x2, g, modl, modl, cs_chan)
    a3 = a.reshape(b, seq, d)
    b3 = bm.reshape(b, seq, d)
    norm = 1.0 / math.sqrt(seq * gw)
    r_in, r_out, r_shape = _route_specs(b, seq, d, n_exp, tm)
    tile = pl.BlockSpec((None, tm, d), lambda i, j: (i, j, 0))
    return pl.pallas_call(
        functools.partial(_fnet_seq_kernel, norm=norm),
        grid=(b, seq // tm),
        in_specs=[
            pl.BlockSpec((tm, seq), lambda i, j: (j, 0)),
            pl.BlockSpec((tm, seq), lambda i, j: (j, 0)),
            pl.BlockSpec((None, seq, d), lambda i, j: (i, 0, 0)),
            pl.BlockSpec((None, seq, d), lambda i, j: (i, 0, 0)),
            pl.BlockSpec((d, d), lambda i, j: (0, 0)),
            tile,
            pl.BlockSpec((None, None, 1, d), lambda i, j: (i, 2, 0, 0)),
        ] + r_in,
        out_specs=[tile] + r_out,
        out_shape=[jax.ShapeDtypeStruct((b, seq, d), F32)] + r_shape,
        compiler_params=_params("parallel", "parallel"),
        name="fnet_seq",
    )(cseq, sseq, a3, b3, w_bf, x3, modl, g2, modl, modl, rw01)


def _topk_kernel(aff_ref, tok_ref, gs_ref, tri_s, *, cap):
    n_exp, seq = aff_ref.shape

    @pl.when(pl.program_id(0) == 0)
    def _():
        r = lax.broadcasted_iota(jnp.int32, (seq, seq), 0)
        c = lax.broadcasted_iota(jnp.int32, (seq, seq), 1)
        tri_s[...] = jnp.where(r < c, 1.0, 0.0).astype(BF16)

    a = aff_ref[...]

    def search(i, t):
        cand = t | lax.shift_left(jnp.int32(1), 30 - i)
        cnt = jnp.sum((a >= pltpu.bitcast(cand, F32)).astype(jnp.int32), axis=1, keepdims=True)
        return jnp.where(cnt >= cap, cand, t)

    thr = lax.fori_loop(0, 31, search, jnp.zeros((a.shape[0], 1), jnp.int32))
    thr = pltpu.bitcast(thr, F32)
    gt = a > thr
    eq = a == thr
    need = cap - jnp.sum(gt.astype(jnp.int32), axis=1, keepdims=True)
    tri = tri_s[...]
    eq_before = jnp.dot(jnp.where(eq, 1.0, 0.0).astype(BF16), tri, preferred_element_type=F32)
    sel = gt | (eq & (eq_before < need.astype(F32)))
    before = jnp.dot(jnp.where(sel, 1.0, 0.0).astype(BF16), tri, preferred_element_type=F32)
    rank = jnp.where(sel, before.astype(jnp.int32), -1)

    slot = lax.broadcasted_iota(jnp.int32, (cap, seq), 0)
    tok_ids = lax.broadcasted_iota(jnp.int32, (1, seq), 1)
    t_hi = lax.shift_right_logical(tok_ids, 6).astype(F32)
    t_lo = (tok_ids & 63).astype(F32)
    pad = jnp.zeros((11, seq), F32)
    for e in range(n_exp):
        pick = jnp.where(slot == rank[e:e + 1, :], 1.0, 0.0).astype(BF16)
        g = a[e:e + 1, :]
        g0 = g.astype(BF16).astype(F32)
        g1 = (g - g0).astype(BF16).astype(F32)
        g2 = (g - g0) - g1
        lhs = jnp.concatenate([t_hi, t_lo, g0, g1, g2, pad], axis=0).astype(BF16)
        res = lax.dot_general(lhs, pick, _NT, preferred_element_type=F32)
        tok_ref[e:e + 1, :] = (res[0:1, :] * 64.0 + res[1:2, :]).astype(jnp.int32)
        gs_ref[e:e + 1, :] = (res[2:3, :] + res[3:4, :]) + res[4:5, :]


def _topk(aff, cap):
    b, e, seq = aff.shape
    out_spec = pl.BlockSpec((None, e, cap), lambda i: (i, 0, 0))
    return pl.pallas_call(
        functools.partial(_topk_kernel, cap=cap),
        grid=(b,),
        in_specs=[pl.BlockSpec((None, e, seq), lambda i: (i, 0, 0))],
        out_specs=[out_spec, out_spec],
        out_shape=[jax.ShapeDtypeStruct((b, e, cap), jnp.int32), jax.ShapeDtypeStruct((b, e, cap), F32)],
        scratch_shapes=[pltpu.VMEM((seq, seq), BF16)],
        compiler_params=_params("arbitrary"),
        name="topk",
    )(aff)


def _gather_kernel(tok_ref, h_ref, o_ref, x_s, *, stride):
    n_exp, cap = tok_ref.shape
    nch = x_s.shape[0] // stride

    def expert(e, carry):
        for i in range(cap):
            t = pl.multiple_of(tok_ref[e, i] * nch, nch)
            x_s[pl.ds(i, nch, stride=stride), :] = h_ref[pl.ds(t, nch), :]
        xe = jnp.concatenate([x_s[j * stride:j * stride + cap, :] for j in range(nch)], axis=-1)
        o_ref[e] = xe.astype(BF16)
        return carry

    lax.fori_loop(0, n_exp, expert, 0)


def _gather_tokens(h_slab, tok, d, stride):
    b, rows, _ = h_slab.shape
    _, e, cap = tok.shape
    nch = d // LANES
    return pl.pallas_call(
        functools.partial(_gather_kernel, stride=stride),
        grid=(b,),
        in_specs=[
            pl.BlockSpec((None, e, cap), lambda i: (i, 0, 0), memory_space=pltpu.SMEM),
            pl.BlockSpec((None, rows, LANES), lambda i: (i, 0, 0)),
        ],
        out_specs=pl.BlockSpec((e, None, cap, d), lambda i: (0, i, 0, 0)),
        out_shape=jax.ShapeDtypeStruct((e, b, cap, d), BF16),
        scratch_shapes=[pltpu.VMEM((nch * stride, LANES), F32)],
        compiler_params=_params("parallel"),
        name="moe_gather",
    )(tok, h_slab)


def _ffn_kernel(x_ref, wg_ref, wu_ref, wd_ref, y_ref, wg_s, wu_s, wd_s, *, cap, ff_chunk):
    @pl.when(pl.program_id(1) == 0)
    def _():
        wg_s[...] = wg_ref[...].astype(BF16)
        wu_s[...] = wu_ref[...].astype(BF16)
        wd_s[...] = wd_ref[...].astype(BF16)

    x = x_ref[...]
    y = None
    for c in range(wg_s.shape[1] // ff_chunk):
        cols = slice(c * ff_chunk, (c + 1) * ff_chunk)
        a = jnp.dot(x, wg_s[:, cols], preferred_element_type=F32)
        u = jnp.dot(x, wu_s[:, cols], preferred_element_type=F32)
        hm = (a * jax.nn.sigmoid(a) * u).astype(BF16)
        yc = jnp.dot(hm, wd_s[cols, :], preferred_element_type=F32)
        y = yc if y is None else y + yc
    bpb, nch, stride, _ = y_ref.shape
    for bb in range(bpb):
        for j in range(nch):
            y_ref[bb, j, 0:cap, :] = y[bb * cap:(bb + 1) * cap, j * LANES:(j + 1) * LANES]
        y_ref[bb, :, cap:, :] = jnp.zeros((nch, stride - cap, LANES), F32)


def _expert_ffn(xe, layer, wg, wu, wd, cap, stride, tm):
    e, m, d = xe.shape
    ff = wg.shape[3]
    nch = d // LANES
    bpb = tm // cap
    return pl.pallas_call(
        functools.partial(_ffn_kernel, cap=cap, ff_chunk=min(512, ff)),
        grid=(e, m // tm),
        in_specs=[
            pl.BlockSpec((None, tm, d), lambda i, j: (i, j, 0)),
            pl.BlockSpec((None, None, d, ff), lambda i, j: (layer, i, 0, 0)),
            pl.BlockSpec((None, None, d, ff), lambda i, j: (layer, i, 0, 0)),
            pl.BlockSpec((None, None, ff, d), lambda i, j: (layer, i, 0, 0)),
        ],
        out_specs=pl.BlockSpec((None, bpb, nch, stride, LANES), lambda i, j: (i, j, 0, 0, 0)),
        out_shape=jax.ShapeDtypeStruct((e, m // cap, nch, stride, LANES), F32),
        scratch_shapes=[pltpu.VMEM((d, ff), BF16), pltpu.VMEM((d, ff), BF16), pltpu.VMEM((ff, d), BF16)],
        compiler_params=_params("parallel", "arbitrary"),
        name="moe_ffn",
    )(xe, wg, wu, wd)


def _scatter_kernel(tok_ref, gs_ref, y_ref, x_ref, gate_ref, o_ref, acc_s, *, stride, n_exp):
    cap = tok_ref.shape[1]
    ts, d = x_ref.shape
    nch = d // LANES
    k = pl.program_id(1)

    @pl.when(k == 0)
    def _():
        acc_s[...] = jnp.zeros_like(acc_s)

    @pl.when(k < n_exp)
    def _():
        def slab(i):
            return pl.ds(pl.multiple_of(tok_ref[0, i] * nch, nch), nch)

        for i0 in range(0, cap, SCATTER_BATCH):
            vals = [acc_s[slab(i), :] + y_ref[pl.ds(i, nch, stride=stride), :] * gs_ref[0, i]
                    for i in range(i0, i0 + SCATTER_BATCH)]
            for i, v in zip(range(i0, i0 + SCATTER_BATCH), vals):
                acc_s[slab(i), :] = v

    @pl.when(k >= n_exp)
    def _():
        base = pl.multiple_of((k - n_exp) * (ts * nch), ts * nch)
        for j in range(nch):
            lanes = slice(j * LANES, (j + 1) * LANES)
            o_ref[:, lanes] = (x_ref[:, lanes]
                               + gate_ref[:, lanes] * acc_s[pl.ds(base + j, ts, stride=nch), :])


def _scatter_residual(y_cm, tok, gs, x3, modl, stride):
    e, b, _, _ = y_cm.shape
    _, seq, d = x3.shape
    cap = tok.shape[-1]
    nch = d // LANES
    assert cap % SCATTER_BATCH == 0
    ts = min(512, seq)
    smem_row = pl.BlockSpec((None, None, 1, cap), lambda i, k: (i, jnp.minimum(k, e - 1), 0, 0),
                            memory_space=pltpu.SMEM)
    tile = pl.BlockSpec((None, ts, d), lambda i, k: (i, jnp.maximum(k - e, 0), 0))
    return pl.pallas_call(
        functools.partial(_scatter_kernel, stride=stride, n_exp=e),
        grid=(b, e + seq // ts),
        in_specs=[
            smem_row, smem_row,
            pl.BlockSpec((None, None, nch * stride, LANES), lambda i, k: (jnp.minimum(k, e - 1), i, 0, 0)),
            tile,
            pl.BlockSpec((None, None, 1, d), lambda i, k: (i, 5, 0, 0)),
        ],
        out_specs=tile,
        out_shape=jax.ShapeDtypeStruct((b, seq, d), F32),
        scratch_shapes=[pltpu.VMEM((seq * nch, LANES), F32)],
        compiler_params=_params("parallel", "arbitrary"),
        name="moe_scatter",
    )(tok.reshape(b, e, 1, cap), gs.reshape(b, e, 1, cap), y_cm, x3, modl)


def _moe_layer(x3, h_slab, aff, modl, layer, wg, wu, wd):
    b, seq, d = x3.shape
    e = aff.shape[1]
    cap = max(1, CAPACITY_FACTOR * seq // e)
    stride = cap + 8
    tok, gs = _topk(aff, cap)
    xe = _gather_tokens(h_slab, tok, d, stride)
    y = _expert_ffn(xe.reshape(e, b * cap, d), layer, wg, wu, wd, cap, stride, tm=min(1024, b * cap))
    return _scatter_residual(y.reshape(e, b, -1, LANES), tok, gs, x3, modl, stride)


def kernel(x, c, mod_w, mod_b, norm_g, mix_w_in, qk_g, rpb, conv_w, mix_w_out, fnet_w_out,
           router_w, exp_w_gate, exp_w_up, exp_w_down):
    b, seq, d = x.shape
    depth = mod_w.shape[0]
    mod = _modulation(c, mod_w, mod_b).reshape(depth, b, N_MOD, 1, d)
    for l in range(depth):
        modl = mod[l]
        g1 = norm_g[l, 0].reshape(1, d)
        g2 = norm_g[l, 1].reshape(1, d)
        rw01 = _router_weights(router_w[l])
        n_exp = router_w.shape[2]
        if l % 2 == 0:
            j = l // 2
            proj = _in_projection(x.reshape(b * seq, d), g1, modl, mix_w_in[j].astype(BF16), seq)
            attn, conv = _mixers(proj.reshape(b, seq, -1), qk_g[j], rpb[j], conv_w[j])
            x, h_slab, aff = _out_projection(attn, conv, mix_w_out[j].astype(BF16), x, modl,
                                             g2, rw01, n_exp)
        else:
            x, h_slab, aff = _fnet_layer(x, g1, modl, fnet_w_out[l // 2].astype(BF16), g2, rw01, n_exp)
        x = _moe_layer(x, h_slab, aff, modl, l, exp_w_gate, exp_w_up, exp_w_down)
    return x
```

```python
import functools
import math

import numpy as np
import jax
import jax.numpy as jnp
from jax import lax
from jax.experimental import pallas as pl
from jax.experimental.pallas import tpu as pltpu

F32 = jnp.float32
BF16 = jnp.bfloat16
HIGHEST = lax.Precision.HIGHEST

GRID_W = 64
WIN_ROWS = 8
WIN_COLS = 16
HEAD_DIM = 64
FOURIER_GROUPS = 4
N_MOD = 6
CAPACITY_FACTOR = 2
RMS_EPS = 1e-6
NEG_BIAS = -1e30
LANES = 128
ROWS_PER_STEP = WIN_ROWS // 2
BAND_ROWS = WIN_ROWS + ROWS_PER_STEP
SCATTER_BATCH = 8
SCATTER_EXPERTS = 4
SLOT_SHIFT = 4
SLOT_RADIX = 1 << SLOT_SHIFT
VMEM_LIMIT = 56 * 1024 * 1024

_NT = (((1,), (1,)), ((), ()))
_TN = (((0,), (0,)), ((), ()))


def _params(*sem):
    return pltpu.CompilerParams(dimension_semantics=sem, vmem_limit_bytes=VMEM_LIMIT)


def _ln_mod(x, g, sc, sh):
    y = x * lax.rsqrt(jnp.mean(x * x, axis=-1, keepdims=True) + RMS_EPS)
    return (y * g) * (1.0 + sc) + sh


def _mod_kernel(c_ref, w_ref, b_ref, o_ref):
    c = c_ref[...]
    s = c * jax.nn.sigmoid(c)
    o_ref[...] = jnp.dot(s, w_ref[...], precision=HIGHEST, preferred_element_type=F32) + b_ref[...]


def _modulation(c, mod_w, mod_b):
    depth, d, n = mod_w.shape
    b = c.shape[0]
    tn = 1536
    return pl.pallas_call(
        _mod_kernel,
        grid=(depth, n // tn),
        in_specs=[
            pl.BlockSpec((b, d), lambda l, j: (0, 0)),
            pl.BlockSpec((None, d, tn), lambda l, j: (l, 0, j)),
            pl.BlockSpec((None, 1, tn), lambda l, j: (l, 0, j)),
        ],
        out_specs=pl.BlockSpec((None, b, tn), lambda l, j: (l, 0, j)),
        out_shape=jax.ShapeDtypeStruct((depth, b, n), F32),
        compiler_params=_params("parallel", "parallel"),
        name="adaln_mod",
    )(c, mod_w, mod_b.reshape(depth, 1, n))


def _row_spec(d, tiles_per_seq, k):
    return pl.BlockSpec((None, None, 1, d), lambda i: (i // tiles_per_seq, k, 0, 0))


def _inproj_kernel(x_ref, g_ref, sc_ref, sh_ref, w_ref, o_ref):
    h = _ln_mod(x_ref[...], g_ref[...], sc_ref[...], sh_ref[...]).astype(BF16)
    o_ref[...] = jnp.dot(h, w_ref[...], preferred_element_type=F32)


def _in_projection(x2, g, modl, w_bf, seq, tm=512):
    t, d = x2.shape
    n = w_bf.shape[1]
    tps = seq // tm
    return pl.pallas_call(
        _inproj_kernel,
        grid=(t // tm,),
        in_specs=[
            pl.BlockSpec((tm, d), lambda i: (i, 0)),
            pl.BlockSpec((1, d), lambda i: (0, 0)),
            _row_spec(d, tps, 1),
            _row_spec(d, tps, 0),
            pl.BlockSpec((d, n), lambda i: (0, 0)),
        ],
        out_specs=pl.BlockSpec((tm, n), lambda i: (i, 0)),
        out_shape=jax.ShapeDtypeStruct((t, n), F32),
        compiler_params=_params("parallel"),
        name="in_proj",
    )(x2, g, modl, modl, w_bf)


def _bias_table(rpb):
    heads, nr, nc = rpb.shape
    assert nr == 2 * WIN_ROWS - 1 and nc == 2 * WIN_COLS - 1
    left = GRID_W - WIN_COLS
    w = jnp.pad(rpb, ((0, 0), (0, 0), (left, 2 * GRID_W - left - nc)))
    flat = jnp.tile(w, (1, 1, GRID_W))[..., :GRID_W * (2 * GRID_W - 1)]
    toe = flat.reshape(heads, nr, GRID_W, 2 * GRID_W - 1)[..., GRID_W - 1:]
    qc = np.arange(GRID_W)[:, None]
    kc = np.arange(GRID_W)[None, :]
    cs = np.clip(qc - WIN_COLS // 2, 0, GRID_W - WIN_COLS)
    toe = jnp.where((kc >= cs) & (kc < cs + WIN_COLS), toe, NEG_BIAS)
    place = [[(u, 0) for u in range(ROWS_PER_STEP)],
             [(WIN_ROWS // 2 + u, u) for u in range(ROWS_PER_STEP)],
             [(BAND_ROWS - ROWS_PER_STEP + u, BAND_ROWS - WIN_ROWS) for u in range(ROWS_PER_STEP)]]
    masked = jnp.full((heads, GRID_W, GRID_W), NEG_BIAS, F32)
    cases = []
    for per_u in place:
        blocks = []
        for off, lo in per_u:
            first = lo - off + WIN_ROWS - 1
            row = [toe[:, first + i - lo] if lo <= i < lo + WIN_ROWS else masked
                   for i in range(BAND_ROWS)]
            blocks.append(jnp.concatenate(row, axis=-1))
        cases.append(jnp.concatenate(blocks, axis=1))
    return jnp.stack(cases, axis=1)


def _mixers_kernel(q_ref, k_ref, v_ref, bg_ref, cg_ref, xv_ref, qkg_ref, tbl_ref, cw_ref,
                   attn_ref, conv_ref, q0_s, q1_s, k_s, v_s, pad_s, *, rows):
    seq = q_ref.shape[0]
    lane = lax.broadcasted_iota(jnp.int32, (1, LANES), 1)
    first = lane < HEAD_DIM

    def head_norm(x, g):
        xx = x * x
        s0 = jnp.sum(jnp.where(first, xx, 0.0), axis=-1, keepdims=True)
        s1 = jnp.sum(jnp.where(first, 0.0, xx), axis=-1, keepdims=True)
        ms = jnp.where(first, s0, s1) * (1.0 / HEAD_DIM)
        return x * lax.rsqrt(ms + RMS_EPS) * g

    qn = head_norm(q_ref[...], qkg_ref[0:1, :]) * (1.0 / math.sqrt(HEAD_DIM))
    q0_s[...] = jnp.where(first, qn, 0.0).astype(BF16)
    q1_s[...] = jnp.where(first, 0.0, qn).astype(BF16)
    k_s[...] = head_norm(k_ref[...], qkg_ref[1:2, :]).astype(BF16)
    v_s[...] = v_ref[...].astype(BF16)

    n_groups = rows // ROWS_PER_STEP
    nq = ROWS_PER_STEP * GRID_W
    for g in range(n_groups):
        b0 = min(max(g * ROWS_PER_STEP - WIN_ROWS // 2, 0), rows - BAND_ROWS)
        case = 0 if g == 0 else (2 if g == n_groups - 1 else 1)
        kb = k_s[b0 * GRID_W:(b0 + BAND_ROWS) * GRID_W, :]
        vb = v_s[b0 * GRID_W:(b0 + BAND_ROWS) * GRID_W, :]
        outs = []
        for h, qs in enumerate((q0_s, q1_s)):
            s = lax.dot_general(qs[g * nq:(g + 1) * nq, :], kb, _NT, preferred_element_type=F32)
            s = s + tbl_ref[h, case]
            p = jnp.exp(s - jnp.max(s, axis=-1, keepdims=True))
            l = jnp.sum(p, axis=-1, keepdims=True)
            outs.append(jnp.dot(p.astype(BF16), vb, preferred_element_type=F32) / l)
        attn_ref[g * nq:(g + 1) * nq, :] = jnp.where(first, outs[0], outs[1])

    zeros = jnp.zeros((8, LANES), F32)
    pad_s[0:8, :] = zeros
    pad_s[seq + 8:seq + 16, :] = zeros
    pad_s[8:seq + 8, :] = cg_ref[...] * xv_ref[...]
    z = (pad_s[7:seq + 7, :] * cw_ref[0:1, :] + pad_s[8:seq + 8, :] * cw_ref[1:2, :]
         + pad_s[9:seq + 9, :] * cw_ref[2:3, :])
    conv_ref[...] = bg_ref[...] * z


def _mixers(proj3, qk_g, rpb, conv_w):
    b, seq, n = proj3.shape
    width = n // 6
    nblk = width // LANES
    heads = width // HEAD_DIM
    rows = seq // GRID_W
    assert rows >= BAND_ROWS + ROWS_PER_STEP and rows % ROWS_PER_STEP == 0 and rpb.shape[0] == heads
    tbl = _bias_table(rpb)
    qkg = jnp.tile(qk_g, (1, LANES // HEAD_DIM))

    def col(k):
        return pl.BlockSpec((None, seq, LANES), lambda i, j, k=k: (i, 0, k * nblk + j))

    out_spec = pl.BlockSpec((None, seq, LANES), lambda i, j: (i, 0, j))
    hp = LANES // HEAD_DIM
    return pl.pallas_call(
        functools.partial(_mixers_kernel, rows=rows),
        grid=(b, nblk),
        in_specs=[col(0), col(1), col(2), col(3), col(4), col(5),
                  pl.BlockSpec((2, LANES), lambda i, j: (0, 0)),
                  pl.BlockSpec((hp,) + tbl.shape[1:], lambda i, j: (j, 0, 0, 0)),
                  pl.BlockSpec((conv_w.shape[0], LANES), lambda i, j: (0, j))],
        out_specs=[out_spec, out_spec],
        out_shape=[jax.ShapeDtypeStruct((b, seq, width), F32)] * 2,
        scratch_shapes=[pltpu.VMEM((seq, LANES), BF16)] * 4 + [pltpu.VMEM((seq + 16, LANES), F32)],
        compiler_params=_params("parallel", "parallel"),
        name="attn_conv",
    )(proj3, proj3, proj3, proj3, proj3, proj3, qkg, tbl, conv_w)


def _route(x, g_ref, sc_ref, sh_ref, rw_ref, h_ref, aff_ref):
    h = _ln_mod(x, g_ref[...], sc_ref[...], sh_ref[...])
    tm, d = h.shape
    nch = d // LANES
    n_exp = aff_ref.shape[0]
    for j in range(nch):
        h_ref[pl.ds(j, tm, stride=nch), :] = h[:, j * LANES:(j + 1) * LANES]
    h0 = h.astype(BF16)
    h1 = (h - h0.astype(F32)).astype(BF16)
    lg = jnp.dot(h0, rw_ref[...], preferred_element_type=F32)
    lg = (lg[:, :LANES] + lg[:, LANES:]) + jnp.dot(h1, rw_ref[:, :LANES], preferred_element_type=F32)
    lane = lax.broadcasted_iota(jnp.int32, (1, LANES), 1)
    lg = jnp.where(lane < n_exp, lg, NEG_BIAS)
    p = jnp.exp(lg - jnp.max(lg, axis=-1, keepdims=True))
    aff = p / jnp.sum(p, axis=-1, keepdims=True)
    aff_ref[...] = aff.T[:n_exp, :]


def _router_weights(router_w):
    w = jnp.pad(router_w, ((0, 0), (0, LANES - router_w.shape[1])))
    w0 = w.astype(BF16)
    w1 = (w - w0.astype(F32)).astype(BF16)
    return jnp.concatenate([w0, w1], axis=1)


def _route_specs(b, seq, d, e, tm):
    nch = d // LANES
    in_specs = [pl.BlockSpec((1, d), lambda i, j: (0, 0)),
                pl.BlockSpec((None, None, 1, d), lambda i, j: (i, 4, 0, 0)),
                pl.BlockSpec((None, None, 1, d), lambda i, j: (i, 3, 0, 0)),
                pl.BlockSpec((d, 2 * LANES), lambda i, j: (0, 0))]
    out_specs = [pl.BlockSpec((None, tm * nch, LANES), lambda i, j: (i, j, 0)),
                 pl.BlockSpec((None, e, tm), lambda i, j: (i, 0, j))]
    out_shape = [jax.ShapeDtypeStruct((b, seq * nch, LANES), F32),
                 jax.ShapeDtypeStruct((b, e, seq), F32)]
    return in_specs, out_specs, out_shape


def _outproj_kernel(a_ref, c_ref, wa_ref, wc_ref, x_ref, gate_ref, g2_ref, sc2_ref, sh2_ref, rw_ref,
                    o_ref, h_ref, aff_ref):
    m = jnp.dot(a_ref[...].astype(BF16), wa_ref[...], preferred_element_type=F32)
    m = m + jnp.dot(c_ref[...].astype(BF16), wc_ref[...], preferred_element_type=F32)
    x = x_ref[...] + gate_ref[...] * m
    o_ref[...] = x
    _route(x, g2_ref, sc2_ref, sh2_ref, rw_ref, h_ref, aff_ref)


def _out_projection(attn3, conv3, w_bf, x3, modl, g2, rw01, n_exp, tm=512):
    b, seq, d = x3.shape
    wa = attn3.shape[2]
    wc = conv3.shape[2]
    r_in, r_out, r_shape = _route_specs(b, seq, d, n_exp, tm)
    tile = pl.BlockSpec((None, tm, d), lambda i, j: (i, j, 0))
    return pl.pallas_call(
        _outproj_kernel,
        grid=(b, seq // tm),
        in_specs=[
            pl.BlockSpec((None, tm, wa), lambda i, j: (i, j, 0)),
            pl.BlockSpec((None, tm, wc), lambda i, j: (i, j, 0)),
            pl.BlockSpec((wa, d), lambda i, j: (0, 0)),
            pl.BlockSpec((wc, d), lambda i, j: (1, 0)),
            tile,
            pl.BlockSpec((None, None, 1, d), lambda i, j: (i, 2, 0, 0)),
        ] + r_in,
        out_specs=[tile] + r_out,
        out_shape=[jax.ShapeDtypeStruct((b, seq, d), F32)] + r_shape,
        compiler_params=_params("parallel", "parallel"),
        name="out_proj",
    )(attn3, conv3, w_bf, w_bf, x3, modl, g2, modl, modl, rw01)


def _dft_tables(n):
    jk = (np.arange(n)[:, None] * np.arange(n)[None, :]) % n
    ang = 2.0 * np.pi * jk.astype(np.float64) / n
    return np.cos(ang).astype(np.float32), np.sin(ang).astype(np.float32)


def _fnet_chan_kernel(x_ref, g_ref, sc_ref, sh_ref, cs_ref, a_ref, b_ref, *, gw):
    h = _ln_mod(x_ref[...], g_ref[...], sc_ref[...], sh_ref[...]).astype(BF16)
    for g in range(h.shape[1] // gw):
        ab = jnp.dot(h[:, g * gw:(g + 1) * gw], cs_ref[...], preferred_element_type=F32)
        a_ref[:, g * gw:(g + 1) * gw] = ab[:, :gw].astype(BF16)
        b_ref[:, g * gw:(g + 1) * gw] = ab[:, gw:].astype(BF16)


def _fnet_seq_kernel(cs_ref, ss_ref, a_ref, b_ref, w_ref, x_ref, gate_ref, g2_ref, sc2_ref, sh2_ref,
                     rw_ref, o_ref, h_ref, aff_ref, *, norm):
    f = jnp.dot(cs_ref[...], a_ref[...], preferred_element_type=F32)
    f = f - jnp.dot(ss_ref[...], b_ref[...], preferred_element_type=F32)
    m = jnp.dot((f * norm).astype(BF16), w_ref[...], preferred_element_type=F32)
    x = x_ref[...] + gate_ref[...] * m
    o_ref[...] = x
    _route(x, g2_ref, sc2_ref, sh2_ref, rw_ref, h_ref, aff_ref)


def _fnet_layer(x3, g, modl, w_bf, g2, rw01, n_exp, tm=512):
    b, seq, d = x3.shape
    gw = d // FOURIER_GROUPS
    cc, sc = _dft_tables(gw)
    cs_chan = jnp.concatenate([jnp.asarray(cc), jnp.asarray(sc)], axis=1).astype(BF16)
    cseq, sseq = _dft_tables(seq)
    cseq = jnp.asarray(cseq).astype(BF16)
    sseq = jnp.asarray(sseq).astype(BF16)
    x2 = x3.reshape(b * seq, d)
    tps = seq // tm
    a, bm = pl.pallas_call(
        functools.partial(_fnet_chan_kernel, gw=gw),
        grid=(b * seq // tm,),
        in_specs=[
            pl.BlockSpec((tm, d), lambda i: (i, 0)),
            pl.BlockSpec((1, d), lambda i: (0, 0)),
            _row_spec(d, tps, 1),
            _row_spec(d, tps, 0),
            pl.BlockSpec((gw, 2 * gw), lambda i: (0, 0)),
        ],
        out_specs=[pl.BlockSpec((tm, d), lambda i: (i, 0))] * 2,
        out_shape=[jax.ShapeDtypeStruct((b * seq, d), BF16)] * 2,
        compiler_params=_params("parallel"),
        name="fnet_chan",
    )(x2, g, modl, modl, cs_chan)
    a3 = a.reshape(b, seq, d)
    b3 = bm.reshape(b, seq, d)
    norm = 1.0 / math.sqrt(seq * gw)
    r_in, r_out, r_shape = _route_specs(b, seq, d, n_exp, tm)
    tile = pl.BlockSpec((None, tm, d), lambda i, j: (i, j, 0))
    return pl.pallas_call(
        functools.partial(_fnet_seq_kernel, norm=norm),
        grid=(b, seq // tm),
        in_specs=[
            pl.BlockSpec((tm, seq), lambda i, j: (j, 0)),
            pl.BlockSpec((tm, seq), lambda i, j: (j, 0)),
            pl.BlockSpec((None, seq, d), lambda i, j: (i, 0, 0)),
            pl.BlockSpec((None, seq, d), lambda i, j: (i, 0, 0)),
            pl.BlockSpec((d, d), lambda i, j: (0, 0)),
            tile,
            pl.BlockSpec((None, None, 1, d), lambda i, j: (i, 2, 0, 0)),
        ] + r_in,
        out_specs=[tile] + r_out,
        out_shape=[jax.ShapeDtypeStruct((b, seq, d), F32)] + r_shape,
        compiler_params=_params("parallel", "parallel"),
        name="fnet_seq",
    )(cseq, sseq, a3, b3, w_bf, x3, modl, g2, modl, modl, rw01)


def _topk_kernel(aff_ref, tok_ref, gs_ref, tri_s, *, cap):
    n_exp, seq = aff_ref.shape

    @pl.when(pl.program_id(0) == 0)
    def _():
        r = lax.broadcasted_iota(jnp.int32, (seq, seq), 0)
        c = lax.broadcasted_iota(jnp.int32, (seq, seq), 1)
        tri_s[...] = jnp.where(r < c, 1.0, 0.0).astype(BF16)

    a = aff_ref[...]

    def search(i, t):
        cand = t | lax.shift_left(jnp.int32(1), 30 - i)
        cnt = jnp.sum((a >= pltpu.bitcast(cand, F32)).astype(jnp.int32), axis=1, keepdims=True)
        return jnp.where(cnt >= cap, cand, t)

    thr = lax.fori_loop(0, 31, search, jnp.zeros((a.shape[0], 1), jnp.int32))
    thr = pltpu.bitcast(thr, F32)
    gt = a > thr
    eq = a == thr
    need = cap - jnp.sum(gt.astype(jnp.int32), axis=1, keepdims=True)
    tri = tri_s[...]
    eq_before = jnp.dot(jnp.where(eq, 1.0, 0.0).astype(BF16), tri, preferred_element_type=F32)
    sel = gt | (eq & (eq_before < need.astype(F32)))
    before = jnp.dot(jnp.where(sel, 1.0, 0.0).astype(BF16), tri, preferred_element_type=F32)
    rank = jnp.where(sel, before.astype(jnp.int32), -1)

    digit = lax.broadcasted_iota(jnp.int32, (SLOT_RADIX, seq), 0)
    tok_ids = lax.broadcasted_iota(jnp.int32, (1, seq), 1)
    t_hi = lax.shift_right_logical(tok_ids, 6).astype(F32)
    t_lo = (tok_ids & 63).astype(F32)
    for e in range(n_exp):
        r = rank[e:e + 1, :]
        hi = jnp.where(digit == lax.shift_right_arithmetic(r, SLOT_SHIFT), 1.0, 0.0).astype(BF16)
        lo = digit == (r & (SLOT_RADIX - 1))
        g = a[e:e + 1, :]
        g0 = g.astype(BF16).astype(F32)
        g1 = (g - g0).astype(BF16).astype(F32)
        g2 = (g - g0) - g1
        vals = jnp.concatenate([jnp.where(lo, v, 0.0) for v in (t_hi, t_lo, g0, g1, g2)], axis=0)
        res = lax.dot_general(hi, vals.astype(BF16), _NT, preferred_element_type=F32)
        part = [res[:, k * SLOT_RADIX:(k + 1) * SLOT_RADIX] for k in range(5)]
        tok_ref[e] = (part[0] * 64.0 + part[1]).astype(jnp.int32)
        gs_ref[e] = (part[2] + part[3]) + part[4]


def _topk(aff, cap):
    b, e, seq = aff.shape
    n_hi = cap // SLOT_RADIX
    assert cap % SLOT_RADIX == 0 and n_hi <= SLOT_RADIX and seq <= 64 * 64
    out_spec = pl.BlockSpec((None, e, SLOT_RADIX, SLOT_RADIX), lambda i: (i, 0, 0, 0))
    tok, gs = pl.pallas_call(
        functools.partial(_topk_kernel, cap=cap),
        grid=(b,),
        in_specs=[pl.BlockSpec((None, e, seq), lambda i: (i, 0, 0))],
        out_specs=[out_spec, out_spec],
        out_shape=[jax.ShapeDtypeStruct((b, e, SLOT_RADIX, SLOT_RADIX), jnp.int32),
                   jax.ShapeDtypeStruct((b, e, SLOT_RADIX, SLOT_RADIX), F32)],
        scratch_shapes=[pltpu.VMEM((seq, seq), BF16)],
        compiler_params=_params("arbitrary"),
        name="topk",
    )(aff)
    return tok[:, :, :n_hi].reshape(b, e, cap), gs[:, :, :n_hi].reshape(b, e, cap)


def _gather_kernel(tok_ref, h_ref, o_ref, x_s, *, stride):
    n_exp, cap = tok_ref.shape
    nch = x_s.shape[0] // stride

    def expert(e, carry):
        for i in range(cap):
            t = pl.multiple_of(tok_ref[e, i] * nch, nch)
            x_s[pl.ds(i, nch, stride=stride), :] = h_ref[pl.ds(t, nch), :]
        xe = jnp.concatenate([x_s[j * stride:j * stride + cap, :] for j in range(nch)], axis=-1)
        o_ref[e] = xe.astype(BF16)
        return carry

    lax.fori_loop(0, n_exp, expert, 0)


def _gather_tokens(h_slab, tok, d, stride):
    b, rows, _ = h_slab.shape
    _, e, cap = tok.shape
    nch = d // LANES
    return pl.pallas_call(
        functools.partial(_gather_kernel, stride=stride),
        grid=(b,),
        in_specs=[
            pl.BlockSpec((None, e, cap), lambda i: (i, 0, 0), memory_space=pltpu.SMEM),
            pl.BlockSpec((None, rows, LANES), lambda i: (i, 0, 0)),
        ],
        out_specs=pl.BlockSpec((e, None, cap, d), lambda i: (0, i, 0, 0)),
        out_shape=jax.ShapeDtypeStruct((e, b, cap, d), BF16),
        scratch_shapes=[pltpu.VMEM((nch * stride, LANES), F32)],
        compiler_params=_params("parallel"),
        name="moe_gather",
    )(tok, h_slab)


def _ffn_kernel(x_ref, wg_ref, wu_ref, wd_ref, y_ref, wg_s, wu_s, wd_s, *, cap, ff_chunk):
    @pl.when(pl.program_id(1) == 0)
    def _():
        wg_s[...] = wg_ref[...].astype(BF16)
        wu_s[...] = wu_ref[...].astype(BF16)
        wd_s[...] = wd_ref[...].astype(BF16)

    x = x_ref[...]
    y = None
    for c in range(wg_s.shape[1] // ff_chunk):
        cols = slice(c * ff_chunk, (c + 1) * ff_chunk)
        a = jnp.dot(x, wg_s[:, cols], preferred_element_type=F32)
        u = jnp.dot(x, wu_s[:, cols], preferred_element_type=F32)
        hm = (a * jax.nn.sigmoid(a) * u).astype(BF16)
        yc = jnp.dot(hm, wd_s[cols, :], preferred_element_type=F32)
        y = yc if y is None else y + yc
    bpb, nch, stride, _ = y_ref.shape
    for bb in range(bpb):
        for j in range(nch):
            y_ref[bb, j, 0:cap, :] = y[bb * cap:(bb + 1) * cap, j * LANES:(j + 1) * LANES]
        y_ref[bb, :, cap:, :] = jnp.zeros((nch, stride - cap, LANES), F32)


def _expert_ffn(xe, layer, wg, wu, wd, cap, stride, tm):
    e, m, d = xe.shape
    ff = wg.shape[3]
    nch = d // LANES
    bpb = tm // cap
    return pl.pallas_call(
        functools.partial(_ffn_kernel, cap=cap, ff_chunk=min(512, ff)),
        grid=(e, m // tm),
        in_specs=[
            pl.BlockSpec((None, tm, d), lambda i, j: (i, j, 0)),
            pl.BlockSpec((None, None, d, ff), lambda i, j: (layer, i, 0, 0)),
            pl.BlockSpec((None, None, d, ff), lambda i, j: (layer, i, 0, 0)),
            pl.BlockSpec((None, None, ff, d), lambda i, j: (layer, i, 0, 0)),
        ],
        out_specs=pl.BlockSpec((None, bpb, nch, stride, LANES), lambda i, j: (i, j, 0, 0, 0)),
        out_shape=jax.ShapeDtypeStruct((e, m // cap, nch, stride, LANES), F32),
        scratch_shapes=[pltpu.VMEM((d, ff), BF16), pltpu.VMEM((d, ff), BF16), pltpu.VMEM((ff, d), BF16)],
        compiler_params=_params("parallel", "arbitrary"),
        name="moe_ffn",
    )(xe, wg, wu, wd)


def _scatter_kernel(tok_ref, gs_ref, y_ref, x_ref, gate_ref, o_ref, acc_s, *, stride, n_steps):
    group, _, cap = tok_ref.shape
    ts, d = x_ref.shape
    nch = d // LANES
    k = pl.program_id(1)

    @pl.when(k == 0)
    def _():
        acc_s[...] = jnp.zeros_like(acc_s)

    @pl.when(k < n_steps)
    def _():
        def expert(ee, carry):
            def slab(i):
                return pl.ds(pl.multiple_of(tok_ref[ee, 0, i] * nch, nch), nch)

            for i0 in range(0, cap, SCATTER_BATCH):
                vals = [acc_s[slab(i), :]
                        + y_ref[ee, pl.ds(i, nch, stride=stride), :] * gs_ref[ee, 0, i]
                        for i in range(i0, i0 + SCATTER_BATCH)]
                for i, v in zip(range(i0, i0 + SCATTER_BATCH), vals):
                    acc_s[slab(i), :] = v
            return carry

        lax.fori_loop(0, group, expert, 0)

    @pl.when(k >= n_steps)
    def _():
        base = pl.multiple_of((k - n_steps) * (ts * nch), ts * nch)
        for j in range(nch):
            lanes = slice(j * LANES, (j + 1) * LANES)
            o_ref[:, lanes] = (x_ref[:, lanes]
                               + gate_ref[:, lanes] * acc_s[pl.ds(base + j, ts, stride=nch), :])


def _scatter_residual(y_cm, tok, gs, x3, modl, stride):
    e, b, _, _ = y_cm.shape
    _, seq, d = x3.shape
    cap = tok.shape[-1]
    nch = d // LANES
    assert cap % SCATTER_BATCH == 0
    ts = min(512, seq)
    group = SCATTER_EXPERTS
    assert e % group == 0
    n_steps = e // group
    smem_rows = pl.BlockSpec((None, group, 1, cap), lambda i, k: (i, jnp.minimum(k, n_steps - 1), 0, 0),
                             memory_space=pltpu.SMEM)
    tile = pl.BlockSpec((None, ts, d), lambda i, k: (i, jnp.maximum(k - n_steps, 0), 0))
    return pl.pallas_call(
        functools.partial(_scatter_kernel, stride=stride, n_steps=n_steps),
        grid=(b, n_steps + seq // ts),
        in_specs=[
            smem_rows, smem_rows,
            pl.BlockSpec((group, None, nch * stride, LANES),
                         lambda i, k: (jnp.minimum(k, n_steps - 1), i, 0, 0)),
            tile,
            pl.BlockSpec((None, None, 1, d), lambda i, k: (i, 5, 0, 0)),
        ],
        out_specs=tile,
        out_shape=jax.ShapeDtypeStruct((b, seq, d), F32),
        scratch_shapes=[pltpu.VMEM((seq * nch, LANES), F32)],
        compiler_params=_params("parallel", "arbitrary"),
        name="moe_scatter",
    )(tok.reshape(b, e, 1, cap), gs.reshape(b, e, 1, cap), y_cm, x3, modl)


def _moe_layer(x3, h_slab, aff, modl, layer, wg, wu, wd):
    b, seq, d = x3.shape
    e = aff.shape[1]
    cap = max(1, CAPACITY_FACTOR * seq // e)
    stride = cap + 8
    tok, gs = _topk(aff, cap)
    xe = _gather_tokens(h_slab, tok, d, stride)
    y = _expert_ffn(xe.reshape(e, b * cap, d), layer, wg, wu, wd, cap, stride, tm=min(1024, b * cap))
    return _scatter_residual(y.reshape(e, b, -1, LANES), tok, gs, x3, modl, stride)


def kernel(x, c, mod_w, mod_b, norm_g, mix_w_in, qk_g, rpb, conv_w, mix_w_out, fnet_w_out,
           router_w, exp_w_gate, exp_w_up, exp_w_down):
    b, seq, d = x.shape
    depth = mod_w.shape[0]
    mod = _modulation(c, mod_w, mod_b).reshape(depth, b, N_MOD, 1, d)
    for l in range(depth):
        modl = mod[l]
        g1 = norm_g[l, 0].reshape(1, d)
        g2 = norm_g[l, 1].reshape(1, d)
        rw01 = _router_weights(router_w[l])
        n_exp = router_w.shape[2]
        if l % 2 == 0:
            j = l // 2
            proj = _in_projection(x.reshape(b * seq, d), g1, modl, mix_w_in[j].astype(BF16), seq)
            attn, conv = _mixers(proj.reshape(b, seq, -1), qk_g[j], rpb[j], conv_w[j])
            x, h_slab, aff = _out_projection(attn, conv, mix_w_out[j].astype(BF16), x, modl,
                                             g2, rw01, n_exp)
        else:
            x, h_slab, aff = _fnet_layer(x, g1, modl, fnet_w_out[l // 2].astype(BF16), g2, rw01, n_exp)
        x = _moe_layer(x, h_slab, aff, modl, l, exp_w_gate, exp_w_up, exp_w_down)
    return x
```

```python
import functools
import math

import numpy as np
import jax
import jax.numpy as jnp
from jax import lax
from jax.experimental import pallas as pl
from jax.experimental.pallas import tpu as pltpu

F32 = jnp.float32
BF16 = jnp.bfloat16
HIGHEST = lax.Precision.HIGHEST

GRID_W = 64
WIN_ROWS = 8
WIN_COLS = 16
HEAD_DIM = 64
FOURIER_GROUPS = 4
N_MOD = 6
CAPACITY_FACTOR = 2
RMS_EPS = 1e-6
NEG_BIAS = -1e30
LANES = 128
ROWS_PER_STEP = WIN_ROWS // 2
BAND_ROWS = WIN_ROWS + ROWS_PER_STEP
COL_BLOCK = WIN_COLS
KEY_COLS = 2 * WIN_COLS
SCATTER_BATCH = 8
SCATTER_EXPERTS = 8
SLOT_SHIFT = 4
SLOT_RADIX = 1 << SLOT_SHIFT
VMEM_LIMIT = 56 * 1024 * 1024

_NT = (((1,), (1,)), ((), ()))
_TN = (((0,), (0,)), ((), ()))


def _params(*sem):
    return pltpu.CompilerParams(dimension_semantics=sem, vmem_limit_bytes=VMEM_LIMIT)


def _ln_mod(x, g, sc, sh):
    y = x * lax.rsqrt(jnp.mean(x * x, axis=-1, keepdims=True) + RMS_EPS)
    return (y * g) * (1.0 + sc) + sh


def _mod_kernel(c_ref, w_ref, b_ref, o_ref):
    c = c_ref[...]
    s = c * jax.nn.sigmoid(c)
    o_ref[...] = jnp.dot(s, w_ref[...], precision=HIGHEST, preferred_element_type=F32) + b_ref[...]


def _modulation(c, mod_w, mod_b):
    depth, d, n = mod_w.shape
    b = c.shape[0]
    tn = 1536
    return pl.pallas_call(
        _mod_kernel,
        grid=(depth, n // tn),
        in_specs=[
            pl.BlockSpec((b, d), lambda l, j: (0, 0)),
            pl.BlockSpec((None, d, tn), lambda l, j: (l, 0, j)),
            pl.BlockSpec((None, 1, tn), lambda l, j: (l, 0, j)),
        ],
        out_specs=pl.BlockSpec((None, b, tn), lambda l, j: (l, 0, j)),
        out_shape=jax.ShapeDtypeStruct((depth, b, n), F32),
        compiler_params=_params("parallel", "parallel"),
        name="adaln_mod",
    )(c, mod_w, mod_b.reshape(depth, 1, n))


def _row_spec(d, tiles_per_seq, k):
    return pl.BlockSpec((None, None, 1, d), lambda i: (i // tiles_per_seq, k, 0, 0))


def _inproj_kernel(x_ref, g_ref, sc_ref, sh_ref, w_ref, o_ref):
    h = _ln_mod(x_ref[...], g_ref[...], sc_ref[...], sh_ref[...]).astype(BF16)
    o_ref[...] = jnp.dot(h, w_ref[...], preferred_element_type=F32)


def _in_projection(x2, g, modl, w_bf, seq, tm=512):
    t, d = x2.shape
    n = w_bf.shape[1]
    tps = seq // tm
    return pl.pallas_call(
        _inproj_kernel,
        grid=(t // tm,),
        in_specs=[
            pl.BlockSpec((tm, d), lambda i: (i, 0)),
            pl.BlockSpec((1, d), lambda i: (0, 0)),
            _row_spec(d, tps, 1),
            _row_spec(d, tps, 0),
            pl.BlockSpec((d, n), lambda i: (0, 0)),
        ],
        out_specs=pl.BlockSpec((tm, n), lambda i: (i, 0)),
        out_shape=jax.ShapeDtypeStruct((t, n), F32),
        compiler_params=_params("parallel"),
        name="in_proj",
    )(x2, g, modl, modl, w_bf)


def _bias_table(rpb):
    heads, nr, nc = rpb.shape
    assert nr == 2 * WIN_ROWS - 1 and nc == 2 * WIN_COLS - 1
    left = GRID_W - WIN_COLS
    w = jnp.pad(rpb, ((0, 0), (0, 0), (left, 2 * GRID_W - left - nc)))
    flat = jnp.tile(w, (1, 1, GRID_W))[..., :GRID_W * (2 * GRID_W - 1)]
    toe = flat.reshape(heads, nr, GRID_W, 2 * GRID_W - 1)[..., GRID_W - 1:]
    qc = np.arange(GRID_W)[:, None]
    kc = np.arange(GRID_W)[None, :]
    cs = np.clip(qc - WIN_COLS // 2, 0, GRID_W - WIN_COLS)
    toe = jnp.where((kc >= cs) & (kc < cs + WIN_COLS), toe, NEG_BIAS)
    place = [[(u, 0) for u in range(ROWS_PER_STEP)],
             [(WIN_ROWS // 2 + u, u) for u in range(ROWS_PER_STEP)],
             [(BAND_ROWS - ROWS_PER_STEP + u, BAND_ROWS - WIN_ROWS) for u in range(ROWS_PER_STEP)]]
    masked = jnp.full((heads, COL_BLOCK, KEY_COLS), NEG_BIAS, F32)
    cases = []
    for per_u in place:
        per_c = []
        for c in range(GRID_W // COL_BLOCK):
            k0 = _key_col0(c)
            toe_c = toe[:, :, c * COL_BLOCK:(c + 1) * COL_BLOCK, k0:k0 + KEY_COLS]
            blocks = []
            for off, lo in per_u:
                first = lo - off + WIN_ROWS - 1
                row = [toe_c[:, first + i - lo] if lo <= i < lo + WIN_ROWS else masked
                       for i in range(BAND_ROWS)]
                blocks.append(jnp.concatenate(row, axis=-1))
            per_c.append(jnp.concatenate(blocks, axis=1))
        cases.append(jnp.stack(per_c, axis=1))
    return jnp.stack(cases, axis=1)


def _key_col0(c):
    return min(max(c * COL_BLOCK - WIN_COLS // 2, 0), GRID_W - KEY_COLS)


def _mixers_kernel(q_ref, k_ref, v_ref, bg_ref, cg_ref, xv_ref, qkg_ref, tbl_ref, cw_ref,
                   attn_ref, conv_ref, qn_s, kn_s, qc_s, kc_s, vc_s, pad_s, *, rows):
    seq = q_ref.shape[0]
    lane = lax.broadcasted_iota(jnp.int32, (1, LANES), 1)
    first = lane < HEAD_DIM

    def head_norm(x, g):
        xx = x * x
        s0 = jnp.sum(jnp.where(first, xx, 0.0), axis=-1, keepdims=True)
        s1 = jnp.sum(jnp.where(first, 0.0, xx), axis=-1, keepdims=True)
        ms = jnp.where(first, s0, s1) * (1.0 / HEAD_DIM)
        return x * lax.rsqrt(ms + RMS_EPS) * g

    qn_s[...] = head_norm(q_ref[...], qkg_ref[0:1, :]) * (1.0 / math.sqrt(HEAD_DIM))
    kn_s[...] = head_norm(k_ref[...], qkg_ref[1:2, :])

    n_cb = GRID_W // COL_BLOCK
    for c in range(n_cb):
        k0 = _key_col0(c)
        for r in range(rows):
            qp = qn_s[r * GRID_W + c * COL_BLOCK:r * GRID_W + (c + 1) * COL_BLOCK, :]
            qc_s[0, c, r * COL_BLOCK:(r + 1) * COL_BLOCK, :] = jnp.where(first, qp, 0.0).astype(BF16)
            qc_s[1, c, r * COL_BLOCK:(r + 1) * COL_BLOCK, :] = jnp.where(first, 0.0, qp).astype(BF16)
            keys = slice(r * GRID_W + k0, r * GRID_W + k0 + KEY_COLS)
            dst = slice(r * KEY_COLS, (r + 1) * KEY_COLS)
            kc_s[c, dst, :] = kn_s[keys, :].astype(BF16)
            vp = v_ref[keys, :]
            vc_s[0, c, dst, :] = jnp.where(first, vp, 1.0).astype(BF16)
            vc_s[1, c, dst, :] = jnp.where(first, 1.0, vp).astype(BF16)

    n_groups = rows // ROWS_PER_STEP
    nq = ROWS_PER_STEP * COL_BLOCK
    for g in range(n_groups):
        b0 = min(max(g * ROWS_PER_STEP - WIN_ROWS // 2, 0), rows - BAND_ROWS)
        case = 0 if g == 0 else (2 if g == n_groups - 1 else 1)
        patch = slice(b0 * KEY_COLS, (b0 + BAND_ROWS) * KEY_COLS)
        blocks = [(c, h) for c in range(n_cb) for h in range(2)]
        s = jnp.concatenate(
            [lax.dot_general(qc_s[h, c, g * nq:(g + 1) * nq, :], kc_s[c, patch, :], _NT,
                             preferred_element_type=F32) + tbl_ref[h, case, c] for c, h in blocks], axis=0)
        p = jnp.exp(s - jnp.max(s, axis=-1, keepdims=True)).astype(BF16)
        pv = jnp.concatenate(
            [jnp.dot(p[i * nq:(i + 1) * nq, :], vc_s[h, c, patch, :], preferred_element_type=F32)
             for i, (c, h) in enumerate(blocks)], axis=0)
        o = pv / pltpu.roll(pv, HEAD_DIM, axis=1)
        for c in range(n_cb):
            out = jnp.where(first, o[2 * c * nq:(2 * c + 1) * nq, :], o[(2 * c + 1) * nq:(2 * c + 2) * nq, :])
            for u in range(ROWS_PER_STEP):
                row0 = (g * ROWS_PER_STEP + u) * GRID_W + c * COL_BLOCK
                attn_ref[row0:row0 + COL_BLOCK, :] = out[u * COL_BLOCK:(u + 1) * COL_BLOCK, :].astype(BF16)

    zeros = jnp.zeros((8, LANES), F32)
    pad_s[0:8, :] = zeros
    pad_s[seq + 8:seq + 16, :] = zeros
    pad_s[8:seq + 8, :] = cg_ref[...] * xv_ref[...]
    z = (pad_s[7:seq + 7, :] * cw_ref[0:1, :] + pad_s[8:seq + 8, :] * cw_ref[1:2, :]
         + pad_s[9:seq + 9, :] * cw_ref[2:3, :])
    conv_ref[...] = (bg_ref[...] * z).astype(BF16)


def _mixers(proj3, qk_g, rpb, conv_w):
    b, seq, n = proj3.shape
    width = n // 6
    nblk = width // LANES
    heads = width // HEAD_DIM
    rows = seq // GRID_W
    assert rows >= BAND_ROWS + ROWS_PER_STEP and rows % ROWS_PER_STEP == 0 and rpb.shape[0] == heads
    assert LANES == 2 * HEAD_DIM and GRID_W % COL_BLOCK == 0
    tbl = _bias_table(rpb)
    qkg = jnp.tile(qk_g, (1, LANES // HEAD_DIM))
    n_cb = GRID_W // COL_BLOCK

    def col(k):
        return pl.BlockSpec((None, seq, LANES), lambda i, j, k=k: (i, 0, k * nblk + j))

    out_spec = pl.BlockSpec((None, seq, LANES), lambda i, j: (i, 0, j))
    hp = LANES // HEAD_DIM
    return pl.pallas_call(
        functools.partial(_mixers_kernel, rows=rows),
        grid=(b, nblk),
        in_specs=[col(0), col(1), col(2), col(3), col(4), col(5),
                  pl.BlockSpec((2, LANES), lambda i, j: (0, 0)),
                  pl.BlockSpec((hp,) + tbl.shape[1:], lambda i, j: (j, 0, 0, 0, 0)),
                  pl.BlockSpec((conv_w.shape[0], LANES), lambda i, j: (0, j))],
        out_specs=[out_spec, out_spec],
        out_shape=[jax.ShapeDtypeStruct((b, seq, width), BF16)] * 2,
        scratch_shapes=[pltpu.VMEM((seq, LANES), F32), pltpu.VMEM((seq, LANES), F32),
                        pltpu.VMEM((hp, n_cb, rows * COL_BLOCK, LANES), BF16),
                        pltpu.VMEM((n_cb, rows * KEY_COLS, LANES), BF16),
                        pltpu.VMEM((hp, n_cb, rows * KEY_COLS, LANES), BF16),
                        pltpu.VMEM((seq + 16, LANES), F32)],
        compiler_params=_params("parallel", "parallel"),
        name="attn_conv",
    )(proj3, proj3, proj3, proj3, proj3, proj3, qkg, tbl, conv_w)


def _route(x, g_ref, sc_ref, sh_ref, rw_ref, h_ref, aff_ref):
    h = _ln_mod(x, g_ref[...], sc_ref[...], sh_ref[...])
    tm, d = h.shape
    nch = d // LANES
    n_exp = aff_ref.shape[0]
    for j in range(nch):
        h_ref[pl.ds(j, tm, stride=nch), :] = h[:, j * LANES:(j + 1) * LANES]
    h0 = h.astype(BF16)
    h1 = (h - h0.astype(F32)).astype(BF16)
    lg = jnp.dot(h0, rw_ref[...], preferred_element_type=F32)
    lg = (lg[:, :LANES] + lg[:, LANES:]) + jnp.dot(h1, rw_ref[:, :LANES], preferred_element_type=F32)
    lane = lax.broadcasted_iota(jnp.int32, (1, LANES), 1)
    lg = jnp.where(lane < n_exp, lg, NEG_BIAS)
    p = jnp.exp(lg - jnp.max(lg, axis=-1, keepdims=True))
    aff = p / jnp.sum(p, axis=-1, keepdims=True)
    aff_ref[...] = aff.T[:n_exp, :]


def _router_weights(router_w):
    w = jnp.pad(router_w, ((0, 0), (0, LANES - router_w.shape[1])))
    w0 = w.astype(BF16)
    w1 = (w - w0.astype(F32)).astype(BF16)
    return jnp.concatenate([w0, w1], axis=1)


def _route_specs(b, seq, d, e, tm):
    nch = d // LANES
    in_specs = [pl.BlockSpec((1, d), lambda i, j: (0, 0)),
                pl.BlockSpec((None, None, 1, d), lambda i, j: (i, 4, 0, 0)),
                pl.BlockSpec((None, None, 1, d), lambda i, j: (i, 3, 0, 0)),
                pl.BlockSpec((d, 2 * LANES), lambda i, j: (0, 0))]
    out_specs = [pl.BlockSpec((None, tm * nch, LANES), lambda i, j: (i, j, 0)),
                 pl.BlockSpec((None, e, tm), lambda i, j: (i, 0, j))]
    out_shape = [jax.ShapeDtypeStruct((b, seq * nch, LANES), F32),
                 jax.ShapeDtypeStruct((b, e, seq), F32)]
    return in_specs, out_specs, out_shape


def _outproj_kernel(a_ref, c_ref, wa_ref, wc_ref, x_ref, gate_ref, g2_ref, sc2_ref, sh2_ref, rw_ref,
                    o_ref, h_ref, aff_ref):
    m = jnp.dot(a_ref[...], wa_ref[...], preferred_element_type=F32)
    m = m + jnp.dot(c_ref[...], wc_ref[...], preferred_element_type=F32)
    x = x_ref[...] + gate_ref[...] * m
    o_ref[...] = x
    _route(x, g2_ref, sc2_ref, sh2_ref, rw_ref, h_ref, aff_ref)


def _out_projection(attn3, conv3, w_bf, x3, modl, g2, rw01, n_exp, tm=512):
    b, seq, d = x3.shape
    wa = attn3.shape[2]
    wc = conv3.shape[2]
    r_in, r_out, r_shape = _route_specs(b, seq, d, n_exp, tm)
    tile = pl.BlockSpec((None, tm, d), lambda i, j: (i, j, 0))
    return pl.pallas_call(
        _outproj_kernel,
        grid=(b, seq // tm),
        in_specs=[
            pl.BlockSpec((None, tm, wa), lambda i, j: (i, j, 0)),
            pl.BlockSpec((None, tm, wc), lambda i, j: (i, j, 0)),
            pl.BlockSpec((wa, d), lambda i, j: (0, 0)),
            pl.BlockSpec((wc, d), lambda i, j: (1, 0)),
            tile,
            pl.BlockSpec((None, None, 1, d), lambda i, j: (i, 2, 0, 0)),
        ] + r_in,
        out_specs=[tile] + r_out,
        out_shape=[jax.ShapeDtypeStruct((b, seq, d), F32)] + r_shape,
        compiler_params=_params("parallel", "parallel"),
        name="out_proj",
    )(attn3, conv3, w_bf, w_bf, x3, modl, g2, modl, modl, rw01)


def _dft_tables(n):
    jk = (np.arange(n)[:, None] * np.arange(n)[None, :]) % n
    ang = 2.0 * np.pi * jk.astype(np.float64) / n
    return np.cos(ang).astype(np.float32), np.sin(ang).astype(np.float32)


def _fnet_chan_kernel(x_ref, g_ref, sc_ref, sh_ref, cs_ref, a_ref, b_ref, *, gw):
    h = _ln_mod(x_ref[...], g_ref[...], sc_ref[...], sh_ref[...]).astype(BF16)
    for g in range(h.shape[1] // gw):
        ab = jnp.dot(h[:, g * gw:(g + 1) * gw], cs_ref[...], preferred_element_type=F32)
        a_ref[:, g * gw:(g + 1) * gw] = ab[:, :gw].astype(BF16)
        b_ref[:, g * gw:(g + 1) * gw] = ab[:, gw:].astype(BF16)


def _fnet_seq_kernel(cs_ref, ss_ref, a_ref, b_ref, w_ref, x_ref, gate_ref, g2_ref, sc2_ref, sh2_ref,
                     rw_ref, o_ref, h_ref, aff_ref, *, norm):
    f = jnp.dot(cs_ref[...], a_ref[...], preferred_element_type=F32)
    f = f - jnp.dot(ss_ref[...], b_ref[...], preferred_element_type=F32)
    m = jnp.dot((f * norm).astype(BF16), w_ref[...], preferred_element_type=F32)
    x = x_ref[...] + gate_ref[...] * m
    o_ref[...] = x
    _route(x, g2_ref, sc2_ref, sh2_ref, rw_ref, h_ref, aff_ref)


def _fnet_layer(x3, g, modl, w_bf, g2, rw01, n_exp, tm=512):
    b, seq, d = x3.shape
    gw = d // FOURIER_GROUPS
    cc, sc = _dft_tables(gw)
    cs_chan = jnp.concatenate([jnp.asarray(cc), jnp.asarray(sc)], axis=1).astype(BF16)
    cseq, sseq = _dft_tables(seq)
    cseq = jnp.asarray(cseq).astype(BF16)
    sseq = jnp.asarray(sseq).astype(BF16)
    x2 = x3.reshape(b * seq, d)
    tps = seq // tm
    a, bm = pl.pallas_call(
        functools.partial(_fnet_chan_kernel, gw=gw),
        grid=(b * seq // tm,),
        in_specs=[
            pl.BlockSpec((tm, d), lambda i: (i, 0)),
            pl.BlockSpec((1, d), lambda i: (0, 0)),
            _row_spec(d, tps, 1),
            _row_spec(d, tps, 0),
            pl.BlockSpec((gw, 2 * gw), lambda i: (0, 0)),
        ],
        out_specs=[pl.BlockSpec((tm, d), lambda i: (i, 0))] * 2,
        out_shape=[jax.ShapeDtypeStruct((b * seq, d), BF16)] * 2,
        compiler_params=_params("parallel"),
        name="fnet_chan",
    )(x2, g, modl, modl, cs_chan)
    a3 = a.reshape(b, seq, d)
    b3 = bm.reshape(b, seq, d)
    norm = 1.0 / math.sqrt(seq * gw)
    r_in, r_out, r_shape = _route_specs(b, seq, d, n_exp, tm)
    tile = pl.BlockSpec((None, tm, d), lambda i, j: (i, j, 0))
    return pl.pallas_call(
        functools.partial(_fnet_seq_kernel, norm=norm),
        grid=(b, seq // tm),
        in_specs=[
            pl.BlockSpec((tm, seq), lambda i, j: (j, 0)),
            pl.BlockSpec((tm, seq), lambda i, j: (j, 0)),
            pl.BlockSpec((None, seq, d), lambda i, j: (i, 0, 0)),
            pl.BlockSpec((None, seq, d), lambda i, j: (i, 0, 0)),
            pl.BlockSpec((d, d), lambda i, j: (0, 0)),
            tile,
            pl.BlockSpec((None, None, 1, d), lambda i, j: (i, 2, 0, 0)),
        ] + r_in,
        out_specs=[tile] + r_out,
        out_shape=[jax.ShapeDtypeStruct((b, seq, d), F32)] + r_shape,
        compiler_params=_params("parallel", "parallel"),
        name="fnet_seq",
    )(cseq, sseq, a3, b3, w_bf, x3, modl, g2, modl, modl, rw01)


def _topk_kernel(aff_ref, tok_ref, gs_ref, tri_s, *, cap):
    n_exp, seq = aff_ref.shape

    @pl.when(pl.program_id(0) == 0)
    def _():
        r = lax.broadcasted_iota(jnp.int32, (seq, seq), 0)
        c = lax.broadcasted_iota(jnp.int32, (seq, seq), 1)
        tri_s[...] = jnp.where(r < c, 1.0, 0.0).astype(BF16)

    a = aff_ref[...]

    def search(i, t):
        cand = t | lax.shift_left(jnp.int32(1), 30 - i)
        cnt = jnp.sum((a >= pltpu.bitcast(cand, F32)).astype(jnp.int32), axis=1, keepdims=True)
        return jnp.where(cnt >= cap, cand, t)

    thr = lax.fori_loop(0, 31, search, jnp.zeros((a.shape[0], 1), jnp.int32))
    thr = pltpu.bitcast(thr, F32)
    gt = a > thr
    eq = a == thr
    need = cap - jnp.sum(gt.astype(jnp.int32), axis=1, keepdims=True)
    tri = tri_s[...]
    eq_before = jnp.dot(jnp.where(eq, 1.0, 0.0).astype(BF16), tri, preferred_element_type=F32)
    sel = gt | (eq & (eq_before < need.astype(F32)))
    before = jnp.dot(jnp.where(sel, 1.0, 0.0).astype(BF16), tri, preferred_element_type=F32)
    rank = jnp.where(sel, before.astype(jnp.int32), -1)

    digit = lax.broadcasted_iota(jnp.int32, (SLOT_RADIX, seq), 0)
    tok_ids = lax.broadcasted_iota(jnp.int32, (1, seq), 1)
    t_hi = lax.shift_right_logical(tok_ids, 6).astype(F32)
    t_lo = (tok_ids & 63).astype(F32)
    for e in range(n_exp):
        r = rank[e:e + 1, :]
        hi = jnp.where(digit == lax.shift_right_arithmetic(r, SLOT_SHIFT), 1.0, 0.0).astype(BF16)
        lo = digit == (r & (SLOT_RADIX - 1))
        g = a[e:e + 1, :]
        g0 = g.astype(BF16).astype(F32)
        g1 = (g - g0).astype(BF16).astype(F32)
        g2 = (g - g0) - g1
        vals = jnp.concatenate([jnp.where(lo, v, 0.0) for v in (t_hi, t_lo, g0, g1, g2)], axis=0)
        res = lax.dot_general(hi, vals.astype(BF16), _NT, preferred_element_type=F32)
        part = [res[:, k * SLOT_RADIX:(k + 1) * SLOT_RADIX] for k in range(5)]
        tok_ref[e] = (part[0] * 64.0 + part[1]).astype(jnp.int32)
        gs_ref[e] = (part[2] + part[3]) + part[4]


def _topk(aff, cap):
    b, e, seq = aff.shape
    n_hi = cap // SLOT_RADIX
    assert cap % SLOT_RADIX == 0 and n_hi <= SLOT_RADIX and seq <= 64 * 64
    out_spec = pl.BlockSpec((None, e, SLOT_RADIX, SLOT_RADIX), lambda i: (i, 0, 0, 0))
    tok, gs = pl.pallas_call(
        functools.partial(_topk_kernel, cap=cap),
        grid=(b,),
        in_specs=[pl.BlockSpec((None, e, seq), lambda i: (i, 0, 0))],
        out_specs=[out_spec, out_spec],
        out_shape=[jax.ShapeDtypeStruct((b, e, SLOT_RADIX, SLOT_RADIX), jnp.int32),
                   jax.ShapeDtypeStruct((b, e, SLOT_RADIX, SLOT_RADIX), F32)],
        scratch_shapes=[pltpu.VMEM((seq, seq), BF16)],
        compiler_params=_params("arbitrary"),
        name="topk",
    )(aff)
    return tok[:, :, :n_hi].reshape(b, e, cap), gs[:, :, :n_hi].reshape(b, e, cap)


def _gather_kernel(tok_ref, h_ref, o_ref, x_s, *, stride):
    n_exp, cap = tok_ref.shape
    nch = x_s.shape[0] // stride

    def expert(e, carry):
        for i in range(cap):
            t = pl.multiple_of(tok_ref[e, i] * nch, nch)
            x_s[pl.ds(i, nch, stride=stride), :] = h_ref[pl.ds(t, nch), :]
        xe = jnp.concatenate([x_s[j * stride:j * stride + cap, :] for j in range(nch)], axis=-1)
        o_ref[e] = xe.astype(BF16)
        return carry

    lax.fori_loop(0, n_exp, expert, 0)


def _gather_tokens(h_slab, tok, d, stride):
    b, rows, _ = h_slab.shape
    _, e, cap = tok.shape
    nch = d // LANES
    return pl.pallas_call(
        functools.partial(_gather_kernel, stride=stride),
        grid=(b,),
        in_specs=[
            pl.BlockSpec((None, e, cap), lambda i: (i, 0, 0), memory_space=pltpu.SMEM),
            pl.BlockSpec((None, rows, LANES), lambda i: (i, 0, 0)),
        ],
        out_specs=pl.BlockSpec((e, None, cap, d), lambda i: (0, i, 0, 0)),
        out_shape=jax.ShapeDtypeStruct((e, b, cap, d), BF16),
        scratch_shapes=[pltpu.VMEM((nch * stride, LANES), F32)],
        compiler_params=_params("parallel"),
        name="moe_gather",
    )(tok, h_slab)


def _ffn_kernel(x_ref, wg_ref, wu_ref, wd_ref, y_ref, wg_s, wu_s, wd_s, *, cap, ff_chunk):
    @pl.when(pl.program_id(1) == 0)
    def _():
        wg_s[...] = wg_ref[...].astype(BF16)
        wu_s[...] = wu_ref[...].astype(BF16)
        wd_s[...] = wd_ref[...].astype(BF16)

    x = x_ref[...]
    y = None
    for c in range(wg_s.shape[1] // ff_chunk):
        cols = slice(c * ff_chunk, (c + 1) * ff_chunk)
        a = jnp.dot(x, wg_s[:, cols], preferred_element_type=F32)
        u = jnp.dot(x, wu_s[:, cols], preferred_element_type=F32)
        hm = (a * jax.nn.sigmoid(a) * u).astype(BF16)
        yc = jnp.dot(hm, wd_s[cols, :], preferred_element_type=F32)
        y = yc if y is None else y + yc
    bpb, nch, stride, _ = y_ref.shape
    for bb in range(bpb):
        for j in range(nch):
            y_ref[bb, j, 0:cap, :] = y[bb * cap:(bb + 1) * cap, j * LANES:(j + 1) * LANES]
        y_ref[bb, :, cap:, :] = jnp.zeros((nch, stride - cap, LANES), F32)


def _expert_ffn(xe, layer, wg, wu, wd, cap, stride, tm):
    e, m, d = xe.shape
    ff = wg.shape[3]
    nch = d // LANES
    bpb = tm // cap
    return pl.pallas_call(
        functools.partial(_ffn_kernel, cap=cap, ff_chunk=min(512, ff)),
        grid=(e, m // tm),
        in_specs=[
            pl.BlockSpec((None, tm, d), lambda i, j: (i, j, 0)),
            pl.BlockSpec((None, None, d, ff), lambda i, j: (layer, i, 0, 0)),
            pl.BlockSpec((None, None, d, ff), lambda i, j: (layer, i, 0, 0)),
            pl.BlockSpec((None, None, ff, d), lambda i, j: (layer, i, 0, 0)),
        ],
        out_specs=pl.BlockSpec((None, bpb, nch, stride, LANES), lambda i, j: (i, j, 0, 0, 0)),
        out_shape=jax.ShapeDtypeStruct((e, m // cap, nch, stride, LANES), F32),
        scratch_shapes=[pltpu.VMEM((d, ff), BF16), pltpu.VMEM((d, ff), BF16), pltpu.VMEM((ff, d), BF16)],
        compiler_params=_params("parallel", "arbitrary"),
        name="moe_ffn",
    )(xe, wg, wu, wd)


def _scatter_kernel(tok_ref, gs_ref, y_ref, x_ref, gate_ref, o_ref, acc_s, *, stride, n_steps):
    group, _, cap = tok_ref.shape
    ts, d = x_ref.shape
    nch = d // LANES
    k = pl.program_id(1)

    @pl.when(k == 0)
    def _():
        acc_s[...] = jnp.zeros_like(acc_s)

    @pl.when(k < n_steps)
    def _():
        def expert(ee, carry):
            def slab(i):
                return pl.ds(pl.multiple_of(tok_ref[ee, 0, i] * nch, nch), nch)

            for i0 in range(0, cap, SCATTER_BATCH):
                vals = [acc_s[slab(i), :]
                        + y_ref[ee, pl.ds(i, nch, stride=stride), :] * gs_ref[ee, 0, i]
                        for i in range(i0, i0 + SCATTER_BATCH)]
                for i, v in zip(range(i0, i0 + SCATTER_BATCH), vals):
                    acc_s[slab(i), :] = v
            return carry

        lax.fori_loop(0, group, expert, 0)

    @pl.when(k >= n_steps)
    def _():
        base = pl.multiple_of((k - n_steps) * (ts * nch), ts * nch)
        for j in range(nch):
            lanes = slice(j * LANES, (j + 1) * LANES)
            o_ref[:, lanes] = (x_ref[:, lanes]
                               + gate_ref[:, lanes] * acc_s[pl.ds(base + j, ts, stride=nch), :])


def _scatter_residual(y_cm, tok, gs, x3, modl, stride):
    e, b, _, _ = y_cm.shape
    _, seq, d = x3.shape
    cap = tok.shape[-1]
    nch = d // LANES
    assert cap % SCATTER_BATCH == 0
    ts = min(512, seq)
    group = SCATTER_EXPERTS
    assert e % group == 0
    n_steps = e // group
    smem_rows = pl.BlockSpec((None, group, 1, cap), lambda i, k: (i, jnp.minimum(k, n_steps - 1), 0, 0),
                             memory_space=pltpu.SMEM)
    tile = pl.BlockSpec((None, ts, d), lambda i, k: (i, jnp.maximum(k - n_steps, 0), 0))
    return pl.pallas_call(
        functools.partial(_scatter_kernel, stride=stride, n_steps=n_steps),
        grid=(b, n_steps + seq // ts),
        in_specs=[
            smem_rows, smem_rows,
            pl.BlockSpec((group, None, nch * stride, LANES),
                         lambda i, k: (jnp.minimum(k, n_steps - 1), i, 0, 0)),
            tile,
            pl.BlockSpec((None, None, 1, d), lambda i, k: (i, 5, 0, 0)),
        ],
        out_specs=tile,
        out_shape=jax.ShapeDtypeStruct((b, seq, d), F32),
        scratch_shapes=[pltpu.VMEM((seq * nch, LANES), F32)],
        compiler_params=_params("parallel", "arbitrary"),
        name="moe_scatter",
    )(tok.reshape(b, e, 1, cap), gs.reshape(b, e, 1, cap), y_cm, x3, modl)


def _moe_layer(x3, h_slab, aff, modl, layer, wg, wu, wd):
    b, seq, d = x3.shape
    e = aff.shape[1]
    cap = max(1, CAPACITY_FACTOR * seq // e)
    stride = cap + 8
    tok, gs = _topk(aff, cap)
    xe = _gather_tokens(h_slab, tok, d, stride)
    y = _expert_ffn(xe.reshape(e, b * cap, d), layer, wg, wu, wd, cap, stride, tm=min(1024, b * cap))
    return _scatter_residual(y.reshape(e, b, -1, LANES), tok, gs, x3, modl, stride)


def kernel(x, c, mod_w, mod_b, norm_g, mix_w_in, qk_g, rpb, conv_w, mix_w_out, fnet_w_out,
           router_w, exp_w_gate, exp_w_up, exp_w_down):
    b, seq, d = x.shape
    depth = mod_w.shape[0]
    mod = _modulation(c, mod_w, mod_b).reshape(depth, b, N_MOD, 1, d)
    for l in range(depth):
        modl = mod[l]
        g1 = norm_g[l, 0].reshape(1, d)
        g2 = norm_g[l, 1].reshape(1, d)
        rw01 = _router_weights(router_w[l])
        n_exp = router_w.shape[2]
        if l % 2 == 0:
            j = l // 2
            proj = _in_projection(x.reshape(b * seq, d), g1, modl, mix_w_in[j].astype(BF16), seq)
            attn, conv = _mixers(proj.reshape(b, seq, -1), qk_g[j], rpb[j], conv_w[j])
            x, h_slab, aff = _out_projection(attn, conv, mix_w_out[j].astype(BF16), x, modl,
                                             g2, rw01, n_exp)
        else:
            x, h_slab, aff = _fnet_layer(x, g1, modl, fnet_w_out[l // 2].astype(BF16), g2, rw01, n_exp)
        x = _moe_layer(x, h_slab, aff, modl, l, exp_w_gate, exp_w_up, exp_w_down)
    return x
```

```python
import functools
import math

import numpy as np
import jax
import jax.numpy as jnp
from jax import lax
from jax.experimental import pallas as pl
from jax.experimental.pallas import tpu as pltpu

F32 = jnp.float32
BF16 = jnp.bfloat16
HIGHEST = lax.Precision.HIGHEST

GRID_W = 64
WIN_ROWS = 8
WIN_COLS = 16
HEAD_DIM = 64
FOURIER_GROUPS = 4
N_MOD = 6
CAPACITY_FACTOR = 2
RMS_EPS = 1e-6
NEG_BIAS = -1e30
LANES = 128
ROWS_PER_STEP = WIN_ROWS // 2
BAND_ROWS = WIN_ROWS + ROWS_PER_STEP
COL_BLOCK = WIN_COLS
KEY_COLS = 2 * WIN_COLS
SCATTER_BATCH = 8
SCATTER_EXPERTS = 4
SLOT_SHIFT = 4
SLOT_RADIX = 1 << SLOT_SHIFT
VMEM_LIMIT = 56 * 1024 * 1024

_NT = (((1,), (1,)), ((), ()))
_TN = (((0,), (0,)), ((), ()))


def _params(*sem):
    return pltpu.CompilerParams(dimension_semantics=sem, vmem_limit_bytes=VMEM_LIMIT)


def _ln_mod(x, g, sc, sh):
    y = x * lax.rsqrt(jnp.mean(x * x, axis=-1, keepdims=True) + RMS_EPS)
    return (y * g) * (1.0 + sc) + sh


def _mod_kernel(c_ref, w_ref, b_ref, o_ref):
    c = c_ref[...]
    s = c * jax.nn.sigmoid(c)
    o_ref[...] = jnp.dot(s, w_ref[...], precision=HIGHEST, preferred_element_type=F32) + b_ref[...]


def _modulation(c, mod_w, mod_b):
    depth, d, n = mod_w.shape
    b = c.shape[0]
    tn = 1536
    return pl.pallas_call(
        _mod_kernel,
        grid=(depth, n // tn),
        in_specs=[
            pl.BlockSpec((b, d), lambda l, j: (0, 0)),
            pl.BlockSpec((None, d, tn), lambda l, j: (l, 0, j)),
            pl.BlockSpec((None, 1, tn), lambda l, j: (l, 0, j)),
        ],
        out_specs=pl.BlockSpec((None, b, tn), lambda l, j: (l, 0, j)),
        out_shape=jax.ShapeDtypeStruct((depth, b, n), F32),
        compiler_params=_params("parallel", "parallel"),
        name="adaln_mod",
    )(c, mod_w, mod_b.reshape(depth, 1, n))


def _row_spec(d, tiles_per_seq, k):
    return pl.BlockSpec((None, None, 1, d), lambda i: (i // tiles_per_seq, k, 0, 0))


def _inproj_kernel(x_ref, g_ref, sc_ref, sh_ref, w_ref, o_ref):
    h = _ln_mod(x_ref[...], g_ref[...], sc_ref[...], sh_ref[...]).astype(BF16)
    o_ref[...] = jnp.dot(h, w_ref[...], preferred_element_type=F32)


def _in_projection(x2, g, modl, w_bf, seq, tm=512):
    t, d = x2.shape
    n = w_bf.shape[1]
    tps = seq // tm
    return pl.pallas_call(
        _inproj_kernel,
        grid=(t // tm,),
        in_specs=[
            pl.BlockSpec((tm, d), lambda i: (i, 0)),
            pl.BlockSpec((1, d), lambda i: (0, 0)),
            _row_spec(d, tps, 1),
            _row_spec(d, tps, 0),
            pl.BlockSpec((d, n), lambda i: (0, 0)),
        ],
        out_specs=pl.BlockSpec((tm, n), lambda i: (i, 0)),
        out_shape=jax.ShapeDtypeStruct((t, n), F32),
        compiler_params=_params("parallel"),
        name="in_proj",
    )(x2, g, modl, modl, w_bf)


def _bias_table(rpb):
    heads, nr, nc = rpb.shape
    assert nr == 2 * WIN_ROWS - 1 and nc == 2 * WIN_COLS - 1
    n_cb = GRID_W // COL_BLOCK
    qcol = (np.arange(n_cb) * COL_BLOCK)[:, None, None] + np.arange(COL_BLOCK)[None, :, None]
    kcol = np.array([_key_col0(c) for c in range(n_cb)])[:, None, None] + np.arange(KEY_COLS)[None, None, :]
    cs = np.clip(qcol - WIN_COLS // 2, 0, GRID_W - WIN_COLS)
    col_ok = (kcol >= cs) & (kcol < cs + WIN_COLS)
    col_sel = col_ok[..., None] & (np.arange(nc) == (kcol - qcol + WIN_COLS - 1)[..., None])
    place = [[(u, 0) for u in range(ROWS_PER_STEP)],
             [(WIN_ROWS // 2 + u, u) for u in range(ROWS_PER_STEP)],
             [(BAND_ROWS - ROWS_PER_STEP + u, BAND_ROWS - WIN_ROWS) for u in range(ROWS_PER_STEP)]]
    row_sel = np.zeros((len(place), ROWS_PER_STEP, BAND_ROWS, nr), bool)
    for x, per_u in enumerate(place):
        for u, (off, lo) in enumerate(per_u):
            for i in range(lo, lo + WIN_ROWS):
                row_sel[x, u, i, i - off + WIN_ROWS - 1] = True
    row_ok = row_sel.any(axis=-1)
    tbl = jnp.einsum('xuia,hab,cqkb->hxcuqik', row_sel.astype(np.float32), rpb,
                     col_sel.astype(np.float32), precision=HIGHEST)
    ok = row_ok[:, None, :, None, :, None] & col_ok[None, :, None, :, None, :]
    tbl = jnp.where(ok, tbl, NEG_BIAS)
    return tbl.reshape(heads, len(place), n_cb, ROWS_PER_STEP * COL_BLOCK, BAND_ROWS * KEY_COLS)


def _key_col0(c):
    return min(max(c * COL_BLOCK - WIN_COLS // 2, 0), GRID_W - KEY_COLS)


def _mixers_kernel(q_ref, k_ref, v_ref, bg_ref, cg_ref, xv_ref, qkg_ref, tbl_ref, cw_ref,
                   attn_ref, conv_ref, qn_s, kn_s, qc_s, kc_s, vc_s, pad_s, *, rows):
    seq = q_ref.shape[0]
    lane = lax.broadcasted_iota(jnp.int32, (1, LANES), 1)
    first = lane < HEAD_DIM

    def head_norm(x, g):
        xx = x * x
        s0 = jnp.sum(jnp.where(first, xx, 0.0), axis=-1, keepdims=True)
        s1 = jnp.sum(jnp.where(first, 0.0, xx), axis=-1, keepdims=True)
        ms = jnp.where(first, s0, s1) * (1.0 / HEAD_DIM)
        return x * lax.rsqrt(ms + RMS_EPS) * g

    qn_s[...] = head_norm(q_ref[...], qkg_ref[0:1, :]) * (1.0 / math.sqrt(HEAD_DIM))
    kn_s[...] = head_norm(k_ref[...], qkg_ref[1:2, :])

    n_cb = GRID_W // COL_BLOCK
    for c in range(n_cb):
        k0 = _key_col0(c)
        for r in range(rows):
            qp = qn_s[r * GRID_W + c * COL_BLOCK:r * GRID_W + (c + 1) * COL_BLOCK, :]
            qc_s[0, c, r * COL_BLOCK:(r + 1) * COL_BLOCK, :] = jnp.where(first, qp, 0.0).astype(BF16)
            qc_s[1, c, r * COL_BLOCK:(r + 1) * COL_BLOCK, :] = jnp.where(first, 0.0, qp).astype(BF16)
            keys = slice(r * GRID_W + k0, r * GRID_W + k0 + KEY_COLS)
            dst = slice(r * KEY_COLS, (r + 1) * KEY_COLS)
            kc_s[c, dst, :] = kn_s[keys, :].astype(BF16)
            vp = v_ref[keys, :]
            vc_s[0, c, dst, :] = jnp.where(first, vp, 1.0).astype(BF16)
            vc_s[1, c, dst, :] = jnp.where(first, 1.0, vp).astype(BF16)

    n_groups = rows // ROWS_PER_STEP
    nq = ROWS_PER_STEP * COL_BLOCK
    for g in range(n_groups):
        b0 = min(max(g * ROWS_PER_STEP - WIN_ROWS // 2, 0), rows - BAND_ROWS)
        case = 0 if g == 0 else (2 if g == n_groups - 1 else 1)
        patch = slice(b0 * KEY_COLS, (b0 + BAND_ROWS) * KEY_COLS)
        blocks = [(c, h) for c in range(n_cb) for h in range(2)]
        s = jnp.concatenate(
            [lax.dot_general(qc_s[h, c, g * nq:(g + 1) * nq, :], kc_s[c, patch, :], _NT,
                             preferred_element_type=F32) + tbl_ref[h, case, c] for c, h in blocks], axis=0)
        p = jnp.exp(s - jnp.max(s, axis=-1, keepdims=True)).astype(BF16)
        pv = jnp.concatenate(
            [jnp.dot(p[i * nq:(i + 1) * nq, :], vc_s[h, c, patch, :], preferred_element_type=F32)
             for i, (c, h) in enumerate(blocks)], axis=0)
        o = pv / pltpu.roll(pv, HEAD_DIM, axis=1)
        for c in range(n_cb):
            out = jnp.where(first, o[2 * c * nq:(2 * c + 1) * nq, :], o[(2 * c + 1) * nq:(2 * c + 2) * nq, :])
            for u in range(ROWS_PER_STEP):
                row0 = (g * ROWS_PER_STEP + u) * GRID_W + c * COL_BLOCK
                attn_ref[row0:row0 + COL_BLOCK, :] = out[u * COL_BLOCK:(u + 1) * COL_BLOCK, :].astype(BF16)

    zeros = jnp.zeros((8, LANES), F32)
    pad_s[0:8, :] = zeros
    pad_s[seq + 8:seq + 16, :] = zeros
    pad_s[8:seq + 8, :] = cg_ref[...] * xv_ref[...]
    z = (pad_s[7:seq + 7, :] * cw_ref[0:1, :] + pad_s[8:seq + 8, :] * cw_ref[1:2, :]
         + pad_s[9:seq + 9, :] * cw_ref[2:3, :])
    conv_ref[...] = (bg_ref[...] * z).astype(BF16)


def _mixers(proj3, qk_g, rpb, conv_w):
    b, seq, n = proj3.shape
    width = n // 6
    nblk = width // LANES
    heads = width // HEAD_DIM
    rows = seq // GRID_W
    assert rows >= BAND_ROWS + ROWS_PER_STEP and rows % ROWS_PER_STEP == 0 and rpb.shape[0] == heads
    assert LANES == 2 * HEAD_DIM and GRID_W % COL_BLOCK == 0
    tbl = _bias_table(rpb)
    qkg = jnp.tile(qk_g, (1, LANES // HEAD_DIM))
    n_cb = GRID_W // COL_BLOCK

    def col(k):
        return pl.BlockSpec((None, seq, LANES), lambda i, j, k=k: (i, 0, k * nblk + j))

    out_spec = pl.BlockSpec((None, seq, LANES), lambda i, j: (i, 0, j))
    hp = LANES // HEAD_DIM
    return pl.pallas_call(
        functools.partial(_mixers_kernel, rows=rows),
        grid=(b, nblk),
        in_specs=[col(0), col(1), col(2), col(3), col(4), col(5),
                  pl.BlockSpec((2, LANES), lambda i, j: (0, 0)),
                  pl.BlockSpec((hp,) + tbl.shape[1:], lambda i, j: (j, 0, 0, 0, 0)),
                  pl.BlockSpec((conv_w.shape[0], LANES), lambda i, j: (0, j))],
        out_specs=[out_spec, out_spec],
        out_shape=[jax.ShapeDtypeStruct((b, seq, width), BF16)] * 2,
        scratch_shapes=[pltpu.VMEM((seq, LANES), F32), pltpu.VMEM((seq, LANES), F32),
                        pltpu.VMEM((hp, n_cb, rows * COL_BLOCK, LANES), BF16),
                        pltpu.VMEM((n_cb, rows * KEY_COLS, LANES), BF16),
                        pltpu.VMEM((hp, n_cb, rows * KEY_COLS, LANES), BF16),
                        pltpu.VMEM((seq + 16, LANES), F32)],
        compiler_params=_params("parallel", "parallel"),
        name="attn_conv",
    )(proj3, proj3, proj3, proj3, proj3, proj3, qkg, tbl, conv_w)


def _route(x, g_ref, sc_ref, sh_ref, rw_ref, h_ref, aff_ref):
    h = _ln_mod(x, g_ref[...], sc_ref[...], sh_ref[...])
    tm, d = h.shape
    nch = d // LANES
    n_exp = aff_ref.shape[0]
    for j in range(nch):
        h_ref[pl.ds(j, tm, stride=nch), :] = h[:, j * LANES:(j + 1) * LANES]
    h0 = h.astype(BF16)
    h1 = (h - h0.astype(F32)).astype(BF16)
    lg = jnp.dot(h0, rw_ref[...], preferred_element_type=F32)
    lg = (lg[:, :LANES] + lg[:, LANES:]) + jnp.dot(h1, rw_ref[:, :LANES], preferred_element_type=F32)
    lane = lax.broadcasted_iota(jnp.int32, (1, LANES), 1)
    lg = jnp.where(lane < n_exp, lg, NEG_BIAS)
    p = jnp.exp(lg - jnp.max(lg, axis=-1, keepdims=True))
    aff = p / jnp.sum(p, axis=-1, keepdims=True)
    aff_ref[...] = aff.T[:n_exp, :]


def _router_weights(router_w):
    w = jnp.pad(router_w, ((0, 0), (0, LANES - router_w.shape[1])))
    w0 = w.astype(BF16)
    w1 = (w - w0.astype(F32)).astype(BF16)
    return jnp.concatenate([w0, w1], axis=1)


def _route_specs(b, seq, d, e, tm):
    nch = d // LANES
    in_specs = [pl.BlockSpec((1, d), lambda i, j: (0, 0)),
                pl.BlockSpec((None, None, 1, d), lambda i, j: (i, 4, 0, 0)),
                pl.BlockSpec((None, None, 1, d), lambda i, j: (i, 3, 0, 0)),
                pl.BlockSpec((d, 2 * LANES), lambda i, j: (0, 0))]
    out_specs = [pl.BlockSpec((None, tm * nch, LANES), lambda i, j: (i, j, 0)),
                 pl.BlockSpec((None, e, tm), lambda i, j: (i, 0, j))]
    out_shape = [jax.ShapeDtypeStruct((b, seq * nch, LANES), F32),
                 jax.ShapeDtypeStruct((b, e, seq), F32)]
    return in_specs, out_specs, out_shape


def _outproj_kernel(a_ref, c_ref, wa_ref, wc_ref, x_ref, gate_ref, g2_ref, sc2_ref, sh2_ref, rw_ref,
                    o_ref, h_ref, aff_ref):
    m = jnp.dot(a_ref[...], wa_ref[...], preferred_element_type=F32)
    m = m + jnp.dot(c_ref[...], wc_ref[...], preferred_element_type=F32)
    x = x_ref[...] + gate_ref[...] * m
    o_ref[...] = x
    _route(x, g2_ref, sc2_ref, sh2_ref, rw_ref, h_ref, aff_ref)


def _out_projection(attn3, conv3, w_bf, x3, modl, g2, rw01, n_exp, tm=512):
    b, seq, d = x3.shape
    wa = attn3.shape[2]
    wc = conv3.shape[2]
    r_in, r_out, r_shape = _route_specs(b, seq, d, n_exp, tm)
    tile = pl.BlockSpec((None, tm, d), lambda i, j: (i, j, 0))
    return pl.pallas_call(
        _outproj_kernel,
        grid=(b, seq // tm),
        in_specs=[
            pl.BlockSpec((None, tm, wa), lambda i, j: (i, j, 0)),
            pl.BlockSpec((None, tm, wc), lambda i, j: (i, j, 0)),
            pl.BlockSpec((wa, d), lambda i, j: (0, 0)),
            pl.BlockSpec((wc, d), lambda i, j: (1, 0)),
            tile,
            pl.BlockSpec((None, None, 1, d), lambda i, j: (i, 2, 0, 0)),
        ] + r_in,
        out_specs=[tile] + r_out,
        out_shape=[jax.ShapeDtypeStruct((b, seq, d), F32)] + r_shape,
        compiler_params=_params("parallel", "parallel"),
        name="out_proj",
    )(attn3, conv3, w_bf, w_bf, x3, modl, g2, modl, modl, rw01)


def _dft_tables(n):
    jk = (np.arange(n)[:, None] * np.arange(n)[None, :]) % n
    ang = 2.0 * np.pi * jk.astype(np.float64) / n
    return np.cos(ang).astype(np.float32), np.sin(ang).astype(np.float32)


def _fnet_chan_kernel(x_ref, g_ref, sc_ref, sh_ref, cs_ref, a_ref, b_ref, *, gw):
    h = _ln_mod(x_ref[...], g_ref[...], sc_ref[...], sh_ref[...]).astype(BF16)
    for g in range(h.shape[1] // gw):
        ab = jnp.dot(h[:, g * gw:(g + 1) * gw], cs_ref[...], preferred_element_type=F32)
        a_ref[:, g * gw:(g + 1) * gw] = ab[:, :gw].astype(BF16)
        b_ref[:, g * gw:(g + 1) * gw] = ab[:, gw:].astype(BF16)


def _fnet_seq_kernel(cs_ref, ss_ref, a_ref, b_ref, w_ref, x_ref, gate_ref, g2_ref, sc2_ref, sh2_ref,
                     rw_ref, o_ref, h_ref, aff_ref, *, norm):
    f = jnp.dot(cs_ref[...], a_ref[...], preferred_element_type=F32)
    f = f - jnp.dot(ss_ref[...], b_ref[...], preferred_element_type=F32)
    m = jnp.dot((f * norm).astype(BF16), w_ref[...], preferred_element_type=F32)
    x = x_ref[...] + gate_ref[...] * m
    o_ref[...] = x
    _route(x, g2_ref, sc2_ref, sh2_ref, rw_ref, h_ref, aff_ref)


def _fnet_layer(x3, g, modl, w_bf, g2, rw01, n_exp, tm=512):
    b, seq, d = x3.shape
    gw = d // FOURIER_GROUPS
    cc, sc = _dft_tables(gw)
    cs_chan = jnp.concatenate([jnp.asarray(cc), jnp.asarray(sc)], axis=1).astype(BF16)
    cseq, sseq = _dft_tables(seq)
    cseq = jnp.asarray(cseq).astype(BF16)
    sseq = jnp.asarray(sseq).astype(BF16)
    x2 = x3.reshape(b * seq, d)
    tps = seq // tm
    a, bm = pl.pallas_call(
        functools.partial(_fnet_chan_kernel, gw=gw),
        grid=(b * seq // tm,),
        in_specs=[
            pl.BlockSpec((tm, d), lambda i: (i, 0)),
            pl.BlockSpec((1, d), lambda i: (0, 0)),
            _row_spec(d, tps, 1),
            _row_spec(d, tps, 0),
            pl.BlockSpec((gw, 2 * gw), lambda i: (0, 0)),
        ],
        out_specs=[pl.BlockSpec((tm, d), lambda i: (i, 0))] * 2,
        out_shape=[jax.ShapeDtypeStruct((b * seq, d), BF16)] * 2,
        compiler_params=_params("parallel"),
        name="fnet_chan",
    )(x2, g, modl, modl, cs_chan)
    a3 = a.reshape(b, seq, d)
    b3 = bm.reshape(b, seq, d)
    norm = 1.0 / math.sqrt(seq * gw)
    r_in, r_out, r_shape = _route_specs(b, seq, d, n_exp, tm)
    tile = pl.BlockSpec((None, tm, d), lambda i, j: (i, j, 0))
    return pl.pallas_call(
        functools.partial(_fnet_seq_kernel, norm=norm),
        grid=(b, seq // tm),
        in_specs=[
            pl.BlockSpec((tm, seq), lambda i, j: (j, 0)),
            pl.BlockSpec((tm, seq), lambda i, j: (j, 0)),
            pl.BlockSpec((None, seq, d), lambda i, j: (i, 0, 0)),
            pl.BlockSpec((None, seq, d), lambda i, j: (i, 0, 0)),
            pl.BlockSpec((d, d), lambda i, j: (0, 0)),
            tile,
            pl.BlockSpec((None, None, 1, d), lambda i, j: (i, 2, 0, 0)),
        ] + r_in,
        out_specs=[tile] + r_out,
        out_shape=[jax.ShapeDtypeStruct((b, seq, d), F32)] + r_shape,
        compiler_params=_params("parallel", "parallel"),
        name="fnet_seq",
    )(cseq, sseq, a3, b3, w_bf, x3, modl, g2, modl, modl, rw01)


def _topk_kernel(aff_ref, tok_ref, gs_ref, tri_s, *, cap, row_scale):
    n_exp, seq = aff_ref.shape

    @pl.when(pl.program_id(0) == 0)
    def _():
        r = lax.broadcasted_iota(jnp.int32, (seq, seq), 0)
        c = lax.broadcasted_iota(jnp.int32, (seq, seq), 1)
        tri_s[...] = jnp.where(r < c, 1.0, 0.0).astype(BF16)

    a = aff_ref[...]

    def search(i, t):
        cand = t | lax.shift_left(jnp.int32(1), 30 - i)
        cnt = jnp.sum((a >= pltpu.bitcast(cand, F32)).astype(jnp.int32), axis=1, keepdims=True)
        return jnp.where(cnt >= cap, cand, t)

    thr = lax.fori_loop(0, 31, search, jnp.zeros((a.shape[0], 1), jnp.int32))
    thr = pltpu.bitcast(thr, F32)
    gt = a > thr
    eq = a == thr
    need = cap - jnp.sum(gt.astype(jnp.int32), axis=1, keepdims=True)
    tri = tri_s[...]
    eq_before = jnp.dot(jnp.where(eq, 1.0, 0.0).astype(BF16), tri, preferred_element_type=F32)
    sel = gt | (eq & (eq_before < need.astype(F32)))
    before = jnp.dot(jnp.where(sel, 1.0, 0.0).astype(BF16), tri, preferred_element_type=F32)
    rank = jnp.where(sel, before.astype(jnp.int32), -1)

    digit = lax.broadcasted_iota(jnp.int32, (SLOT_RADIX, seq), 0)
    tok_ids = lax.broadcasted_iota(jnp.int32, (1, seq), 1)
    t_hi = lax.shift_right_logical(tok_ids, 6).astype(F32)
    t_lo = (tok_ids & 63).astype(F32)
    for e in range(n_exp):
        r = rank[e:e + 1, :]
        hi = jnp.where(digit == lax.shift_right_arithmetic(r, SLOT_SHIFT), 1.0, 0.0).astype(BF16)
        lo = digit == (r & (SLOT_RADIX - 1))
        g = a[e:e + 1, :]
        g0 = g.astype(BF16).astype(F32)
        g1 = (g - g0).astype(BF16).astype(F32)
        g2 = (g - g0) - g1
        vals = jnp.concatenate([jnp.where(lo, v, 0.0) for v in (t_hi, t_lo, g0, g1, g2)], axis=0)
        res = lax.dot_general(hi, vals.astype(BF16), _NT, preferred_element_type=F32)
        part = [res[:, k * SLOT_RADIX:(k + 1) * SLOT_RADIX] for k in range(5)]
        tok_ref[e] = ((part[0] * 64.0 + part[1]) * float(row_scale)).astype(jnp.int32)
        gs_ref[e] = (part[2] + part[3]) + part[4]


def _topk(aff, cap, row_scale):
    b, e, seq = aff.shape
    n_hi = cap // SLOT_RADIX
    assert cap % SLOT_RADIX == 0 and n_hi <= SLOT_RADIX and seq <= 64 * 64
    out_spec = pl.BlockSpec((None, e, SLOT_RADIX, SLOT_RADIX), lambda i: (i, 0, 0, 0))
    tok, gs = pl.pallas_call(
        functools.partial(_topk_kernel, cap=cap, row_scale=row_scale),
        grid=(b,),
        in_specs=[pl.BlockSpec((None, e, seq), lambda i: (i, 0, 0))],
        out_specs=[out_spec, out_spec],
        out_shape=[jax.ShapeDtypeStruct((b, e, SLOT_RADIX, SLOT_RADIX), jnp.int32),
                   jax.ShapeDtypeStruct((b, e, SLOT_RADIX, SLOT_RADIX), F32)],
        scratch_shapes=[pltpu.VMEM((seq, seq), BF16)],
        compiler_params=_params("arbitrary"),
        name="topk",
    )(aff)
    return tok[:, :, :n_hi].reshape(b, e, cap), gs[:, :, :n_hi].reshape(b, e, cap)


def _gather_kernel(tok_ref, h_ref, o_ref, x_s, *, stride):
    n_exp, cap = tok_ref.shape
    nch = x_s.shape[0] // stride

    def expert(e, carry):
        for i in range(cap):
            t = pl.multiple_of(tok_ref[e, i], nch)
            x_s[pl.ds(i, nch, stride=stride), :] = h_ref[pl.ds(t, nch), :]
        xe = jnp.concatenate([x_s[j * stride:j * stride + cap, :] for j in range(nch)], axis=-1)
        o_ref[e] = xe.astype(BF16)
        return carry

    lax.fori_loop(0, n_exp, expert, 0)


def _gather_tokens(h_slab, tok, d, stride):
    b, rows, _ = h_slab.shape
    _, e, cap = tok.shape
    nch = d // LANES
    return pl.pallas_call(
        functools.partial(_gather_kernel, stride=stride),
        grid=(b,),
        in_specs=[
            pl.BlockSpec((None, e, cap), lambda i: (i, 0, 0), memory_space=pltpu.SMEM),
            pl.BlockSpec((None, rows, LANES), lambda i: (i, 0, 0)),
        ],
        out_specs=pl.BlockSpec((e, None, cap, d), lambda i: (0, i, 0, 0)),
        out_shape=jax.ShapeDtypeStruct((e, b, cap, d), BF16),
        scratch_shapes=[pltpu.VMEM((nch * stride, LANES), F32)],
        compiler_params=_params("parallel"),
        name="moe_gather",
    )(tok, h_slab)


def _ffn_kernel(x_ref, gs_ref, wg_ref, wu_ref, wd_ref, y_ref, wg_s, wu_s, wd_s, *, cap, ff_chunk):
    @pl.when(pl.program_id(1) == 0)
    def _():
        wg_s[...] = wg_ref[...].astype(BF16)
        wu_s[...] = wu_ref[...].astype(BF16)
        wd_s[...] = wd_ref[...].astype(BF16)

    x = x_ref[...]
    y = None
    for c in range(wg_s.shape[1] // ff_chunk):
        cols = slice(c * ff_chunk, (c + 1) * ff_chunk)
        a = jnp.dot(x, wg_s[:, cols], preferred_element_type=F32)
        u = jnp.dot(x, wu_s[:, cols], preferred_element_type=F32)
        hm = (a * jax.nn.sigmoid(a) * u).astype(BF16)
        yc = jnp.dot(hm, wd_s[cols, :], preferred_element_type=F32)
        y = yc if y is None else y + yc
    y = y * gs_ref[...]
    bpb, nch, stride, _ = y_ref.shape
    for bb in range(bpb):
        for j in range(nch):
            y_ref[bb, j, 0:cap, :] = y[bb * cap:(bb + 1) * cap, j * LANES:(j + 1) * LANES]
        y_ref[bb, :, cap:, :] = jnp.zeros((nch, stride - cap, LANES), F32)


def _expert_ffn(xe, gs_col, layer, wg, wu, wd, cap, stride, tm):
    e, m, d = xe.shape
    ff = wg.shape[3]
    nch = d // LANES
    bpb = tm // cap
    return pl.pallas_call(
        functools.partial(_ffn_kernel, cap=cap, ff_chunk=min(512, ff)),
        grid=(e, m // tm),
        in_specs=[
            pl.BlockSpec((None, tm, d), lambda i, j: (i, j, 0)),
            pl.BlockSpec((None, tm, 1), lambda i, j: (i, j, 0)),
            pl.BlockSpec((None, None, d, ff), lambda i, j: (layer, i, 0, 0)),
            pl.BlockSpec((None, None, d, ff), lambda i, j: (layer, i, 0, 0)),
            pl.BlockSpec((None, None, ff, d), lambda i, j: (layer, i, 0, 0)),
        ],
        out_specs=pl.BlockSpec((None, bpb, nch, stride, LANES), lambda i, j: (i, j, 0, 0, 0)),
        out_shape=jax.ShapeDtypeStruct((e, m // cap, nch, stride, LANES), F32),
        scratch_shapes=[pltpu.VMEM((d, ff), BF16), pltpu.VMEM((d, ff), BF16), pltpu.VMEM((ff, d), BF16)],
        compiler_params=_params("parallel", "arbitrary"),
        name="moe_ffn",
    )(xe, gs_col, wg, wu, wd)


def _scatter_kernel(tok_ref, y_ref, x_ref, gate_ref, o_ref, acc_s, *, stride, n_steps):
    group, _, cap = tok_ref.shape
    ts, d = x_ref.shape
    nch = d // LANES
    k = pl.program_id(1)

    @pl.when(k == 0)
    def _():
        acc_s[...] = jnp.zeros_like(acc_s)

    @pl.when(k < n_steps)
    def _():
        def expert(ee, carry):
            def slab(i):
                return pl.ds(pl.multiple_of(tok_ref[ee, 0, i], nch), nch)

            for i0 in range(0, cap, SCATTER_BATCH):
                vals = [acc_s[slab(i), :] + y_ref[ee, pl.ds(i, nch, stride=stride), :]
                        for i in range(i0, i0 + SCATTER_BATCH)]
                for i, v in zip(range(i0, i0 + SCATTER_BATCH), vals):
                    acc_s[slab(i), :] = v
            return carry

        lax.fori_loop(0, group, expert, 0)

    @pl.when(k >= n_steps)
    def _():
        base = pl.multiple_of((k - n_steps) * (ts * nch), ts * nch)
        for j in range(nch):
            lanes = slice(j * LANES, (j + 1) * LANES)
            o_ref[:, lanes] = (x_ref[:, lanes]
                               + gate_ref[:, lanes] * acc_s[pl.ds(base + j, ts, stride=nch), :])


def _scatter_residual(y_cm, tok, x3, modl, stride):
    e, b, _, _ = y_cm.shape
    _, seq, d = x3.shape
    cap = tok.shape[-1]
    nch = d // LANES
    assert cap % SCATTER_BATCH == 0
    ts = min(512, seq)
    group = SCATTER_EXPERTS
    assert e % group == 0
    n_steps = e // group
    smem_rows = pl.BlockSpec((None, group, 1, cap), lambda i, k: (i, jnp.minimum(k, n_steps - 1), 0, 0),
                             memory_space=pltpu.SMEM)
    tile = pl.BlockSpec((None, ts, d), lambda i, k: (i, jnp.maximum(k - n_steps, 0), 0))
    return pl.pallas_call(
        functools.partial(_scatter_kernel, stride=stride, n_steps=n_steps),
        grid=(b, n_steps + seq // ts),
        in_specs=[
            smem_rows,
            pl.BlockSpec((group, None, nch * stride, LANES),
                         lambda i, k: (jnp.minimum(k, n_steps - 1), i, 0, 0)),
            tile,
            pl.BlockSpec((None, None, 1, d), lambda i, k: (i, 5, 0, 0)),
        ],
        out_specs=tile,
        out_shape=jax.ShapeDtypeStruct((b, seq, d), F32),
        scratch_shapes=[pltpu.VMEM((seq * nch, LANES), F32)],
        compiler_params=_params("parallel", "arbitrary"),
        name="moe_scatter",
    )(tok.reshape(b, e, 1, cap), y_cm, x3, modl)


def _moe_layer(x3, h_slab, aff, modl, layer, wg, wu, wd):
    b, seq, d = x3.shape
    e = aff.shape[1]
    cap = max(1, CAPACITY_FACTOR * seq // e)
    stride = cap + 8
    tok, gs = _topk(aff, cap, d // LANES)
    xe = _gather_tokens(h_slab, tok, d, stride)
    gs_col = gs.transpose(1, 0, 2).reshape(e, b * cap, 1)
    y = _expert_ffn(xe.reshape(e, b * cap, d), gs_col, layer, wg, wu, wd, cap, stride,
                    tm=min(1024, b * cap))
    return _scatter_residual(y.reshape(e, b, -1, LANES), tok, x3, modl, stride)


def kernel(x, c, mod_w, mod_b, norm_g, mix_w_in, qk_g, rpb, conv_w, mix_w_out, fnet_w_out,
           router_w, exp_w_gate, exp_w_up, exp_w_down):
    b, seq, d = x.shape
    depth = mod_w.shape[0]
    mod = _modulation(c, mod_w, mod_b).reshape(depth, b, N_MOD, 1, d)
    for l in range(depth):
        modl = mod[l]
        g1 = norm_g[l, 0].reshape(1, d)
        g2 = norm_g[l, 1].reshape(1, d)
        rw01 = _router_weights(router_w[l])
        n_exp = router_w.shape[2]
        if l % 2 == 0:
            j = l // 2
            proj = _in_projection(x.reshape(b * seq, d), g1, modl, mix_w_in[j].astype(BF16), seq)
            attn, conv = _mixers(proj.reshape(b, seq, -1), qk_g[j], rpb[j], conv_w[j])
            x, h_slab, aff = _out_projection(attn, conv, mix_w_out[j].astype(BF16), x, modl,
                                             g2, rw01, n_exp)
        else:
            x, h_slab, aff = _fnet_layer(x, g1, modl, fnet_w_out[l // 2].astype(BF16), g2, rw01, n_exp)
        x = _moe_layer(x, h_slab, aff, modl, l, exp_w_gate, exp_w_up, exp_w_down)
    return x
```

```python
import functools
import math

import numpy as np
import jax
import jax.numpy as jnp
from jax import lax
from jax.experimental import pallas as pl
from jax.experimental.pallas import tpu as pltpu

F32 = jnp.float32
BF16 = jnp.bfloat16
HIGHEST = lax.Precision.HIGHEST

GRID_W = 64
WIN_ROWS = 8
WIN_COLS = 16
HEAD_DIM = 64
FOURIER_GROUPS = 4
N_MOD = 6
CAPACITY_FACTOR = 2
RMS_EPS = 1e-6
NEG_BIAS = -1e30
LANES = 128
ROWS_PER_STEP = WIN_ROWS // 2
BAND_ROWS = WIN_ROWS + ROWS_PER_STEP
COL_BLOCK = WIN_COLS
KEY_COLS = 2 * WIN_COLS
SCATTER_BATCH = 8
SCATTER_EXPERTS = 4
SLOT_SHIFT = 4
SLOT_RADIX = 1 << SLOT_SHIFT
VMEM_LIMIT = 56 * 1024 * 1024

_NT = (((1,), (1,)), ((), ()))
_TN = (((0,), (0,)), ((), ()))


def _params(*sem):
    return pltpu.CompilerParams(dimension_semantics=sem, vmem_limit_bytes=VMEM_LIMIT)


def _ln_mod(x, g, sc, sh):
    y = x * lax.rsqrt(jnp.mean(x * x, axis=-1, keepdims=True) + RMS_EPS)
    return (y * g) * (1.0 + sc) + sh


def _mod_kernel(c_ref, w_ref, b_ref, o_ref):
    c = c_ref[...]
    s = c * jax.nn.sigmoid(c)
    o_ref[...] = jnp.dot(s, w_ref[...], precision=HIGHEST, preferred_element_type=F32) + b_ref[...]


def _modulation(c, mod_w, mod_b):
    depth, d, n = mod_w.shape
    b = c.shape[0]
    tn = 1536
    return pl.pallas_call(
        _mod_kernel,
        grid=(depth, n // tn),
        in_specs=[
            pl.BlockSpec((b, d), lambda l, j: (0, 0)),
            pl.BlockSpec((None, d, tn), lambda l, j: (l, 0, j)),
            pl.BlockSpec((None, 1, tn), lambda l, j: (l, 0, j)),
        ],
        out_specs=pl.BlockSpec((None, b, tn), lambda l, j: (l, 0, j)),
        out_shape=jax.ShapeDtypeStruct((depth, b, n), F32),
        compiler_params=_params("parallel", "parallel"),
        name="adaln_mod",
    )(c, mod_w, mod_b.reshape(depth, 1, n))


def _row_spec(d, tiles_per_seq, k):
    return pl.BlockSpec((None, None, 1, d), lambda i: (i // tiles_per_seq, k, 0, 0))


def _inproj_kernel(x_ref, g_ref, sc_ref, sh_ref, w_ref, o_ref):
    h = _ln_mod(x_ref[...], g_ref[...], sc_ref[...], sh_ref[...]).astype(BF16)
    o_ref[...] = jnp.dot(h, w_ref[...], preferred_element_type=F32)


def _in_projection(x2, g, modl, w_bf, seq, tm=512):
    t, d = x2.shape
    n = w_bf.shape[1]
    tps = seq // tm
    return pl.pallas_call(
        _inproj_kernel,
        grid=(t // tm,),
        in_specs=[
            pl.BlockSpec((tm, d), lambda i: (i, 0)),
            pl.BlockSpec((1, d), lambda i: (0, 0)),
            _row_spec(d, tps, 1),
            _row_spec(d, tps, 0),
            pl.BlockSpec((d, n), lambda i: (0, 0)),
        ],
        out_specs=pl.BlockSpec((tm, n), lambda i: (i, 0)),
        out_shape=jax.ShapeDtypeStruct((t, n), F32),
        compiler_params=_params("parallel"),
        name="in_proj",
    )(x2, g, modl, modl, w_bf)


def _bias_table(rpb):
    heads, nr, nc = rpb.shape
    assert nr == 2 * WIN_ROWS - 1 and nc == 2 * WIN_COLS - 1
    n_cb = GRID_W // COL_BLOCK
    qcol = (np.arange(n_cb) * COL_BLOCK)[:, None, None] + np.arange(COL_BLOCK)[None, :, None]
    kcol = np.array([_key_col0(c) for c in range(n_cb)])[:, None, None] + np.arange(KEY_COLS)[None, None, :]
    cs = np.clip(qcol - WIN_COLS // 2, 0, GRID_W - WIN_COLS)
    col_ok = (kcol >= cs) & (kcol < cs + WIN_COLS)
    col_sel = col_ok[..., None] & (np.arange(nc) == (kcol - qcol + WIN_COLS - 1)[..., None])
    place = [[(u, 0) for u in range(ROWS_PER_STEP)],
             [(WIN_ROWS // 2 + u, u) for u in range(ROWS_PER_STEP)],
             [(BAND_ROWS - ROWS_PER_STEP + u, BAND_ROWS - WIN_ROWS) for u in range(ROWS_PER_STEP)]]
    row_sel = np.zeros((len(place), ROWS_PER_STEP, BAND_ROWS, nr), bool)
    for x, per_u in enumerate(place):
        for u, (off, lo) in enumerate(per_u):
            for i in range(lo, lo + WIN_ROWS):
                row_sel[x, u, i, i - off + WIN_ROWS - 1] = True
    row_ok = row_sel.any(axis=-1)
    n_case, nq, nk = len(place), ROWS_PER_STEP * COL_BLOCK, BAND_ROWS * KEY_COLS
    cols = jnp.einsum('hab,cqkb->hcqak', rpb, col_sel.astype(np.float32), precision=HIGHEST)
    depth = -(-nr * KEY_COLS // LANES) * LANES
    cols = jnp.pad(cols.reshape(heads * n_cb * COL_BLOCK, nr * KEY_COLS), ((0, 0), (0, depth - nr * KEY_COLS)))
    sel = row_sel.transpose(0, 3, 1, 2)[:, :, None, :, :, None] & np.eye(KEY_COLS, dtype=bool)[None, None, :, None, None, :]
    sel = np.pad(sel.reshape(n_case, nr * KEY_COLS, ROWS_PER_STEP * nk), ((0, 0), (0, depth - nr * KEY_COLS), (0, 0)))
    ok = row_ok[:, None, :, None, :, None] & col_ok[None, :, None, :, None, :]
    mask = np.where(ok, 0.0, NEG_BIAS).astype(np.float32).reshape(n_case, n_cb, nq, nk)
    return pl.pallas_call(
        functools.partial(_bias_kernel, n_u=ROWS_PER_STEP),
        grid=(n_case,),
        in_specs=[pl.BlockSpec(cols.shape, lambda x: (0, 0)),
                  pl.BlockSpec((None,) + sel.shape[1:], lambda x: (x, 0, 0)),
                  pl.BlockSpec((None, n_cb, nq, nk), lambda x: (x, 0, 0, 0))],
        out_specs=pl.BlockSpec((heads, None, n_cb, nq, nk), lambda x: (0, x, 0, 0, 0)),
        out_shape=jax.ShapeDtypeStruct((heads, n_case, n_cb, nq, nk), F32),
        compiler_params=_params("parallel"),
        name="bias_table",
    )(cols, jnp.asarray(sel, BF16), jnp.asarray(mask))


def _bias_kernel(cols_ref, sel_ref, mask_ref, o_ref, *, n_u):
    heads, n_cb, nq, nk = o_ref.shape
    nqc = nq // n_u
    t = cols_ref[...]
    t0 = t.astype(BF16)
    r = t - t0.astype(F32)
    t1 = r.astype(BF16)
    t2 = (r - t1.astype(F32)).astype(BF16)
    sel = sel_ref[...]
    res = (jnp.dot(t0, sel, preferred_element_type=F32) + jnp.dot(t1, sel, preferred_element_type=F32)
           + jnp.dot(t2, sel, preferred_element_type=F32))
    for h in range(heads):
        for c in range(n_cb):
            rows = slice((h * n_cb + c) * nqc, (h * n_cb + c + 1) * nqc)
            for u in range(n_u):
                o_ref[h, c, u * nqc:(u + 1) * nqc, :] = (res[rows, u * nk:(u + 1) * nk]
                                                         + mask_ref[c, u * nqc:(u + 1) * nqc, :])


def _key_col0(c):
    return min(max(c * COL_BLOCK - WIN_COLS // 2, 0), GRID_W - KEY_COLS)


def _mixers_kernel(q_ref, k_ref, v_ref, bg_ref, cg_ref, xv_ref, qkg_ref, tbl_ref, cw_ref,
                   attn_ref, conv_ref, qn_s, kn_s, qc_s, kc_s, vc_s, pad_s, *, rows):
    seq = q_ref.shape[0]
    lane = lax.broadcasted_iota(jnp.int32, (1, LANES), 1)
    first = lane < HEAD_DIM

    def head_norm(x, g):
        xx = x * x
        s0 = jnp.sum(jnp.where(first, xx, 0.0), axis=-1, keepdims=True)
        s1 = jnp.sum(jnp.where(first, 0.0, xx), axis=-1, keepdims=True)
        ms = jnp.where(first, s0, s1) * (1.0 / HEAD_DIM)
        return x * lax.rsqrt(ms + RMS_EPS) * g

    qn_s[...] = head_norm(q_ref[...], qkg_ref[0:1, :]) * (1.0 / math.sqrt(HEAD_DIM))
    kn_s[...] = head_norm(k_ref[...], qkg_ref[1:2, :])

    n_cb = GRID_W // COL_BLOCK
    for c in range(n_cb):
        k0 = _key_col0(c)
        for r in range(rows):
            qp = qn_s[r * GRID_W + c * COL_BLOCK:r * GRID_W + (c + 1) * COL_BLOCK, :]
            qc_s[0, c, r * COL_BLOCK:(r + 1) * COL_BLOCK, :] = jnp.where(first, qp, 0.0).astype(BF16)
            qc_s[1, c, r * COL_BLOCK:(r + 1) * COL_BLOCK, :] = jnp.where(first, 0.0, qp).astype(BF16)
            keys = slice(r * GRID_W + k0, r * GRID_W + k0 + KEY_COLS)
            dst = slice(r * KEY_COLS, (r + 1) * KEY_COLS)
            kc_s[c, dst, :] = kn_s[keys, :].astype(BF16)
            vp = v_ref[keys, :]
            vc_s[0, c, dst, :] = jnp.where(first, vp, 1.0).astype(BF16)
            vc_s[1, c, dst, :] = jnp.where(first, 1.0, vp).astype(BF16)

    n_groups = rows // ROWS_PER_STEP
    nq = ROWS_PER_STEP * COL_BLOCK
    for g in range(n_groups):
        b0 = min(max(g * ROWS_PER_STEP - WIN_ROWS // 2, 0), rows - BAND_ROWS)
        case = 0 if g == 0 else (2 if g == n_groups - 1 else 1)
        patch = slice(b0 * KEY_COLS, (b0 + BAND_ROWS) * KEY_COLS)
        blocks = [(c, h) for c in range(n_cb) for h in range(2)]
        s = jnp.concatenate(
            [lax.dot_general(qc_s[h, c, g * nq:(g + 1) * nq, :], kc_s[c, patch, :], _NT,
                             preferred_element_type=F32) + tbl_ref[h, case, c] for c, h in blocks], axis=0)
        p = jnp.exp(s - jnp.max(s, axis=-1, keepdims=True)).astype(BF16)
        pv = jnp.concatenate(
            [jnp.dot(p[i * nq:(i + 1) * nq, :], vc_s[h, c, patch, :], preferred_element_type=F32)
             for i, (c, h) in enumerate(blocks)], axis=0)
        o = pv / pltpu.roll(pv, HEAD_DIM, axis=1)
        for c in range(n_cb):
            out = jnp.where(first, o[2 * c * nq:(2 * c + 1) * nq, :], o[(2 * c + 1) * nq:(2 * c + 2) * nq, :])
            for u in range(ROWS_PER_STEP):
                row0 = (g * ROWS_PER_STEP + u) * GRID_W + c * COL_BLOCK
                attn_ref[row0:row0 + COL_BLOCK, :] = out[u * COL_BLOCK:(u + 1) * COL_BLOCK, :].astype(BF16)

    zeros = jnp.zeros((8, LANES), F32)
    pad_s[0:8, :] = zeros
    pad_s[seq + 8:seq + 16, :] = zeros
    pad_s[8:seq + 8, :] = cg_ref[...] * xv_ref[...]
    z = (pad_s[7:seq + 7, :] * cw_ref[0:1, :] + pad_s[8:seq + 8, :] * cw_ref[1:2, :]
         + pad_s[9:seq + 9, :] * cw_ref[2:3, :])
    conv_ref[...] = (bg_ref[...] * z).astype(BF16)


def _mixers(proj3, qk_g, tbl, layer, conv_w):
    b, seq, n = proj3.shape
    width = n // 6
    nblk = width // LANES
    rows = seq // GRID_W
    assert rows >= BAND_ROWS + ROWS_PER_STEP and rows % ROWS_PER_STEP == 0
    assert LANES == 2 * HEAD_DIM and GRID_W % COL_BLOCK == 0
    qkg = jnp.tile(qk_g, (1, LANES // HEAD_DIM))
    n_cb = GRID_W // COL_BLOCK

    def col(k):
        return pl.BlockSpec((None, seq, LANES), lambda i, j, k=k: (i, 0, k * nblk + j))

    out_spec = pl.BlockSpec((None, seq, LANES), lambda i, j: (i, 0, j))
    hp = LANES // HEAD_DIM
    return pl.pallas_call(
        functools.partial(_mixers_kernel, rows=rows),
        grid=(b, nblk),
        in_specs=[col(0), col(1), col(2), col(3), col(4), col(5),
                  pl.BlockSpec((2, LANES), lambda i, j: (0, 0)),
                  pl.BlockSpec((hp,) + tbl.shape[1:], lambda i, j: (layer * nblk + j, 0, 0, 0, 0)),
                  pl.BlockSpec((conv_w.shape[0], LANES), lambda i, j: (0, j))],
        out_specs=[out_spec, out_spec],
        out_shape=[jax.ShapeDtypeStruct((b, seq, width), BF16)] * 2,
        scratch_shapes=[pltpu.VMEM((seq, LANES), F32), pltpu.VMEM((seq, LANES), F32),
                        pltpu.VMEM((hp, n_cb, rows * COL_BLOCK, LANES), BF16),
                        pltpu.VMEM((n_cb, rows * KEY_COLS, LANES), BF16),
                        pltpu.VMEM((hp, n_cb, rows * KEY_COLS, LANES), BF16),
                        pltpu.VMEM((seq + 16, LANES), F32)],
        compiler_params=_params("parallel", "parallel"),
        name="attn_conv",
    )(proj3, proj3, proj3, proj3, proj3, proj3, qkg, tbl, conv_w)


def _route(x, g_ref, sc_ref, sh_ref, rw_ref, h_ref, aff_ref):
    h = _ln_mod(x, g_ref[...], sc_ref[...], sh_ref[...])
    tm, d = h.shape
    nch = d // LANES
    n_exp = aff_ref.shape[0]
    for j in range(nch):
        h_ref[pl.ds(j, tm, stride=nch), :] = h[:, j * LANES:(j + 1) * LANES]
    h0 = h.astype(BF16)
    h1 = (h - h0.astype(F32)).astype(BF16)
    lg = jnp.dot(h0, rw_ref[...], preferred_element_type=F32)
    lg = (lg[:, :LANES] + lg[:, LANES:]) + jnp.dot(h1, rw_ref[:, :LANES], preferred_element_type=F32)
    lane = lax.broadcasted_iota(jnp.int32, (1, LANES), 1)
    lg = jnp.where(lane < n_exp, lg, NEG_BIAS)
    p = jnp.exp(lg - jnp.max(lg, axis=-1, keepdims=True))
    aff = p / jnp.sum(p, axis=-1, keepdims=True)
    aff_ref[...] = aff.T[:n_exp, :]


def _router_weights(router_w):
    w = jnp.pad(router_w, ((0, 0), (0, LANES - router_w.shape[1])))
    w0 = w.astype(BF16)
    w1 = (w - w0.astype(F32)).astype(BF16)
    return jnp.concatenate([w0, w1], axis=1)


def _route_specs(b, seq, d, e, tm):
    nch = d // LANES
    in_specs = [pl.BlockSpec((1, d), lambda i, j: (0, 0)),
                pl.BlockSpec((None, None, 1, d), lambda i, j: (i, 4, 0, 0)),
                pl.BlockSpec((None, None, 1, d), lambda i, j: (i, 3, 0, 0)),
                pl.BlockSpec((d, 2 * LANES), lambda i, j: (0, 0))]
    out_specs = [pl.BlockSpec((None, tm * nch, LANES), lambda i, j: (i, j, 0)),
                 pl.BlockSpec((None, e, tm), lambda i, j: (i, 0, j))]
    out_shape = [jax.ShapeDtypeStruct((b, seq * nch, LANES), F32),
                 jax.ShapeDtypeStruct((b, e, seq), F32)]
    return in_specs, out_specs, out_shape


def _outproj_kernel(a_ref, c_ref, wa_ref, wc_ref, x_ref, gate_ref, g2_ref, sc2_ref, sh2_ref, rw_ref,
                    o_ref, h_ref, aff_ref):
    m = jnp.dot(a_ref[...], wa_ref[...], preferred_element_type=F32)
    m = m + jnp.dot(c_ref[...], wc_ref[...], preferred_element_type=F32)
    x = x_ref[...] + gate_ref[...] * m
    o_ref[...] = x
    _route(x, g2_ref, sc2_ref, sh2_ref, rw_ref, h_ref, aff_ref)


def _out_projection(attn3, conv3, w_bf, x3, modl, g2, rw01, n_exp, tm=512):
    b, seq, d = x3.shape
    wa = attn3.shape[2]
    wc = conv3.shape[2]
    r_in, r_out, r_shape = _route_specs(b, seq, d, n_exp, tm)
    tile = pl.BlockSpec((None, tm, d), lambda i, j: (i, j, 0))
    return pl.pallas_call(
        _outproj_kernel,
        grid=(b, seq // tm),
        in_specs=[
            pl.BlockSpec((None, tm, wa), lambda i, j: (i, j, 0)),
            pl.BlockSpec((None, tm, wc), lambda i, j: (i, j, 0)),
            pl.BlockSpec((wa, d), lambda i, j: (0, 0)),
            pl.BlockSpec((wc, d), lambda i, j: (1, 0)),
            tile,
            pl.BlockSpec((None, None, 1, d), lambda i, j: (i, 2, 0, 0)),
        ] + r_in,
        out_specs=[tile] + r_out,
        out_shape=[jax.ShapeDtypeStruct((b, seq, d), F32)] + r_shape,
        compiler_params=_params("parallel", "parallel"),
        name="out_proj",
    )(attn3, conv3, w_bf, w_bf, x3, modl, g2, modl, modl, rw01)


def _dft_tables(n):
    jk = (np.arange(n)[:, None] * np.arange(n)[None, :]) % n
    ang = 2.0 * np.pi * jk.astype(np.float64) / n
    return np.cos(ang).astype(np.float32), np.sin(ang).astype(np.float32)


def _fnet_chan_kernel(x_ref, g_ref, sc_ref, sh_ref, cs_ref, a_ref, b_ref, *, gw):
    h = _ln_mod(x_ref[...], g_ref[...], sc_ref[...], sh_ref[...]).astype(BF16)
    for g in range(h.shape[1] // gw):
        ab = jnp.dot(h[:, g * gw:(g + 1) * gw], cs_ref[...], preferred_element_type=F32)
        a_ref[:, g * gw:(g + 1) * gw] = ab[:, :gw].astype(BF16)
        b_ref[:, g * gw:(g + 1) * gw] = ab[:, gw:].astype(BF16)


def _fnet_seq_kernel(cs_ref, ss_ref, a_ref, b_ref, w_ref, x_ref, gate_ref, g2_ref, sc2_ref, sh2_ref,
                     rw_ref, o_ref, h_ref, aff_ref, *, norm):
    f = jnp.dot(cs_ref[...], a_ref[...], preferred_element_type=F32)
    f = f - jnp.dot(ss_ref[...], b_ref[...], preferred_element_type=F32)
    m = jnp.dot((f * norm).astype(BF16), w_ref[...], preferred_element_type=F32)
    x = x_ref[...] + gate_ref[...] * m
    o_ref[...] = x
    _route(x, g2_ref, sc2_ref, sh2_ref, rw_ref, h_ref, aff_ref)


def _fnet_layer(x3, g, modl, w_bf, g2, rw01, n_exp, tm=512):
    b, seq, d = x3.shape
    gw = d // FOURIER_GROUPS
    cc, sc = _dft_tables(gw)
    cs_chan = jnp.concatenate([jnp.asarray(cc), jnp.asarray(sc)], axis=1).astype(BF16)
    cseq, sseq = _dft_tables(seq)
    cseq = jnp.asarray(cseq).astype(BF16)
    sseq = jnp.asarray(sseq).astype(BF16)
    x2 = x3.reshape(b * seq, d)
    tps = seq // tm
    a, bm = pl.pallas_call(
        functools.partial(_fnet_chan_kernel, gw=gw),
        grid=(b * seq // tm,),
        in_specs=[
            pl.BlockSpec((tm, d), lambda i: (i, 0)),
            pl.BlockSpec((1, d), lambda i: (0, 0)),
            _row_spec(d, tps, 1),
            _row_spec(d, tps, 0),
            pl.BlockSpec((gw, 2 * gw), lambda i: (0, 0)),
        ],
        out_specs=[pl.BlockSpec((tm, d), lambda i: (i, 0))] * 2,
        out_shape=[jax.ShapeDtypeStruct((b * seq, d), BF16)] * 2,
        compiler_params=_params("parallel"),
        name="fnet_chan",
    )(x2, g, modl, modl, cs_chan)
    a3 = a.reshape(b, seq, d)
    b3 = bm.reshape(b, seq, d)
    norm = 1.0 / math.sqrt(seq * gw)
    r_in, r_out, r_shape = _route_specs(b, seq, d, n_exp, tm)
    tile = pl.BlockSpec((None, tm, d), lambda i, j: (i, j, 0))
    return pl.pallas_call(
        functools.partial(_fnet_seq_kernel, norm=norm),
        grid=(b, seq // tm),
        in_specs=[
            pl.BlockSpec((tm, seq), lambda i, j: (j, 0)),
            pl.BlockSpec((tm, seq), lambda i, j: (j, 0)),
            pl.BlockSpec((None, seq, d), lambda i, j: (i, 0, 0)),
            pl.BlockSpec((None, seq, d), lambda i, j: (i, 0, 0)),
            pl.BlockSpec((d, d), lambda i, j: (0, 0)),
            tile,
            pl.BlockSpec((None, None, 1, d), lambda i, j: (i, 2, 0, 0)),
        ] + r_in,
        out_specs=[tile] + r_out,
        out_shape=[jax.ShapeDtypeStruct((b, seq, d), F32)] + r_shape,
        compiler_params=_params("parallel", "parallel"),
        name="fnet_seq",
    )(cseq, sseq, a3, b3, w_bf, x3, modl, g2, modl, modl, rw01)


def _topk_kernel(aff_ref, tok_ref, gs_ref, tri_s, *, cap, row_scale):
    n_exp, seq = aff_ref.shape

    @pl.when(pl.program_id(0) == 0)
    def _():
        r = lax.broadcasted_iota(jnp.int32, (seq, seq), 0)
        c = lax.broadcasted_iota(jnp.int32, (seq, seq), 1)
        tri_s[...] = jnp.where(r < c, 1.0, 0.0).astype(BF16)

    a = aff_ref[...]

    def search(i, t):
        cand = t | lax.shift_left(jnp.int32(1), 30 - i)
        cnt = jnp.sum((a >= pltpu.bitcast(cand, F32)).astype(jnp.int32), axis=1, keepdims=True)
        return jnp.where(cnt >= cap, cand, t)

    thr = lax.fori_loop(0, 31, search, jnp.zeros((a.shape[0], 1), jnp.int32))
    thr = pltpu.bitcast(thr, F32)
    gt = a > thr
    eq = a == thr
    need = cap - jnp.sum(gt.astype(jnp.int32), axis=1, keepdims=True)
    tri = tri_s[...]
    eq_before = jnp.dot(jnp.where(eq, 1.0, 0.0).astype(BF16), tri, preferred_element_type=F32)
    sel = gt | (eq & (eq_before < need.astype(F32)))
    before = jnp.dot(jnp.where(sel, 1.0, 0.0).astype(BF16), tri, preferred_element_type=F32)
    rank = jnp.where(sel, before.astype(jnp.int32), -1)

    digit = lax.broadcasted_iota(jnp.int32, (SLOT_RADIX, seq), 0)
    tok_ids = lax.broadcasted_iota(jnp.int32, (1, seq), 1)
    t_hi = lax.shift_right_logical(tok_ids, 6).astype(F32)
    t_lo = (tok_ids & 63).astype(F32)
    for e in range(n_exp):
        r = rank[e:e + 1, :]
        hi = jnp.where(digit == lax.shift_right_arithmetic(r, SLOT_SHIFT), 1.0, 0.0).astype(BF16)
        lo = digit == (r & (SLOT_RADIX - 1))
        g = a[e:e + 1, :]
        g0 = g.astype(BF16).astype(F32)
        g1 = (g - g0).astype(BF16).astype(F32)
        g2 = (g - g0) - g1
        vals = jnp.concatenate([jnp.where(lo, v, 0.0) for v in (t_hi, t_lo, g0, g1, g2)], axis=0)
        res = lax.dot_general(hi, vals.astype(BF16), _NT, preferred_element_type=F32)
        part = [res[:, k * SLOT_RADIX:(k + 1) * SLOT_RADIX] for k in range(5)]
        tok_ref[e] = ((part[0] * 64.0 + part[1]) * float(row_scale)).astype(jnp.int32)
        gs_ref[e] = (part[2] + part[3]) + part[4]


def _topk(aff, cap, row_scale):
    b, e, seq = aff.shape
    n_hi = cap // SLOT_RADIX
    assert cap % SLOT_RADIX == 0 and n_hi <= SLOT_RADIX and seq <= 64 * 64
    out_spec = pl.BlockSpec((None, e, SLOT_RADIX, SLOT_RADIX), lambda i: (i, 0, 0, 0))
    tok, gs = pl.pallas_call(
        functools.partial(_topk_kernel, cap=cap, row_scale=row_scale),
        grid=(b,),
        in_specs=[pl.BlockSpec((None, e, seq), lambda i: (i, 0, 0))],
        out_specs=[out_spec, out_spec],
        out_shape=[jax.ShapeDtypeStruct((b, e, SLOT_RADIX, SLOT_RADIX), jnp.int32),
                   jax.ShapeDtypeStruct((b, e, SLOT_RADIX, SLOT_RADIX), F32)],
        scratch_shapes=[pltpu.VMEM((seq, seq), BF16)],
        compiler_params=_params("arbitrary"),
        name="topk",
    )(aff)
    return tok[:, :, :n_hi].reshape(b, e, cap), gs[:, :, :n_hi].reshape(b, e, cap)


def _gather_kernel(tok_ref, h_ref, o_ref, x_s, *, stride):
    n_exp, cap = tok_ref.shape
    nch = x_s.shape[0] // stride

    def expert(e, carry):
        for i in range(cap):
            t = pl.multiple_of(tok_ref[e, i], nch)
            x_s[pl.ds(i, nch, stride=stride), :] = h_ref[pl.ds(t, nch), :]
        xe = jnp.concatenate([x_s[j * stride:j * stride + cap, :] for j in range(nch)], axis=-1)
        o_ref[e] = xe.astype(BF16)
        return carry

    lax.fori_loop(0, n_exp, expert, 0)


def _gather_tokens(h_slab, tok, d, stride):
    b, rows, _ = h_slab.shape
    _, e, cap = tok.shape
    nch = d // LANES
    return pl.pallas_call(
        functools.partial(_gather_kernel, stride=stride),
        grid=(b,),
        in_specs=[
            pl.BlockSpec((None, e, cap), lambda i: (i, 0, 0), memory_space=pltpu.SMEM),
            pl.BlockSpec((None, rows, LANES), lambda i: (i, 0, 0)),
        ],
        out_specs=pl.BlockSpec((e, None, cap, d), lambda i: (0, i, 0, 0)),
        out_shape=jax.ShapeDtypeStruct((e, b, cap, d), BF16),
        scratch_shapes=[pltpu.VMEM((nch * stride, LANES), F32)],
        compiler_params=_params("parallel"),
        name="moe_gather",
    )(tok, h_slab)


def _ffn_kernel(x_ref, gs_ref, wg_ref, wu_ref, wd_ref, y_ref, wg_s, wu_s, wd_s, *, cap, ff_chunk):
    @pl.when(pl.program_id(1) == 0)
    def _():
        wg_s[...] = wg_ref[...].astype(BF16)
        wu_s[...] = wu_ref[...].astype(BF16)
        wd_s[...] = wd_ref[...].astype(BF16)

    x = x_ref[...]
    y = None
    for c in range(wg_s.shape[1] // ff_chunk):
        cols = slice(c * ff_chunk, (c + 1) * ff_chunk)
        a = jnp.dot(x, wg_s[:, cols], preferred_element_type=F32)
        u = jnp.dot(x, wu_s[:, cols], preferred_element_type=F32)
        hm = (a * jax.nn.sigmoid(a) * u).astype(BF16)
        yc = jnp.dot(hm, wd_s[cols, :], preferred_element_type=F32)
        y = yc if y is None else y + yc
    g = jnp.broadcast_to(gs_ref[...], (LANES, x.shape[0])).T
    bpb, nch, stride, _ = y_ref.shape
    for bb in range(bpb):
        rows = slice(bb * cap, (bb + 1) * cap)
        for j in range(nch):
            y_ref[bb, j, 0:cap, :] = y[rows, j * LANES:(j + 1) * LANES] * g[rows, :]
        y_ref[bb, :, cap:, :] = jnp.zeros((nch, stride - cap, LANES), F32)


def _expert_ffn(xe, gs_col, layer, wg, wu, wd, cap, stride, tm):
    e, m, d = xe.shape
    ff = wg.shape[3]
    nch = d // LANES
    bpb = tm // cap
    return pl.pallas_call(
        functools.partial(_ffn_kernel, cap=cap, ff_chunk=min(512, ff)),
        grid=(e, m // tm),
        in_specs=[
            pl.BlockSpec((None, tm, d), lambda i, j: (i, j, 0)),
            pl.BlockSpec((None, 1, tm), lambda i, j: (i, 0, j)),
            pl.BlockSpec((None, None, d, ff), lambda i, j: (layer, i, 0, 0)),
            pl.BlockSpec((None, None, d, ff), lambda i, j: (layer, i, 0, 0)),
            pl.BlockSpec((None, None, ff, d), lambda i, j: (layer, i, 0, 0)),
        ],
        out_specs=pl.BlockSpec((None, bpb, nch, stride, LANES), lambda i, j: (i, j, 0, 0, 0)),
        out_shape=jax.ShapeDtypeStruct((e, m // cap, nch, stride, LANES), F32),
        scratch_shapes=[pltpu.VMEM((d, ff), BF16), pltpu.VMEM((d, ff), BF16), pltpu.VMEM((ff, d), BF16)],
        compiler_params=_params("parallel", "arbitrary"),
        name="moe_ffn",
    )(xe, gs_col, wg, wu, wd)


def _scatter_kernel(tok_ref, y_ref, x_ref, gate_ref, o_ref, acc_s, *, stride, n_steps):
    group, _, cap = tok_ref.shape
    ts, d = x_ref.shape
    nch = d // LANES
    k = pl.program_id(1)

    @pl.when(k == 0)
    def _():
        acc_s[...] = jnp.zeros_like(acc_s)

    @pl.when(k < n_steps)
    def _():
        def expert(ee, carry):
            def slab(i):
                return pl.ds(pl.multiple_of(tok_ref[ee, 0, i], nch), nch)

            for i0 in range(0, cap, SCATTER_BATCH):
                vals = [acc_s[slab(i), :] + y_ref[ee, pl.ds(i, nch, stride=stride), :]
                        for i in range(i0, i0 + SCATTER_BATCH)]
                for i, v in zip(range(i0, i0 + SCATTER_BATCH), vals):
                    acc_s[slab(i), :] = v
            return carry

        lax.fori_loop(0, group, expert, 0)

    @pl.when(k >= n_steps)
    def _():
        base = pl.multiple_of((k - n_steps) * (ts * nch), ts * nch)
        for j in range(nch):
            lanes = slice(j * LANES, (j + 1) * LANES)
            o_ref[:, lanes] = (x_ref[:, lanes]
                               + gate_ref[:, lanes] * acc_s[pl.ds(base + j, ts, stride=nch), :])


def _scatter_residual(y_cm, tok, x3, modl, stride):
    e, b, _, _ = y_cm.shape
    _, seq, d = x3.shape
    cap = tok.shape[-1]
    nch = d // LANES
    assert cap % SCATTER_BATCH == 0
    ts = min(512, seq)
    group = SCATTER_EXPERTS
    assert e % group == 0
    n_steps = e // group
    smem_rows = pl.BlockSpec((None, group, 1, cap), lambda i, k: (i, jnp.minimum(k, n_steps - 1), 0, 0),
                             memory_space=pltpu.SMEM)
    tile = pl.BlockSpec((None, ts, d), lambda i, k: (i, jnp.maximum(k - n_steps, 0), 0))
    return pl.pallas_call(
        functools.partial(_scatter_kernel, stride=stride, n_steps=n_steps),
        grid=(b, n_steps + seq // ts),
        in_specs=[
            smem_rows,
            pl.BlockSpec((group, None, nch * stride, LANES),
                         lambda i, k: (jnp.minimum(k, n_steps - 1), i, 0, 0)),
            tile,
            pl.BlockSpec((None, None, 1, d), lambda i, k: (i, 5, 0, 0)),
        ],
        out_specs=tile,
        out_shape=jax.ShapeDtypeStruct((b, seq, d), F32),
        scratch_shapes=[pltpu.VMEM((seq * nch, LANES), F32)],
        compiler_params=_params("parallel", "arbitrary"),
        name="moe_scatter",
    )(tok.reshape(b, e, 1, cap), y_cm, x3, modl)


def _moe_layer(x3, h_slab, aff, modl, layer, wg, wu, wd):
    b, seq, d = x3.shape
    e = aff.shape[1]
    cap = max(1, CAPACITY_FACTOR * seq // e)
    stride = cap + 8
    tok, gs = _topk(aff, cap, d // LANES)
    xe = _gather_tokens(h_slab, tok, d, stride)
    gs_col = gs.transpose(1, 0, 2).reshape(e, 1, b * cap)
    y = _expert_ffn(xe.reshape(e, b * cap, d), gs_col, layer, wg, wu, wd, cap, stride,
                    tm=min(1024, b * cap))
    return _scatter_residual(y.reshape(e, b, -1, LANES), tok, x3, modl, stride)


def kernel(x, c, mod_w, mod_b, norm_g, mix_w_in, qk_g, rpb, conv_w, mix_w_out, fnet_w_out,
           router_w, exp_w_gate, exp_w_up, exp_w_down):
    b, seq, d = x.shape
    depth = mod_w.shape[0]
    mod = _modulation(c, mod_w, mod_b).reshape(depth, b, N_MOD, 1, d)
    tbl = _bias_table(rpb.reshape((-1,) + rpb.shape[2:]))
    for l in range(depth):
        modl = mod[l]
        g1 = norm_g[l, 0].reshape(1, d)
        g2 = norm_g[l, 1].reshape(1, d)
        rw01 = _router_weights(router_w[l])
        n_exp = router_w.shape[2]
        if l % 2 == 0:
            j = l // 2
            proj = _in_projection(x.reshape(b * seq, d), g1, modl, mix_w_in[j].astype(BF16), seq)
            attn, conv = _mixers(proj.reshape(b, seq, -1), qk_g[j], tbl, j, conv_w[j])
            x, h_slab, aff = _out_projection(attn, conv, mix_w_out[j].astype(BF16), x, modl,
                                             g2, rw01, n_exp)
        else:
            x, h_slab, aff = _fnet_layer(x, g1, modl, fnet_w_out[l // 2].astype(BF16), g2, rw01, n_exp)
        x = _moe_layer(x, h_slab, aff, modl, l, exp_w_gate, exp_w_up, exp_w_down)
    return x
```

```python
import functools
import math

import numpy as np
import jax
import jax.numpy as jnp
from jax import lax
from jax.experimental import pallas as pl
from jax.experimental.pallas import tpu as pltpu

F32 = jnp.float32
BF16 = jnp.bfloat16
HIGHEST = lax.Precision.HIGHEST

GRID_W = 64
WIN_ROWS = 8
WIN_COLS = 16
HEAD_DIM = 64
FOURIER_GROUPS = 4
N_MOD = 6
CAPACITY_FACTOR = 2
RMS_EPS = 1e-6
NEG_BIAS = -1e30
LANES = 128
ROWS_PER_STEP = WIN_ROWS // 2
BAND_ROWS = WIN_ROWS + ROWS_PER_STEP
COL_BLOCK = WIN_COLS
KEY_COLS = 2 * WIN_COLS
SCATTER_BATCH = 8
SCATTER_EXPERTS = 4
SLOT_SHIFT = 4
SLOT_RADIX = 1 << SLOT_SHIFT
VMEM_LIMIT = 56 * 1024 * 1024

_NT = (((1,), (1,)), ((), ()))
_TN = (((0,), (0,)), ((), ()))


def _params(*sem):
    return pltpu.CompilerParams(dimension_semantics=sem, vmem_limit_bytes=VMEM_LIMIT)


def _ln_mod(x, g, sc, sh):
    y = x * lax.rsqrt(jnp.mean(x * x, axis=-1, keepdims=True) + RMS_EPS)
    return (y * g) * (1.0 + sc) + sh


def _mod_kernel(c_ref, w_ref, b_ref, o_ref):
    c = c_ref[...]
    s = c * jax.nn.sigmoid(c)
    o_ref[...] = jnp.dot(s, w_ref[...], precision=HIGHEST, preferred_element_type=F32) + b_ref[...]


def _modulation(c, mod_w, mod_b):
    depth, d, n = mod_w.shape
    b = c.shape[0]
    tn = 1536
    return pl.pallas_call(
        _mod_kernel,
        grid=(depth, n // tn),
        in_specs=[
            pl.BlockSpec((b, d), lambda l, j: (0, 0)),
            pl.BlockSpec((None, d, tn), lambda l, j: (l, 0, j)),
            pl.BlockSpec((None, 1, tn), lambda l, j: (l, 0, j)),
        ],
        out_specs=pl.BlockSpec((None, b, tn), lambda l, j: (l, 0, j)),
        out_shape=jax.ShapeDtypeStruct((depth, b, n), F32),
        compiler_params=_params("parallel", "parallel"),
        name="adaln_mod",
    )(c, mod_w, mod_b.reshape(depth, 1, n))


def _row_spec(d, tiles_per_seq, k):
    return pl.BlockSpec((None, None, 1, d), lambda i: (i // tiles_per_seq, k, 0, 0))


def _inproj_kernel(x_ref, g_ref, sc_ref, sh_ref, w_ref, o_ref):
    h = _ln_mod(x_ref[...], g_ref[...], sc_ref[...], sh_ref[...]).astype(BF16)
    o_ref[...] = jnp.dot(h, w_ref[...], preferred_element_type=F32)


def _in_projection(x2, g, modl, w_bf, seq, tm=512):
    t, d = x2.shape
    n = w_bf.shape[1]
    tps = seq // tm
    return pl.pallas_call(
        _inproj_kernel,
        grid=(t // tm,),
        in_specs=[
            pl.BlockSpec((tm, d), lambda i: (i, 0)),
            pl.BlockSpec((1, d), lambda i: (0, 0)),
            _row_spec(d, tps, 1),
            _row_spec(d, tps, 0),
            pl.BlockSpec((d, n), lambda i: (0, 0)),
        ],
        out_specs=pl.BlockSpec((tm, n), lambda i: (i, 0)),
        out_shape=jax.ShapeDtypeStruct((t, n), F32),
        compiler_params=_params("parallel"),
        name="in_proj",
    )(x2, g, modl, modl, w_bf)


def _bias_table(rpb):
    heads, nr, nc = rpb.shape
    assert nr == 2 * WIN_ROWS - 1 and nc == 2 * WIN_COLS - 1
    n_cb = GRID_W // COL_BLOCK
    qcol = (np.arange(n_cb) * COL_BLOCK)[:, None, None] + np.arange(COL_BLOCK)[None, :, None]
    kcol = np.array([_key_col0(c) for c in range(n_cb)])[:, None, None] + np.arange(KEY_COLS)[None, None, :]
    cs = np.clip(qcol - WIN_COLS // 2, 0, GRID_W - WIN_COLS)
    col_ok = (kcol >= cs) & (kcol < cs + WIN_COLS)
    col_sel = col_ok[..., None] & (np.arange(nc) == (kcol - qcol + WIN_COLS - 1)[..., None])
    place = [[(u, 0) for u in range(ROWS_PER_STEP)],
             [(WIN_ROWS // 2 + u, u) for u in range(ROWS_PER_STEP)],
             [(BAND_ROWS - ROWS_PER_STEP + u, BAND_ROWS - WIN_ROWS) for u in range(ROWS_PER_STEP)]]
    row_sel = np.zeros((len(place), ROWS_PER_STEP, BAND_ROWS, nr), bool)
    for x, per_u in enumerate(place):
        for u, (off, lo) in enumerate(per_u):
            for i in range(lo, lo + WIN_ROWS):
                row_sel[x, u, i, i - off + WIN_ROWS - 1] = True
    row_ok = row_sel.any(axis=-1)
    n_case, nq, nk = len(place), ROWS_PER_STEP * COL_BLOCK, BAND_ROWS * KEY_COLS
    cols = jnp.einsum('hab,cqkb->hcqak', rpb, col_sel.astype(np.float32), precision=HIGHEST)
    depth = -(-nr * KEY_COLS // LANES) * LANES
    cols = jnp.pad(cols.reshape(heads * n_cb * COL_BLOCK, nr * KEY_COLS), ((0, 0), (0, depth - nr * KEY_COLS)))
    sel = row_sel.transpose(0, 3, 1, 2)[:, :, None, :, :, None] & np.eye(KEY_COLS, dtype=bool)[None, None, :, None, None, :]
    sel = np.pad(sel.reshape(n_case, nr * KEY_COLS, ROWS_PER_STEP * nk), ((0, 0), (0, depth - nr * KEY_COLS), (0, 0)))
    ok = row_ok[:, None, :, None, :, None] & col_ok[None, :, None, :, None, :]
    mask = np.where(ok, 0.0, NEG_BIAS).astype(np.float32).reshape(n_case, n_cb, nq, nk)
    return pl.pallas_call(
        functools.partial(_bias_kernel, n_u=ROWS_PER_STEP),
        grid=(n_case,),
        in_specs=[pl.BlockSpec(cols.shape, lambda x: (0, 0)),
                  pl.BlockSpec((None,) + sel.shape[1:], lambda x: (x, 0, 0)),
                  pl.BlockSpec((None, n_cb, nq, nk), lambda x: (x, 0, 0, 0))],
        out_specs=pl.BlockSpec((heads, None, n_cb, nq, nk), lambda x: (0, x, 0, 0, 0)),
        out_shape=jax.ShapeDtypeStruct((heads, n_case, n_cb, nq, nk), F32),
        compiler_params=_params("parallel"),
        name="bias_table",
    )(cols, jnp.asarray(sel, BF16), jnp.asarray(mask))


def _bias_kernel(cols_ref, sel_ref, mask_ref, o_ref, *, n_u):
    heads, n_cb, nq, nk = o_ref.shape
    nqc = nq // n_u
    t = cols_ref[...]
    t0 = t.astype(BF16)
    r = t - t0.astype(F32)
    t1 = r.astype(BF16)
    t2 = (r - t1.astype(F32)).astype(BF16)
    sel = sel_ref[...]
    res = (jnp.dot(t0, sel, preferred_element_type=F32) + jnp.dot(t1, sel, preferred_element_type=F32)
           + jnp.dot(t2, sel, preferred_element_type=F32))
    for h in range(heads):
        for c in range(n_cb):
            rows = slice((h * n_cb + c) * nqc, (h * n_cb + c + 1) * nqc)
            for u in range(n_u):
                o_ref[h, c, u * nqc:(u + 1) * nqc, :] = (res[rows, u * nk:(u + 1) * nk]
                                                         + mask_ref[c, u * nqc:(u + 1) * nqc, :])


def _key_col0(c):
    return min(max(c * COL_BLOCK - WIN_COLS // 2, 0), GRID_W - KEY_COLS)


def _mixers_kernel(q_ref, k_ref, v_ref, bg_ref, cg_ref, xv_ref, qkg_ref, tbl_ref, cw_ref,
                   attn_ref, conv_ref, qn_s, kn_s, qc_s, kc_s, vc_s, pad_s, *, rows):
    seq = q_ref.shape[0]
    lane = lax.broadcasted_iota(jnp.int32, (1, LANES), 1)
    first = lane < HEAD_DIM

    def head_norm(x, g):
        xx = x * x
        s0 = jnp.sum(jnp.where(first, xx, 0.0), axis=-1, keepdims=True)
        s1 = jnp.sum(jnp.where(first, 0.0, xx), axis=-1, keepdims=True)
        ms = jnp.where(first, s0, s1) * (1.0 / HEAD_DIM)
        return x * lax.rsqrt(ms + RMS_EPS) * g

    qn_s[...] = head_norm(q_ref[...], qkg_ref[0:1, :]) * (1.0 / math.sqrt(HEAD_DIM))
    kn_s[...] = head_norm(k_ref[...], qkg_ref[1:2, :])

    n_cb = GRID_W // COL_BLOCK
    for c in range(n_cb):
        k0 = _key_col0(c)
        for r in range(rows):
            qp = qn_s[r * GRID_W + c * COL_BLOCK:r * GRID_W + (c + 1) * COL_BLOCK, :]
            qc_s[0, c, r * COL_BLOCK:(r + 1) * COL_BLOCK, :] = jnp.where(first, qp, 0.0).astype(BF16)
            qc_s[1, c, r * COL_BLOCK:(r + 1) * COL_BLOCK, :] = jnp.where(first, 0.0, qp).astype(BF16)
            keys = slice(r * GRID_W + k0, r * GRID_W + k0 + KEY_COLS)
            dst = slice(r * KEY_COLS, (r + 1) * KEY_COLS)
            kc_s[c, dst, :] = kn_s[keys, :].astype(BF16)
            vp = v_ref[keys, :]
            vc_s[0, c, dst, :] = jnp.where(first, vp, 1.0).astype(BF16)
            vc_s[1, c, dst, :] = jnp.where(first, 1.0, vp).astype(BF16)

    n_groups = rows // ROWS_PER_STEP
    nq = ROWS_PER_STEP * COL_BLOCK
    for g in range(n_groups):
        b0 = min(max(g * ROWS_PER_STEP - WIN_ROWS // 2, 0), rows - BAND_ROWS)
        case = 0 if g == 0 else (2 if g == n_groups - 1 else 1)
        patch = slice(b0 * KEY_COLS, (b0 + BAND_ROWS) * KEY_COLS)
        blocks = [(c, h) for c in range(n_cb) for h in range(2)]
        s = jnp.concatenate(
            [lax.dot_general(qc_s[h, c, g * nq:(g + 1) * nq, :], kc_s[c, patch, :], _NT,
                             preferred_element_type=F32) + tbl_ref[h, case, c] for c, h in blocks], axis=0)
        p = jnp.exp(s - jnp.max(s, axis=-1, keepdims=True)).astype(BF16)
        pv = jnp.concatenate(
            [jnp.dot(p[i * nq:(i + 1) * nq, :], vc_s[h, c, patch, :], preferred_element_type=F32)
             for i, (c, h) in enumerate(blocks)], axis=0)
        o = pv / pltpu.roll(pv, HEAD_DIM, axis=1)
        for c in range(n_cb):
            out = jnp.where(first, o[2 * c * nq:(2 * c + 1) * nq, :], o[(2 * c + 1) * nq:(2 * c + 2) * nq, :])
            for u in range(ROWS_PER_STEP):
                row0 = (g * ROWS_PER_STEP + u) * GRID_W + c * COL_BLOCK
                attn_ref[row0:row0 + COL_BLOCK, :] = out[u * COL_BLOCK:(u + 1) * COL_BLOCK, :].astype(BF16)

    zeros = jnp.zeros((8, LANES), F32)
    pad_s[0:8, :] = zeros
    pad_s[seq + 8:seq + 16, :] = zeros
    pad_s[8:seq + 8, :] = cg_ref[...] * xv_ref[...]
    z = (pad_s[7:seq + 7, :] * cw_ref[0:1, :] + pad_s[8:seq + 8, :] * cw_ref[1:2, :]
         + pad_s[9:seq + 9, :] * cw_ref[2:3, :])
    conv_ref[...] = (bg_ref[...] * z).astype(BF16)


def _mixers(proj3, qk_g, tbl, layer, conv_w):
    b, seq, n = proj3.shape
    width = n // 6
    nblk = width // LANES
    rows = seq // GRID_W
    assert rows >= BAND_ROWS + ROWS_PER_STEP and rows % ROWS_PER_STEP == 0
    assert LANES == 2 * HEAD_DIM and GRID_W % COL_BLOCK == 0
    qkg = jnp.tile(qk_g, (1, LANES // HEAD_DIM))
    n_cb = GRID_W // COL_BLOCK

    def col(k):
        return pl.BlockSpec((None, seq, LANES), lambda i, j, k=k: (i, 0, k * nblk + j))

    out_spec = pl.BlockSpec((None, seq, LANES), lambda i, j: (i, 0, j))
    hp = LANES // HEAD_DIM
    return pl.pallas_call(
        functools.partial(_mixers_kernel, rows=rows),
        grid=(b, nblk),
        in_specs=[col(0), col(1), col(2), col(3), col(4), col(5),
                  pl.BlockSpec((2, LANES), lambda i, j: (0, 0)),
                  pl.BlockSpec((hp,) + tbl.shape[1:], lambda i, j: (layer * nblk + j, 0, 0, 0, 0)),
                  pl.BlockSpec((conv_w.shape[0], LANES), lambda i, j: (0, j))],
        out_specs=[out_spec, out_spec],
        out_shape=[jax.ShapeDtypeStruct((b, seq, width), BF16)] * 2,
        scratch_shapes=[pltpu.VMEM((seq, LANES), F32), pltpu.VMEM((seq, LANES), F32),
                        pltpu.VMEM((hp, n_cb, rows * COL_BLOCK, LANES), BF16),
                        pltpu.VMEM((n_cb, rows * KEY_COLS, LANES), BF16),
                        pltpu.VMEM((hp, n_cb, rows * KEY_COLS, LANES), BF16),
                        pltpu.VMEM((seq + 16, LANES), F32)],
        compiler_params=_params("parallel", "parallel"),
        name="attn_conv",
    )(proj3, proj3, proj3, proj3, proj3, proj3, qkg, tbl, conv_w)


def _route(x, g_ref, sc_ref, sh_ref, rw_ref, h_ref, aff_ref):
    h = _ln_mod(x, g_ref[...], sc_ref[...], sh_ref[...])
    tm, d = h.shape
    nch = d // LANES
    n_exp = aff_ref.shape[0]
    for j in range(nch):
        h_ref[pl.ds(j, tm, stride=nch), :] = h[:, j * LANES:(j + 1) * LANES]
    h0 = h.astype(BF16)
    h1 = (h - h0.astype(F32)).astype(BF16)
    lg = jnp.dot(h0, rw_ref[...], preferred_element_type=F32)
    lg = (lg[:, :LANES] + lg[:, LANES:]) + jnp.dot(h1, rw_ref[:, :LANES], preferred_element_type=F32)
    lane = lax.broadcasted_iota(jnp.int32, (1, LANES), 1)
    lg = jnp.where(lane < n_exp, lg, NEG_BIAS)
    p = jnp.exp(lg - jnp.max(lg, axis=-1, keepdims=True))
    aff = p / jnp.sum(p, axis=-1, keepdims=True)
    aff_ref[...] = aff.T[:n_exp, :]


def _router_weights(router_w):
    w = jnp.pad(router_w, ((0, 0), (0, LANES - router_w.shape[1])))
    w0 = w.astype(BF16)
    w1 = (w - w0.astype(F32)).astype(BF16)
    return jnp.concatenate([w0, w1], axis=1)


def _route_specs(b, seq, d, e, tm):
    nch = d // LANES
    in_specs = [pl.BlockSpec((1, d), lambda i, j: (0, 0)),
                pl.BlockSpec((None, None, 1, d), lambda i, j: (i, 4, 0, 0)),
                pl.BlockSpec((None, None, 1, d), lambda i, j: (i, 3, 0, 0)),
                pl.BlockSpec((d, 2 * LANES), lambda i, j: (0, 0))]
    out_specs = [pl.BlockSpec((None, tm * nch, LANES), lambda i, j: (i, j, 0)),
                 pl.BlockSpec((None, e, tm), lambda i, j: (i, 0, j))]
    out_shape = [jax.ShapeDtypeStruct((b, seq * nch, LANES), F32),
                 jax.ShapeDtypeStruct((b, e, seq), F32)]
    return in_specs, out_specs, out_shape


def _outproj_kernel(a_ref, c_ref, wa_ref, wc_ref, x_ref, gate_ref, g2_ref, sc2_ref, sh2_ref, rw_ref,
                    o_ref, h_ref, aff_ref):
    m = jnp.dot(a_ref[...], wa_ref[...], preferred_element_type=F32)
    m = m + jnp.dot(c_ref[...], wc_ref[...], preferred_element_type=F32)
    x = x_ref[...] + gate_ref[...] * m
    o_ref[...] = x
    _route(x, g2_ref, sc2_ref, sh2_ref, rw_ref, h_ref, aff_ref)


def _out_projection(attn3, conv3, w_bf, x3, modl, g2, rw01, n_exp, tm=512):
    b, seq, d = x3.shape
    wa = attn3.shape[2]
    wc = conv3.shape[2]
    r_in, r_out, r_shape = _route_specs(b, seq, d, n_exp, tm)
    tile = pl.BlockSpec((None, tm, d), lambda i, j: (i, j, 0))
    return pl.pallas_call(
        _outproj_kernel,
        grid=(b, seq // tm),
        in_specs=[
            pl.BlockSpec((None, tm, wa), lambda i, j: (i, j, 0)),
            pl.BlockSpec((None, tm, wc), lambda i, j: (i, j, 0)),
            pl.BlockSpec((wa, d), lambda i, j: (0, 0)),
            pl.BlockSpec((wc, d), lambda i, j: (1, 0)),
            tile,
            pl.BlockSpec((None, None, 1, d), lambda i, j: (i, 2, 0, 0)),
        ] + r_in,
        out_specs=[tile] + r_out,
        out_shape=[jax.ShapeDtypeStruct((b, seq, d), F32)] + r_shape,
        compiler_params=_params("parallel", "parallel"),
        name="out_proj",
    )(attn3, conv3, w_bf, w_bf, x3, modl, g2, modl, modl, rw01)


def _dft_tables(n):
    jk = (np.arange(n)[:, None] * np.arange(n)[None, :]) % n
    ang = 2.0 * np.pi * jk.astype(np.float64) / n
    return np.cos(ang).astype(np.float32), np.sin(ang).astype(np.float32)


def _fnet_prep_kernel(x_ref, g_ref, sc_ref, sh_ref, h_ref):
    h_ref[...] = _ln_mod(x_ref[...], g_ref[...], sc_ref[...], sh_ref[...]).astype(BF16)


def _fnet_seq_kernel(cs_ref, ss_ref, h_ref, cc_ref, scn_ref, w_ref, x_ref, gate_ref, g2_ref, sc2_ref,
                     sh2_ref, rw_ref, o_ref, hs_ref, aff_ref, pq_s, m_s, stash_s, *, norm, gw, n_half):
    tm, d = x_ref.shape
    k = pl.program_id(1)

    @pl.when(k < n_half)
    def _():
        rows = pl.ds(pl.multiple_of(k * tm, tm), pq_s.shape[1])
        h = h_ref[...]
        pq_s[0] = jnp.dot(cs_ref[rows, :], h, preferred_element_type=F32)
        pq_s[1] = jnp.dot(ss_ref[rows, :], h, preferred_element_type=F32)
        p = jnp.concatenate([pq_s[0, 0:tm, :], pq_s[0, 1:tm + 1, :]], axis=0).astype(BF16)
        q = jnp.concatenate([pq_s[1, 0:tm, :], pq_s[1, 1:tm + 1, :]], axis=0).astype(BF16)
        lo, up = [], []
        for g in range(d // gw):
            u = jnp.dot(p[:, g * gw:(g + 1) * gw], cc_ref[...], preferred_element_type=F32)
            v = jnp.dot(q[:, g * gw:(g + 1) * gw], scn_ref[...], preferred_element_type=F32)
            lo.append(((u[:tm] - v[:tm]) * norm).astype(BF16))
            up.append(((u[tm:] + v[tm:]) * norm).astype(BF16))
        f_lo = jnp.concatenate(lo, axis=1)
        r = lax.broadcasted_iota(jnp.int32, (tm, tm), 0)
        c = lax.broadcasted_iota(jnp.int32, (tm, tm), 1)
        flip = jnp.where(r + c == tm - 1, 1.0, 0.0).astype(BF16)
        f_up = jnp.dot(flip, jnp.concatenate(up, axis=1), preferred_element_type=F32).astype(BF16)
        m_s[...] = jnp.dot(f_lo, w_ref[...], preferred_element_type=F32)
        stash_s[n_half - 1 - k] = jnp.dot(f_up, w_ref[...], preferred_element_type=F32)

    @pl.when(k >= n_half)
    def _():
        m_s[...] = stash_s[k - n_half]

    x = x_ref[...] + gate_ref[...] * m_s[...]
    o_ref[...] = x
    _route(x, g2_ref, sc2_ref, sh2_ref, rw_ref, hs_ref, aff_ref)


def _fnet_layer(x3, g, modl, w_bf, g2, rw01, n_exp, tm=256, tp=512):
    b, seq, d = x3.shape
    gw = d // FOURIER_GROUPS
    half = seq // 2
    assert half % tm == 0 and tm % 16 == 0
    pad = 16
    cc, sc = _dft_tables(gw)
    cseq, sseq = _dft_tables(seq)
    cs_half = jnp.asarray(cseq[:half + pad]).astype(BF16)
    ss_half = jnp.asarray(sseq[:half + pad]).astype(BF16)
    tps = seq // tp
    h = pl.pallas_call(
        _fnet_prep_kernel,
        grid=(b * tps,),
        in_specs=[
            pl.BlockSpec((tp, d), lambda i: (i, 0)),
            pl.BlockSpec((1, d), lambda i: (0, 0)),
            _row_spec(d, tps, 1),
            _row_spec(d, tps, 0),
        ],
        out_specs=pl.BlockSpec((tp, d), lambda i: (i, 0)),
        out_shape=jax.ShapeDtypeStruct((b * seq, d), BF16),
        compiler_params=_params("parallel"),
        name="fnet_prep",
    )(x3.reshape(b * seq, d), g, modl, modl)
    n_half = half // tm
    r_in, r_out, r_shape = _route_specs(b, seq, d, n_exp, tm)
    tile = pl.BlockSpec((None, tm, d), lambda i, j: (i, j, 0))
    whole = lambda a: pl.BlockSpec(a.shape, lambda i, j: (0,) * a.ndim)
    cc_bf = jnp.asarray(cc).astype(BF16)
    sc_bf = jnp.asarray(sc).astype(BF16)
    return pl.pallas_call(
        functools.partial(_fnet_seq_kernel, norm=1.0 / math.sqrt(seq * gw), gw=gw, n_half=n_half),
        grid=(b, 2 * n_half),
        in_specs=[
            whole(cs_half), whole(ss_half),
            pl.BlockSpec((None, seq, d), lambda i, j: (i, 0, 0)),
            whole(cc_bf), whole(sc_bf), whole(w_bf),
            tile,
            pl.BlockSpec((None, None, 1, d), lambda i, j: (i, 2, 0, 0)),
        ] + r_in,
        out_specs=[tile] + r_out,
        out_shape=[jax.ShapeDtypeStruct((b, seq, d), F32)] + r_shape,
        scratch_shapes=[pltpu.VMEM((2, tm + pad, d), F32), pltpu.VMEM((tm, d), F32),
                        pltpu.VMEM((n_half, tm, d), F32)],
        compiler_params=_params("parallel", "arbitrary"),
        name="fnet_seq",
    )(cs_half, ss_half, h.reshape(b, seq, d), cc_bf, sc_bf, w_bf, x3, modl, g2, modl, modl, rw01)


def _topk_kernel(aff_ref, tok_ref, gs_ref, tri_s, *, cap, row_scale):
    n_exp, seq = aff_ref.shape

    @pl.when(pl.program_id(0) == 0)
    def _():
        r = lax.broadcasted_iota(jnp.int32, (seq, seq), 0)
        c = lax.broadcasted_iota(jnp.int32, (seq, seq), 1)
        tri_s[...] = jnp.where(r < c, 1.0, 0.0).astype(BF16)

    a = aff_ref[...]

    def search(i, t):
        cand = t | lax.shift_left(jnp.int32(1), 30 - i)
        cnt = jnp.sum((a >= pltpu.bitcast(cand, F32)).astype(jnp.int32), axis=1, keepdims=True)
        return jnp.where(cnt >= cap, cand, t)

    thr = lax.fori_loop(0, 31, search, jnp.zeros((a.shape[0], 1), jnp.int32))
    thr = pltpu.bitcast(thr, F32)
    gt = a > thr
    eq = a == thr
    need = cap - jnp.sum(gt.astype(jnp.int32), axis=1, keepdims=True)
    tri = tri_s[...]
    eq_before = jnp.dot(jnp.where(eq, 1.0, 0.0).astype(BF16), tri, preferred_element_type=F32)
    sel = gt | (eq & (eq_before < need.astype(F32)))
    before = jnp.dot(jnp.where(sel, 1.0, 0.0).astype(BF16), tri, preferred_element_type=F32)
    rank = jnp.where(sel, before.astype(jnp.int32), -1)

    digit = lax.broadcasted_iota(jnp.int32, (SLOT_RADIX, seq), 0)
    tok_ids = lax.broadcasted_iota(jnp.int32, (1, seq), 1)
    t_hi = lax.shift_right_logical(tok_ids, 6).astype(F32)
    t_lo = (tok_ids & 63).astype(F32)
    for e in range(n_exp):
        r = rank[e:e + 1, :]
        hi = jnp.where(digit == lax.shift_right_arithmetic(r, SLOT_SHIFT), 1.0, 0.0).astype(BF16)
        lo = digit == (r & (SLOT_RADIX - 1))
        g = a[e:e + 1, :]
        g0 = g.astype(BF16).astype(F32)
        g1 = (g - g0).astype(BF16).astype(F32)
        g2 = (g - g0) - g1
        vals = jnp.concatenate([jnp.where(lo, v, 0.0) for v in (t_hi, t_lo, g0, g1, g2)], axis=0)
        res = lax.dot_general(hi, vals.astype(BF16), _NT, preferred_element_type=F32)
        part = [res[:, k * SLOT_RADIX:(k + 1) * SLOT_RADIX] for k in range(5)]
        tok_ref[e] = ((part[0] * 64.0 + part[1]) * float(row_scale)).astype(jnp.int32)
        gs_ref[e] = (part[2] + part[3]) + part[4]


def _topk(aff, cap, row_scale):
    b, e, seq = aff.shape
    n_hi = cap // SLOT_RADIX
    assert cap % SLOT_RADIX == 0 and n_hi <= SLOT_RADIX and seq <= 64 * 64
    out_spec = pl.BlockSpec((None, e, SLOT_RADIX, SLOT_RADIX), lambda i: (i, 0, 0, 0))
    tok, gs = pl.pallas_call(
        functools.partial(_topk_kernel, cap=cap, row_scale=row_scale),
        grid=(b,),
        in_specs=[pl.BlockSpec((None, e, seq), lambda i: (i, 0, 0))],
        out_specs=[out_spec, out_spec],
        out_shape=[jax.ShapeDtypeStruct((b, e, SLOT_RADIX, SLOT_RADIX), jnp.int32),
                   jax.ShapeDtypeStruct((b, e, SLOT_RADIX, SLOT_RADIX), F32)],
        scratch_shapes=[pltpu.VMEM((seq, seq), BF16)],
        compiler_params=_params("arbitrary"),
        name="topk",
    )(aff)
    return tok[:, :, :n_hi].reshape(b, e, cap), gs[:, :, :n_hi].reshape(b, e, cap)


def _gather_kernel(tok_ref, h_ref, o_ref, x_s, *, stride):
    n_exp, cap = tok_ref.shape
    nch = x_s.shape[0] // stride

    def expert(e, carry):
        for i in range(cap):
            t = pl.multiple_of(tok_ref[e, i], nch)
            x_s[pl.ds(i, nch, stride=stride), :] = h_ref[pl.ds(t, nch), :]
        xe = jnp.concatenate([x_s[j * stride:j * stride + cap, :] for j in range(nch)], axis=-1)
        o_ref[e] = xe.astype(BF16)
        return carry

    lax.fori_loop(0, n_exp, expert, 0)


def _gather_tokens(h_slab, tok, d, stride):
    b, rows, _ = h_slab.shape
    _, e, cap = tok.shape
    nch = d // LANES
    return pl.pallas_call(
        functools.partial(_gather_kernel, stride=stride),
        grid=(b,),
        in_specs=[
            pl.BlockSpec((None, e, cap), lambda i: (i, 0, 0), memory_space=pltpu.SMEM),
            pl.BlockSpec((None, rows, LANES), lambda i: (i, 0, 0)),
        ],
        out_specs=pl.BlockSpec((e, None, cap, d), lambda i: (0, i, 0, 0)),
        out_shape=jax.ShapeDtypeStruct((e, b, cap, d), BF16),
        scratch_shapes=[pltpu.VMEM((nch * stride, LANES), F32)],
        compiler_params=_params("parallel"),
        name="moe_gather",
    )(tok, h_slab)


def _ffn_kernel(x_ref, gs_ref, wg_ref, wu_ref, wd_ref, y_ref, wg_s, wu_s, wd_s, *, cap, ff_chunk):
    @pl.when(pl.program_id(1) == 0)
    def _():
        wg_s[...] = wg_ref[...].astype(BF16)
        wu_s[...] = wu_ref[...].astype(BF16)
        wd_s[...] = wd_ref[...].astype(BF16)

    x = x_ref[...]
    y = None
    for c in range(wg_s.shape[1] // ff_chunk):
        cols = slice(c * ff_chunk, (c + 1) * ff_chunk)
        a = jnp.dot(x, wg_s[:, cols], preferred_element_type=F32)
        u = jnp.dot(x, wu_s[:, cols], preferred_element_type=F32)
        hm = (a * jax.nn.sigmoid(a) * u).astype(BF16)
        yc = jnp.dot(hm, wd_s[cols, :], preferred_element_type=F32)
        y = yc if y is None else y + yc
    g = jnp.broadcast_to(gs_ref[...], (LANES, x.shape[0])).T
    bpb, nch, stride, _ = y_ref.shape
    for bb in range(bpb):
        rows = slice(bb * cap, (bb + 1) * cap)
        for j in range(nch):
            y_ref[bb, j, 0:cap, :] = y[rows, j * LANES:(j + 1) * LANES] * g[rows, :]
        y_ref[bb, :, cap:, :] = jnp.zeros((nch, stride - cap, LANES), F32)


def _expert_ffn(xe, gs_col, layer, wg, wu, wd, cap, stride, tm):
    e, m, d = xe.shape
    ff = wg.shape[3]
    nch = d // LANES
    bpb = tm // cap
    return pl.pallas_call(
        functools.partial(_ffn_kernel, cap=cap, ff_chunk=min(512, ff)),
        grid=(e, m // tm),
        in_specs=[
            pl.BlockSpec((None, tm, d), lambda i, j: (i, j, 0)),
            pl.BlockSpec((None, 1, tm), lambda i, j: (i, 0, j)),
            pl.BlockSpec((None, None, d, ff), lambda i, j: (layer, i, 0, 0)),
            pl.BlockSpec((None, None, d, ff), lambda i, j: (layer, i, 0, 0)),
            pl.BlockSpec((None, None, ff, d), lambda i, j: (layer, i, 0, 0)),
        ],
        out_specs=pl.BlockSpec((None, bpb, nch, stride, LANES), lambda i, j: (i, j, 0, 0, 0)),
        out_shape=jax.ShapeDtypeStruct((e, m // cap, nch, stride, LANES), F32),
        scratch_shapes=[pltpu.VMEM((d, ff), BF16), pltpu.VMEM((d, ff), BF16), pltpu.VMEM((ff, d), BF16)],
        compiler_params=_params("parallel", "arbitrary"),
        name="moe_ffn",
    )(xe, gs_col, wg, wu, wd)


def _scatter_kernel(tok_ref, y_ref, x_ref, gate_ref, o_ref, acc_s, *, stride, n_steps):
    group, _, cap = tok_ref.shape
    ts, d = x_ref.shape
    nch = d // LANES
    k = pl.program_id(1)

    @pl.when(k == 0)
    def _():
        acc_s[...] = jnp.zeros_like(acc_s)

    @pl.when(k < n_steps)
    def _():
        def expert(ee, carry):
            def slab(i):
                return pl.ds(pl.multiple_of(tok_ref[ee, 0, i], nch), nch)

            for i0 in range(0, cap, SCATTER_BATCH):
                vals = [acc_s[slab(i), :] + y_ref[ee, pl.ds(i, nch, stride=stride), :]
                        for i in range(i0, i0 + SCATTER_BATCH)]
                for i, v in zip(range(i0, i0 + SCATTER_BATCH), vals):
                    acc_s[slab(i), :] = v
            return carry

        lax.fori_loop(0, group, expert, 0)

    @pl.when(k >= n_steps)
    def _():
        base = pl.multiple_of((k - n_steps) * (ts * nch), ts * nch)
        for j in range(nch):
            lanes = slice(j * LANES, (j + 1) * LANES)
            o_ref[:, lanes] = (x_ref[:, lanes]
                               + gate_ref[:, lanes] * acc_s[pl.ds(base + j, ts, stride=nch), :])


def _scatter_residual(y_cm, tok, x3, modl, stride):
    e, b, _, _ = y_cm.shape
    _, seq, d = x3.shape
    cap = tok.shape[-1]
    nch = d // LANES
    assert cap % SCATTER_BATCH == 0
    ts = min(512, seq)
    group = SCATTER_EXPERTS
    assert e % group == 0
    n_steps = e // group
    smem_rows = pl.BlockSpec((None, group, 1, cap), lambda i, k: (i, jnp.minimum(k, n_steps - 1), 0, 0),
                             memory_space=pltpu.SMEM)
    tile = pl.BlockSpec((None, ts, d), lambda i, k: (i, jnp.maximum(k - n_steps, 0), 0))
    return pl.pallas_call(
        functools.partial(_scatter_kernel, stride=stride, n_steps=n_steps),
        grid=(b, n_steps + seq // ts),
        in_specs=[
            smem_rows,
            pl.BlockSpec((group, None, nch * stride, LANES),
                         lambda i, k: (jnp.minimum(k, n_steps - 1), i, 0, 0)),
            tile,
            pl.BlockSpec((None, None, 1, d), lambda i, k: (i, 5, 0, 0)),
        ],
        out_specs=tile,
        out_shape=jax.ShapeDtypeStruct((b, seq, d), F32),
        scratch_shapes=[pltpu.VMEM((seq * nch, LANES), F32)],
        compiler_params=_params("parallel", "arbitrary"),
        name="moe_scatter",
    )(tok.reshape(b, e, 1, cap), y_cm, x3, modl)


def _moe_layer(x3, h_slab, aff, modl, layer, wg, wu, wd):
    b, seq, d = x3.shape
    e = aff.shape[1]
    cap = max(1, CAPACITY_FACTOR * seq // e)
    stride = cap + 8
    tok, gs = _topk(aff, cap, d // LANES)
    xe = _gather_tokens(h_slab, tok, d, stride)
    gs_col = gs.transpose(1, 0, 2).reshape(e, 1, b * cap)
    y = _expert_ffn(xe.reshape(e, b * cap, d), gs_col, layer, wg, wu, wd, cap, stride,
                    tm=min(1024, b * cap))
    return _scatter_residual(y.reshape(e, b, -1, LANES), tok, x3, modl, stride)


def kernel(x, c, mod_w, mod_b, norm_g, mix_w_in, qk_g, rpb, conv_w, mix_w_out, fnet_w_out,
           router_w, exp_w_gate, exp_w_up, exp_w_down):
    b, seq, d = x.shape
    depth = mod_w.shape[0]
    mod = _modulation(c, mod_w, mod_b).reshape(depth, b, N_MOD, 1, d)
    tbl = _bias_table(rpb.reshape((-1,) + rpb.shape[2:]))
    for l in range(depth):
        modl = mod[l]
        g1 = norm_g[l, 0].reshape(1, d)
        g2 = norm_g[l, 1].reshape(1, d)
        rw01 = _router_weights(router_w[l])
        n_exp = router_w.shape[2]
        if l % 2 == 0:
            j = l // 2
            proj = _in_projection(x.reshape(b * seq, d), g1, modl, mix_w_in[j].astype(BF16), seq)
            attn, conv = _mixers(proj.reshape(b, seq, -1), qk_g[j], tbl, j, conv_w[j])
            x, h_slab, aff = _out_projection(attn, conv, mix_w_out[j].astype(BF16), x, modl,
                                             g2, rw01, n_exp)
        else:
            x, h_slab, aff = _fnet_layer(x, g1, modl, fnet_w_out[l // 2].astype(BF16), g2, rw01, n_exp)
        x = _moe_layer(x, h_slab, aff, modl, l, exp_w_gate, exp_w_up, exp_w_down)
    return x
```

```python
import functools
import math

import numpy as np
import jax
import jax.numpy as jnp
from jax import lax
from jax.experimental import pallas as pl
from jax.experimental.pallas import tpu as pltpu

F32 = jnp.float32
BF16 = jnp.bfloat16
HIGHEST = lax.Precision.HIGHEST

GRID_W = 64
WIN_ROWS = 8
WIN_COLS = 16
HEAD_DIM = 64
FOURIER_GROUPS = 4
N_MOD = 6
CAPACITY_FACTOR = 2
RMS_EPS = 1e-6
NEG_BIAS = -1e30
LANES = 128
ROWS_PER_STEP = WIN_ROWS // 2
BAND_ROWS = WIN_ROWS + ROWS_PER_STEP
COL_BLOCK = WIN_COLS
KEY_COLS = 2 * WIN_COLS
SCATTER_BATCH = 8
SCATTER_EXPERTS = 4
SLOT_SHIFT = 4
SLOT_RADIX = 1 << SLOT_SHIFT
VMEM_LIMIT = 56 * 1024 * 1024

_NT = (((1,), (1,)), ((), ()))
_TN = (((0,), (0,)), ((), ()))


def _params(*sem):
    return pltpu.CompilerParams(dimension_semantics=sem, vmem_limit_bytes=VMEM_LIMIT)


def _ln_mod(x, g, sc, sh):
    y = x * lax.rsqrt(jnp.mean(x * x, axis=-1, keepdims=True) + RMS_EPS)
    return (y * g) * (1.0 + sc) + sh


def _mod_kernel(c_ref, w_ref, b_ref, o_ref):
    c = c_ref[...]
    s = c * jax.nn.sigmoid(c)
    o_ref[...] = jnp.dot(s, w_ref[...], precision=HIGHEST, preferred_element_type=F32) + b_ref[...]


def _modulation(c, mod_w, mod_b):
    depth, d, n = mod_w.shape
    b = c.shape[0]
    tn = 1536
    return pl.pallas_call(
        _mod_kernel,
        grid=(depth, n // tn),
        in_specs=[
            pl.BlockSpec((b, d), lambda l, j: (0, 0)),
            pl.BlockSpec((None, d, tn), lambda l, j: (l, 0, j)),
            pl.BlockSpec((None, 1, tn), lambda l, j: (l, 0, j)),
        ],
        out_specs=pl.BlockSpec((None, b, tn), lambda l, j: (l, 0, j)),
        out_shape=jax.ShapeDtypeStruct((depth, b, n), F32),
        compiler_params=_params("parallel", "parallel"),
        name="adaln_mod",
    )(c, mod_w, mod_b.reshape(depth, 1, n))


def _row_spec(d, tiles_per_seq, k):
    return pl.BlockSpec((None, None, 1, d), lambda i: (i // tiles_per_seq, k, 0, 0))


def _inproj_kernel(x_ref, g_ref, sc_ref, sh_ref, w_ref, o_ref):
    h = _ln_mod(x_ref[...], g_ref[...], sc_ref[...], sh_ref[...]).astype(BF16)
    o_ref[...] = jnp.dot(h, w_ref[...], preferred_element_type=F32)


def _in_projection(x2, g, modl, w_bf, seq, tm=512):
    t, d = x2.shape
    n = w_bf.shape[1]
    tps = seq // tm
    return pl.pallas_call(
        _inproj_kernel,
        grid=(t // tm,),
        in_specs=[
            pl.BlockSpec((tm, d), lambda i: (i, 0)),
            pl.BlockSpec((1, d), lambda i: (0, 0)),
            _row_spec(d, tps, 1),
            _row_spec(d, tps, 0),
            pl.BlockSpec((d, n), lambda i: (0, 0)),
        ],
        out_specs=pl.BlockSpec((tm, n), lambda i: (i, 0)),
        out_shape=jax.ShapeDtypeStruct((t, n), F32),
        compiler_params=_params("parallel"),
        name="in_proj",
    )(x2, g, modl, modl, w_bf)


def _bias_table(rpb):
    heads, nr, nc = rpb.shape
    assert nr == 2 * WIN_ROWS - 1 and nc == 2 * WIN_COLS - 1
    n_cb = GRID_W // COL_BLOCK
    qcol = (np.arange(n_cb) * COL_BLOCK)[:, None, None] + np.arange(COL_BLOCK)[None, :, None]
    kcol = np.array([_key_col0(c) for c in range(n_cb)])[:, None, None] + np.arange(KEY_COLS)[None, None, :]
    cs = np.clip(qcol - WIN_COLS // 2, 0, GRID_W - WIN_COLS)
    col_ok = (kcol >= cs) & (kcol < cs + WIN_COLS)
    col_sel = col_ok[..., None] & (np.arange(nc) == (kcol - qcol + WIN_COLS - 1)[..., None])
    place = [[(u, 0) for u in range(ROWS_PER_STEP)],
             [(WIN_ROWS // 2 + u, u) for u in range(ROWS_PER_STEP)],
             [(BAND_ROWS - ROWS_PER_STEP + u, BAND_ROWS - WIN_ROWS) for u in range(ROWS_PER_STEP)]]
    row_sel = np.zeros((len(place), ROWS_PER_STEP, BAND_ROWS, nr), bool)
    for x, per_u in enumerate(place):
        for u, (off, lo) in enumerate(per_u):
            for i in range(lo, lo + WIN_ROWS):
                row_sel[x, u, i, i - off + WIN_ROWS - 1] = True
    row_ok = row_sel.any(axis=-1)
    n_case, nq, nk = len(place), ROWS_PER_STEP * COL_BLOCK, BAND_ROWS * KEY_COLS
    cols = jnp.einsum('hab,cqkb->hcqak', rpb, col_sel.astype(np.float32), precision=HIGHEST)
    depth = -(-nr * KEY_COLS // LANES) * LANES
    cols = jnp.pad(cols.reshape(heads * n_cb * COL_BLOCK, nr * KEY_COLS), ((0, 0), (0, depth - nr * KEY_COLS)))
    sel = row_sel.transpose(0, 3, 1, 2)[:, :, None, :, :, None] & np.eye(KEY_COLS, dtype=bool)[None, None, :, None, None, :]
    sel = np.pad(sel.reshape(n_case, nr * KEY_COLS, ROWS_PER_STEP * nk), ((0, 0), (0, depth - nr * KEY_COLS), (0, 0)))
    ok = row_ok[:, None, :, None, :, None] & col_ok[None, :, None, :, None, :]
    mask = np.where(ok, 0.0, NEG_BIAS).astype(np.float32).reshape(n_case, n_cb, nq, nk)
    return pl.pallas_call(
        functools.partial(_bias_kernel, n_u=ROWS_PER_STEP),
        grid=(n_case,),
        in_specs=[pl.BlockSpec(cols.shape, lambda x: (0, 0)),
                  pl.BlockSpec((None,) + sel.shape[1:], lambda x: (x, 0, 0)),
                  pl.BlockSpec((None, n_cb, nq, nk), lambda x: (x, 0, 0, 0))],
        out_specs=pl.BlockSpec((heads, None, n_cb, nq, nk), lambda x: (0, x, 0, 0, 0)),
        out_shape=jax.ShapeDtypeStruct((heads, n_case, n_cb, nq, nk), F32),
        compiler_params=_params("parallel"),
        name="bias_table",
    )(cols, jnp.asarray(sel, BF16), jnp.asarray(mask))


def _bias_kernel(cols_ref, sel_ref, mask_ref, o_ref, *, n_u):
    heads, n_cb, nq, nk = o_ref.shape
    nqc = nq // n_u
    t = cols_ref[...]
    t0 = t.astype(BF16)
    r = t - t0.astype(F32)
    t1 = r.astype(BF16)
    t2 = (r - t1.astype(F32)).astype(BF16)
    sel = sel_ref[...]
    res = (jnp.dot(t0, sel, preferred_element_type=F32) + jnp.dot(t1, sel, preferred_element_type=F32)
           + jnp.dot(t2, sel, preferred_element_type=F32))
    for h in range(heads):
        for c in range(n_cb):
            rows = slice((h * n_cb + c) * nqc, (h * n_cb + c + 1) * nqc)
            for u in range(n_u):
                o_ref[h, c, u * nqc:(u + 1) * nqc, :] = (res[rows, u * nk:(u + 1) * nk]
                                                         + mask_ref[c, u * nqc:(u + 1) * nqc, :])


def _key_col0(c):
    return min(max(c * COL_BLOCK - WIN_COLS // 2, 0), GRID_W - KEY_COLS)


def _mixers_kernel(q_ref, k_ref, v_ref, bg_ref, cg_ref, xv_ref, qkg_ref, tbl_ref, cw_ref,
                   attn_ref, conv_ref, qn_s, kn_s, qc_s, kc_s, vc_s, pad_s, *, rows):
    seq = q_ref.shape[0]
    lane = lax.broadcasted_iota(jnp.int32, (1, LANES), 1)
    first = lane < HEAD_DIM

    def head_norm(x, g):
        xx = x * x
        s0 = jnp.sum(jnp.where(first, xx, 0.0), axis=-1, keepdims=True)
        s1 = jnp.sum(jnp.where(first, 0.0, xx), axis=-1, keepdims=True)
        ms = jnp.where(first, s0, s1) * (1.0 / HEAD_DIM)
        return x * lax.rsqrt(ms + RMS_EPS) * g

    qn_s[...] = head_norm(q_ref[...], qkg_ref[0:1, :]) * (1.0 / math.sqrt(HEAD_DIM))
    kn_s[...] = head_norm(k_ref[...], qkg_ref[1:2, :])

    n_cb = GRID_W // COL_BLOCK
    for c in range(n_cb):
        k0 = _key_col0(c)
        for r in range(rows):
            qp = qn_s[r * GRID_W + c * COL_BLOCK:r * GRID_W + (c + 1) * COL_BLOCK, :]
            qc_s[0, c, r * COL_BLOCK:(r + 1) * COL_BLOCK, :] = jnp.where(first, qp, 0.0).astype(BF16)
            qc_s[1, c, r * COL_BLOCK:(r + 1) * COL_BLOCK, :] = jnp.where(first, 0.0, qp).astype(BF16)
            keys = slice(r * GRID_W + k0, r * GRID_W + k0 + KEY_COLS)
            dst = slice(r * KEY_COLS, (r + 1) * KEY_COLS)
            kc_s[c, dst, :] = kn_s[keys, :].astype(BF16)
            vp = v_ref[keys, :]
            vc_s[0, c, dst, :] = jnp.where(first, vp, 1.0).astype(BF16)
            vc_s[1, c, dst, :] = jnp.where(first, 1.0, vp).astype(BF16)

    n_groups = rows // ROWS_PER_STEP
    nq = ROWS_PER_STEP * COL_BLOCK
    for g in range(n_groups):
        b0 = min(max(g * ROWS_PER_STEP - WIN_ROWS // 2, 0), rows - BAND_ROWS)
        case = 0 if g == 0 else (2 if g == n_groups - 1 else 1)
        patch = slice(b0 * KEY_COLS, (b0 + BAND_ROWS) * KEY_COLS)
        blocks = [(c, h) for c in range(n_cb) for h in range(2)]
        s = jnp.concatenate(
            [lax.dot_general(qc_s[h, c, g * nq:(g + 1) * nq, :], kc_s[c, patch, :], _NT,
                             preferred_element_type=F32) + tbl_ref[h, case, c] for c, h in blocks], axis=0)
        p = jnp.exp(s - jnp.max(s, axis=-1, keepdims=True)).astype(BF16)
        pv = jnp.concatenate(
            [jnp.dot(p[i * nq:(i + 1) * nq, :], vc_s[h, c, patch, :], preferred_element_type=F32)
             for i, (c, h) in enumerate(blocks)], axis=0)
        o = pv / pltpu.roll(pv, HEAD_DIM, axis=1)
        for c in range(n_cb):
            out = jnp.where(first, o[2 * c * nq:(2 * c + 1) * nq, :], o[(2 * c + 1) * nq:(2 * c + 2) * nq, :])
            for u in range(ROWS_PER_STEP):
                row0 = (g * ROWS_PER_STEP + u) * GRID_W + c * COL_BLOCK
                attn_ref[row0:row0 + COL_BLOCK, :] = out[u * COL_BLOCK:(u + 1) * COL_BLOCK, :].astype(BF16)

    zeros = jnp.zeros((8, LANES), F32)
    pad_s[0:8, :] = zeros
    pad_s[seq + 8:seq + 16, :] = zeros
    pad_s[8:seq + 8, :] = cg_ref[...] * xv_ref[...]
    z = (pad_s[7:seq + 7, :] * cw_ref[0:1, :] + pad_s[8:seq + 8, :] * cw_ref[1:2, :]
         + pad_s[9:seq + 9, :] * cw_ref[2:3, :])
    conv_ref[...] = (bg_ref[...] * z).astype(BF16)


def _mixers(proj3, qk_g, tbl, layer, conv_w):
    b, seq, n = proj3.shape
    width = n // 6
    nblk = width // LANES
    rows = seq // GRID_W
    assert rows >= BAND_ROWS + ROWS_PER_STEP and rows % ROWS_PER_STEP == 0
    assert LANES == 2 * HEAD_DIM and GRID_W % COL_BLOCK == 0
    qkg = jnp.tile(qk_g, (1, LANES // HEAD_DIM))
    n_cb = GRID_W // COL_BLOCK

    def col(k):
        return pl.BlockSpec((None, seq, LANES), lambda i, j, k=k: (i, 0, k * nblk + j))

    out_spec = pl.BlockSpec((None, seq, LANES), lambda i, j: (i, 0, j))
    hp = LANES // HEAD_DIM
    return pl.pallas_call(
        functools.partial(_mixers_kernel, rows=rows),
        grid=(b, nblk),
        in_specs=[col(0), col(1), col(2), col(3), col(4), col(5),
                  pl.BlockSpec((2, LANES), lambda i, j: (0, 0)),
                  pl.BlockSpec((hp,) + tbl.shape[1:], lambda i, j: (layer * nblk + j, 0, 0, 0, 0)),
                  pl.BlockSpec((conv_w.shape[0], LANES), lambda i, j: (0, j))],
        out_specs=[out_spec, out_spec],
        out_shape=[jax.ShapeDtypeStruct((b, seq, width), BF16)] * 2,
        scratch_shapes=[pltpu.VMEM((seq, LANES), F32), pltpu.VMEM((seq, LANES), F32),
                        pltpu.VMEM((hp, n_cb, rows * COL_BLOCK, LANES), BF16),
                        pltpu.VMEM((n_cb, rows * KEY_COLS, LANES), BF16),
                        pltpu.VMEM((hp, n_cb, rows * KEY_COLS, LANES), BF16),
                        pltpu.VMEM((seq + 16, LANES), F32)],
        compiler_params=_params("parallel", "parallel"),
        name="attn_conv",
    )(proj3, proj3, proj3, proj3, proj3, proj3, qkg, tbl, conv_w)


def _route(x, g_ref, sc_ref, sh_ref, rw_ref, h_ref, aff_ref):
    h = _ln_mod(x, g_ref[...], sc_ref[...], sh_ref[...])
    tm, d = h.shape
    nch = d // LANES
    n_exp = aff_ref.shape[0]
    for j in range(nch):
        h_ref[pl.ds(j, tm, stride=nch), :] = h[:, j * LANES:(j + 1) * LANES]
    h0 = h.astype(BF16)
    h1 = (h - h0.astype(F32)).astype(BF16)
    lg = jnp.dot(h0, rw_ref[...], preferred_element_type=F32)
    lg = (lg[:, :LANES] + lg[:, LANES:]) + jnp.dot(h1, rw_ref[:, :LANES], preferred_element_type=F32)
    lane = lax.broadcasted_iota(jnp.int32, (1, LANES), 1)
    lg = jnp.where(lane < n_exp, lg, NEG_BIAS)
    p = jnp.exp(lg - jnp.max(lg, axis=-1, keepdims=True))
    aff = p / jnp.sum(p, axis=-1, keepdims=True)
    aff_ref[...] = aff.T[:n_exp, :]


def _router_weights(router_w):
    w = jnp.pad(router_w, ((0, 0), (0, LANES - router_w.shape[1])))
    w0 = w.astype(BF16)
    w1 = (w - w0.astype(F32)).astype(BF16)
    return jnp.concatenate([w0, w1], axis=1)


def _route_specs(b, seq, d, e, tm):
    nch = d // LANES
    in_specs = [pl.BlockSpec((1, d), lambda i, j: (0, 0)),
                pl.BlockSpec((None, None, 1, d), lambda i, j: (i, 4, 0, 0)),
                pl.BlockSpec((None, None, 1, d), lambda i, j: (i, 3, 0, 0)),
                pl.BlockSpec((d, 2 * LANES), lambda i, j: (0, 0))]
    out_specs = [pl.BlockSpec((None, tm * nch, LANES), lambda i, j: (i, j, 0)),
                 pl.BlockSpec((None, e, tm), lambda i, j: (i, 0, j))]
    out_shape = [jax.ShapeDtypeStruct((b, seq * nch, LANES), F32),
                 jax.ShapeDtypeStruct((b, e, seq), F32)]
    return in_specs, out_specs, out_shape


def _outproj_kernel(a_ref, c_ref, wa_ref, wc_ref, x_ref, gate_ref, g2_ref, sc2_ref, sh2_ref, rw_ref,
                    o_ref, h_ref, aff_ref):
    m = jnp.dot(a_ref[...], wa_ref[...], preferred_element_type=F32)
    m = m + jnp.dot(c_ref[...], wc_ref[...], preferred_element_type=F32)
    x = x_ref[...] + gate_ref[...] * m
    o_ref[...] = x
    _route(x, g2_ref, sc2_ref, sh2_ref, rw_ref, h_ref, aff_ref)


def _out_projection(attn3, conv3, w_bf, x3, modl, g2, rw01, n_exp, tm=512):
    b, seq, d = x3.shape
    wa = attn3.shape[2]
    wc = conv3.shape[2]
    r_in, r_out, r_shape = _route_specs(b, seq, d, n_exp, tm)
    tile = pl.BlockSpec((None, tm, d), lambda i, j: (i, j, 0))
    return pl.pallas_call(
        _outproj_kernel,
        grid=(b, seq // tm),
        in_specs=[
            pl.BlockSpec((None, tm, wa), lambda i, j: (i, j, 0)),
            pl.BlockSpec((None, tm, wc), lambda i, j: (i, j, 0)),
            pl.BlockSpec((wa, d), lambda i, j: (0, 0)),
            pl.BlockSpec((wc, d), lambda i, j: (1, 0)),
            tile,
            pl.BlockSpec((None, None, 1, d), lambda i, j: (i, 2, 0, 0)),
        ] + r_in,
        out_specs=[tile] + r_out,
        out_shape=[jax.ShapeDtypeStruct((b, seq, d), F32)] + r_shape,
        compiler_params=_params("parallel", "parallel"),
        name="out_proj",
    )(attn3, conv3, w_bf, w_bf, x3, modl, g2, modl, modl, rw01)


def _dft_tables(n):
    jk = (np.arange(n)[:, None] * np.arange(n)[None, :]) % n
    ang = 2.0 * np.pi * jk.astype(np.float64) / n
    return np.cos(ang).astype(np.float32), np.sin(ang).astype(np.float32)


def _fnet_chan_kernel(x_ref, g_ref, sc_ref, sh_ref, cs_ref, a_ref, b_ref, *, gw):
    h = _ln_mod(x_ref[...], g_ref[...], sc_ref[...], sh_ref[...]).astype(BF16)
    for g in range(h.shape[1] // gw):
        ab = jnp.dot(h[:, g * gw:(g + 1) * gw], cs_ref[...], preferred_element_type=F32)
        a_ref[:, g * gw:(g + 1) * gw] = ab[:, :gw].astype(BF16)
        b_ref[:, g * gw:(g + 1) * gw] = ab[:, gw:].astype(BF16)


def _fnet_seq_kernel(cs_ref, ss_ref, a_ref, b_ref, w_ref, x_ref, gate_ref, g2_ref, sc2_ref, sh2_ref,
                     rw_ref, o_ref, h_ref, aff_ref, *, norm):
    f = jnp.dot(cs_ref[...], a_ref[...], preferred_element_type=F32)
    f = f - jnp.dot(ss_ref[...], b_ref[...], preferred_element_type=F32)
    m = jnp.dot((f * norm).astype(BF16), w_ref[...], preferred_element_type=F32)
    x = x_ref[...] + gate_ref[...] * m
    o_ref[...] = x
    _route(x, g2_ref, sc2_ref, sh2_ref, rw_ref, h_ref, aff_ref)


def _fnet_layer(x3, g, modl, w_bf, g2, rw01, n_exp, tm=512):
    b, seq, d = x3.shape
    gw = d // FOURIER_GROUPS
    cc, sc = _dft_tables(gw)
    cs_chan = jnp.concatenate([jnp.asarray(cc), jnp.asarray(sc)], axis=1).astype(BF16)
    cseq, sseq = _dft_tables(seq)
    cseq = jnp.asarray(cseq).astype(BF16)
    sseq = jnp.asarray(sseq).astype(BF16)
    x2 = x3.reshape(b * seq, d)
    tps = seq // tm
    a, bm = pl.pallas_call(
        functools.partial(_fnet_chan_kernel, gw=gw),
        grid=(b * seq // tm,),
        in_specs=[
            pl.BlockSpec((tm, d), lambda i: (i, 0)),
            pl.BlockSpec((1, d), lambda i: (0, 0)),
            _row_spec(d, tps, 1),
            _row_spec(d, tps, 0),
            pl.BlockSpec((gw, 2 * gw), lambda i: (0, 0)),
        ],
        out_specs=[pl.BlockSpec((tm, d), lambda i: (i, 0))] * 2,
        out_shape=[jax.ShapeDtypeStruct((b * seq, d), BF16)] * 2,
        compiler_params=_params("parallel"),
        name="fnet_chan",
    )(x2, g, modl, modl, cs_chan)
    a3 = a.reshape(b, seq, d)
    b3 = bm.reshape(b, seq, d)
    norm = 1.0 / math.sqrt(seq * gw)
    r_in, r_out, r_shape = _route_specs(b, seq, d, n_exp, tm)
    tile = pl.BlockSpec((None, tm, d), lambda i, j: (i, j, 0))
    return pl.pallas_call(
        functools.partial(_fnet_seq_kernel, norm=norm),
        grid=(b, seq // tm),
        in_specs=[
            pl.BlockSpec((tm, seq), lambda i, j: (j, 0)),
            pl.BlockSpec((tm, seq), lambda i, j: (j, 0)),
            pl.BlockSpec((None, seq, d), lambda i, j: (i, 0, 0)),
            pl.BlockSpec((None, seq, d), lambda i, j: (i, 0, 0)),
            pl.BlockSpec((d, d), lambda i, j: (0, 0)),
            tile,
            pl.BlockSpec((None, None, 1, d), lambda i, j: (i, 2, 0, 0)),
        ] + r_in,
        out_specs=[tile] + r_out,
        out_shape=[jax.ShapeDtypeStruct((b, seq, d), F32)] + r_shape,
        compiler_params=_params("parallel", "parallel"),
        name="fnet_seq",
    )(cseq, sseq, a3, b3, w_bf, x3, modl, g2, modl, modl, rw01)


def _topk_kernel(aff_ref, tok_ref, gs_ref, tri_s, *, cap, row_scale):
    n_exp, seq = aff_ref.shape

    @pl.when(pl.program_id(0) == 0)
    def _():
        r = lax.broadcasted_iota(jnp.int32, (seq, seq), 0)
        c = lax.broadcasted_iota(jnp.int32, (seq, seq), 1)
        tri_s[...] = jnp.where(r < c, 1.0, 0.0).astype(BF16)

    a = aff_ref[...]

    def enough(cand):
        cnt = jnp.sum((a >= pltpu.bitcast(cand, F32)).astype(jnp.int32), axis=1, keepdims=True)
        return cnt >= cap

    def search(i, t):
        hi = lax.shift_left(jnp.int32(1), 29 - 2 * i)
        lo = lax.shift_left(jnp.int32(1), 28 - 2 * i)
        ok_hi, ok_lo, ok_both = enough(t | hi), enough(t | lo), enough(t | hi | lo)
        return jnp.where(ok_hi, jnp.where(ok_both, t | hi | lo, t | hi), jnp.where(ok_lo, t | lo, t))

    top = jnp.full((a.shape[0], 1), 1 << 30, jnp.int32)
    thr = lax.fori_loop(0, 15, search, jnp.where(enough(top), top, 0))
    thr = pltpu.bitcast(thr, F32)
    gt = a > thr
    eq = a == thr
    need = cap - jnp.sum(gt.astype(jnp.int32), axis=1, keepdims=True)
    tri = tri_s[...]
    eq_before = jnp.dot(jnp.where(eq, 1.0, 0.0).astype(BF16), tri, preferred_element_type=F32)
    sel = gt | (eq & (eq_before < need.astype(F32)))
    before = jnp.dot(jnp.where(sel, 1.0, 0.0).astype(BF16), tri, preferred_element_type=F32)
    rank = jnp.where(sel, before.astype(jnp.int32), -1)

    digit = lax.broadcasted_iota(jnp.int32, (SLOT_RADIX, seq), 0)
    tok_ids = lax.broadcasted_iota(jnp.int32, (1, seq), 1)
    t_hi = lax.shift_right_logical(tok_ids, 6).astype(F32)
    t_lo = (tok_ids & 63).astype(F32)
    for e in range(n_exp):
        r = rank[e:e + 1, :]
        hi = jnp.where(digit == lax.shift_right_arithmetic(r, SLOT_SHIFT), 1.0, 0.0).astype(BF16)
        lo = digit == (r & (SLOT_RADIX - 1))
        g = a[e:e + 1, :]
        g0 = g.astype(BF16).astype(F32)
        g1 = (g - g0).astype(BF16).astype(F32)
        g2 = (g - g0) - g1
        vals = jnp.concatenate([jnp.where(lo, v, 0.0) for v in (t_hi, t_lo, g0, g1, g2)], axis=0)
        res = lax.dot_general(hi, vals.astype(BF16), _NT, preferred_element_type=F32)
        part = [res[:, k * SLOT_RADIX:(k + 1) * SLOT_RADIX] for k in range(5)]
        tok_ref[e] = ((part[0] * 64.0 + part[1]) * float(row_scale)).astype(jnp.int32)
        gs_ref[e] = (part[2] + part[3]) + part[4]


def _topk(aff, cap, row_scale):
    b, e, seq = aff.shape
    n_hi = cap // SLOT_RADIX
    assert cap % SLOT_RADIX == 0 and n_hi <= SLOT_RADIX and seq <= 64 * 64
    out_spec = pl.BlockSpec((None, e, SLOT_RADIX, SLOT_RADIX), lambda i: (i, 0, 0, 0))
    tok, gs = pl.pallas_call(
        functools.partial(_topk_kernel, cap=cap, row_scale=row_scale),
        grid=(b,),
        in_specs=[pl.BlockSpec((None, e, seq), lambda i: (i, 0, 0))],
        out_specs=[out_spec, out_spec],
        out_shape=[jax.ShapeDtypeStruct((b, e, SLOT_RADIX, SLOT_RADIX), jnp.int32),
                   jax.ShapeDtypeStruct((b, e, SLOT_RADIX, SLOT_RADIX), F32)],
        scratch_shapes=[pltpu.VMEM((seq, seq), BF16)],
        compiler_params=_params("arbitrary"),
        name="topk",
    )(aff)
    return tok[:, :, :n_hi].reshape(b, e, cap), gs[:, :, :n_hi].reshape(b, e, cap)


def _gather_kernel(tok_ref, h_ref, o_ref, x_s, *, stride):
    n_exp, cap = tok_ref.shape
    nch = x_s.shape[0] // stride

    def expert(e, carry):
        for i in range(cap):
            t = pl.multiple_of(tok_ref[e, i], nch)
            x_s[pl.ds(i, nch, stride=stride), :] = h_ref[pl.ds(t, nch), :]
        xe = jnp.concatenate([x_s[j * stride:j * stride + cap, :] for j in range(nch)], axis=-1)
        o_ref[e] = xe.astype(BF16)
        return carry

    lax.fori_loop(0, n_exp, expert, 0)


def _gather_tokens(h_slab, tok, d, stride):
    b, rows, _ = h_slab.shape
    _, e, cap = tok.shape
    nch = d // LANES
    return pl.pallas_call(
        functools.partial(_gather_kernel, stride=stride),
        grid=(b,),
        in_specs=[
            pl.BlockSpec((None, e, cap), lambda i: (i, 0, 0), memory_space=pltpu.SMEM),
            pl.BlockSpec((None, rows, LANES), lambda i: (i, 0, 0)),
        ],
        out_specs=pl.BlockSpec((e, None, cap, d), lambda i: (0, i, 0, 0)),
        out_shape=jax.ShapeDtypeStruct((e, b, cap, d), BF16),
        scratch_shapes=[pltpu.VMEM((nch * stride, LANES), F32)],
        compiler_params=_params("parallel"),
        name="moe_gather",
    )(tok, h_slab)


def _ffn_kernel(x_ref, gs_ref, wg_ref, wu_ref, wd_ref, y_ref, wg_s, wu_s, wd_s, *, cap, ff_chunk):
    @pl.when(pl.program_id(1) == 0)
    def _():
        wg_s[...] = wg_ref[...].astype(BF16)
        wu_s[...] = wu_ref[...].astype(BF16)
        wd_s[...] = wd_ref[...].astype(BF16)

    x = x_ref[...]
    y = None
    for c in range(wg_s.shape[1] // ff_chunk):
        cols = slice(c * ff_chunk, (c + 1) * ff_chunk)
        a = jnp.dot(x, wg_s[:, cols], preferred_element_type=F32)
        u = jnp.dot(x, wu_s[:, cols], preferred_element_type=F32)
        hm = (a * jax.nn.sigmoid(a) * u).astype(BF16)
        yc = jnp.dot(hm, wd_s[cols, :], preferred_element_type=F32)
        y = yc if y is None else y + yc
    g = jnp.broadcast_to(gs_ref[...], (LANES, x.shape[0])).T
    bpb, nch, stride, _ = y_ref.shape
    for bb in range(bpb):
        rows = slice(bb * cap, (bb + 1) * cap)
        for j in range(nch):
            y_ref[bb, j, 0:cap, :] = y[rows, j * LANES:(j + 1) * LANES] * g[rows, :]
        y_ref[bb, :, cap:, :] = jnp.zeros((nch, stride - cap, LANES), F32)


def _expert_ffn(xe, gs_col, layer, wg, wu, wd, cap, stride, tm):
    e, m, d = xe.shape
    ff = wg.shape[3]
    nch = d // LANES
    bpb = tm // cap
    return pl.pallas_call(
        functools.partial(_ffn_kernel, cap=cap, ff_chunk=min(512, ff)),
        grid=(e, m // tm),
        in_specs=[
            pl.BlockSpec((None, tm, d), lambda i, j: (i, j, 0)),
            pl.BlockSpec((None, 1, tm), lambda i, j: (i, 0, j)),
            pl.BlockSpec((None, None, d, ff), lambda i, j: (layer, i, 0, 0)),
            pl.BlockSpec((None, None, d, ff), lambda i, j: (layer, i, 0, 0)),
            pl.BlockSpec((None, None, ff, d), lambda i, j: (layer, i, 0, 0)),
        ],
        out_specs=pl.BlockSpec((None, bpb, nch, stride, LANES), lambda i, j: (i, j, 0, 0, 0)),
        out_shape=jax.ShapeDtypeStruct((e, m // cap, nch, stride, LANES), F32),
        scratch_shapes=[pltpu.VMEM((d, ff), BF16), pltpu.VMEM((d, ff), BF16), pltpu.VMEM((ff, d), BF16)],
        compiler_params=_params("parallel", "arbitrary"),
        name="moe_ffn",
    )(xe, gs_col, wg, wu, wd)


def _scatter_kernel(tok_ref, y_ref, x_ref, gate_ref, o_ref, acc_s, *, stride, n_steps):
    group, _, cap = tok_ref.shape
    ts, d = x_ref.shape
    nch = d // LANES
    k = pl.program_id(1)

    @pl.when(k == 0)
    def _():
        acc_s[...] = jnp.zeros_like(acc_s)

    @pl.when(k < n_steps)
    def _():
        def expert(ee, carry):
            def slab(i):
                return pl.ds(pl.multiple_of(tok_ref[ee, 0, i], nch), nch)

            for i0 in range(0, cap, SCATTER_BATCH):
                vals = [acc_s[slab(i), :] + y_ref[ee, pl.ds(i, nch, stride=stride), :]
                        for i in range(i0, i0 + SCATTER_BATCH)]
                for i, v in zip(range(i0, i0 + SCATTER_BATCH), vals):
                    acc_s[slab(i), :] = v
            return carry

        lax.fori_loop(0, group, expert, 0)

    @pl.when(k >= n_steps)
    def _():
        base = pl.multiple_of((k - n_steps) * (ts * nch), ts * nch)
        for j in range(nch):
            lanes = slice(j * LANES, (j + 1) * LANES)
            o_ref[:, lanes] = (x_ref[:, lanes]
                               + gate_ref[:, lanes] * acc_s[pl.ds(base + j, ts, stride=nch), :])


def _scatter_residual(y_cm, tok, x3, modl, stride):
    e, b, _, _ = y_cm.shape
    _, seq, d = x3.shape
    cap = tok.shape[-1]
    nch = d // LANES
    assert cap % SCATTER_BATCH == 0
    ts = min(512, seq)
    group = SCATTER_EXPERTS
    assert e % group == 0
    n_steps = e // group
    smem_rows = pl.BlockSpec((None, group, 1, cap), lambda i, k: (i, jnp.minimum(k, n_steps - 1), 0, 0),
                             memory_space=pltpu.SMEM)
    tile = pl.BlockSpec((None, ts, d), lambda i, k: (i, jnp.maximum(k - n_steps, 0), 0))
    return pl.pallas_call(
        functools.partial(_scatter_kernel, stride=stride, n_steps=n_steps),
        grid=(b, n_steps + seq // ts),
        in_specs=[
            smem_rows,
            pl.BlockSpec((group, None, nch * stride, LANES),
                         lambda i, k: (jnp.minimum(k, n_steps - 1), i, 0, 0)),
            tile,
            pl.BlockSpec((None, None, 1, d), lambda i, k: (i, 5, 0, 0)),
        ],
        out_specs=tile,
        out_shape=jax.ShapeDtypeStruct((b, seq, d), F32),
        scratch_shapes=[pltpu.VMEM((seq * nch, LANES), F32)],
        compiler_params=_params("parallel", "arbitrary"),
        name="moe_scatter",
    )(tok.reshape(b, e, 1, cap), y_cm, x3, modl)


def _moe_layer(x3, h_slab, aff, modl, layer, wg, wu, wd):
    b, seq, d = x3.shape
    e = aff.shape[1]
    cap = max(1, CAPACITY_FACTOR * seq // e)
    stride = cap + 8
    tok, gs = _topk(aff, cap, d // LANES)
    xe = _gather_tokens(h_slab, tok, d, stride)
    gs_col = gs.transpose(1, 0, 2).reshape(e, 1, b * cap)
    y = _expert_ffn(xe.reshape(e, b * cap, d), gs_col, layer, wg, wu, wd, cap, stride,
                    tm=min(1024, b * cap))
    return _scatter_residual(y.reshape(e, b, -1, LANES), tok, x3, modl, stride)


def kernel(x, c, mod_w, mod_b, norm_g, mix_w_in, qk_g, rpb, conv_w, mix_w_out, fnet_w_out,
           router_w, exp_w_gate, exp_w_up, exp_w_down):
    b, seq, d = x.shape
    depth = mod_w.shape[0]
    mod = _modulation(c, mod_w, mod_b).reshape(depth, b, N_MOD, 1, d)
    tbl = _bias_table(rpb.reshape((-1,) + rpb.shape[2:]))
    for l in range(depth):
        modl = mod[l]
        g1 = norm_g[l, 0].reshape(1, d)
        g2 = norm_g[l, 1].reshape(1, d)
        rw01 = _router_weights(router_w[l])
        n_exp = router_w.shape[2]
        if l % 2 == 0:
            j = l // 2
            proj = _in_projection(x.reshape(b * seq, d), g1, modl, mix_w_in[j].astype(BF16), seq)
            attn, conv = _mixers(proj.reshape(b, seq, -1), qk_g[j], tbl, j, conv_w[j])
            x, h_slab, aff = _out_projection(attn, conv, mix_w_out[j].astype(BF16), x, modl,
                                             g2, rw01, n_exp)
        else:
            x, h_slab, aff = _fnet_layer(x, g1, modl, fnet_w_out[l // 2].astype(BF16), g2, rw01, n_exp)
        x = _moe_layer(x, h_slab, aff, modl, l, exp_w_gate, exp_w_up, exp_w_down)
    return x
```

```python
import functools
import math

import numpy as np
import jax
import jax.numpy as jnp
from jax import lax
from jax.experimental import pallas as pl
from jax.experimental.pallas import tpu as pltpu

F32 = jnp.float32
BF16 = jnp.bfloat16
HIGHEST = lax.Precision.HIGHEST

GRID_W = 64
WIN_ROWS = 8
WIN_COLS = 16
HEAD_DIM = 64
FOURIER_GROUPS = 4
N_MOD = 6
CAPACITY_FACTOR = 2
RMS_EPS = 1e-6
NEG_BIAS = -1e30
LANES = 128
ROWS_PER_STEP = WIN_ROWS // 2
BAND_ROWS = WIN_ROWS + ROWS_PER_STEP
COL_BLOCK = WIN_COLS
KEY_COLS = 2 * WIN_COLS
SCATTER_BATCH = 8
SCATTER_EXPERTS = 4
SLOT_SHIFT = 4
SLOT_RADIX = 1 << SLOT_SHIFT
VMEM_LIMIT = 56 * 1024 * 1024

_NT = (((1,), (1,)), ((), ()))
_TN = (((0,), (0,)), ((), ()))


def _params(*sem):
    return pltpu.CompilerParams(dimension_semantics=sem, vmem_limit_bytes=VMEM_LIMIT)


def _ln_mod(x, g, sc, sh):
    y = x * lax.rsqrt(jnp.mean(x * x, axis=-1, keepdims=True) + RMS_EPS)
    return (y * g) * (1.0 + sc) + sh


def _mod_kernel(c_ref, w_ref, b_ref, o_ref):
    c = c_ref[...]
    s = c * jax.nn.sigmoid(c)
    o_ref[...] = jnp.dot(s, w_ref[...], precision=HIGHEST, preferred_element_type=F32) + b_ref[...]


def _modulation(c, mod_w, mod_b):
    depth, d, n = mod_w.shape
    b = c.shape[0]
    tn = 1536
    return pl.pallas_call(
        _mod_kernel,
        grid=(depth, n // tn),
        in_specs=[
            pl.BlockSpec((b, d), lambda l, j: (0, 0)),
            pl.BlockSpec((None, d, tn), lambda l, j: (l, 0, j)),
            pl.BlockSpec((None, 1, tn), lambda l, j: (l, 0, j)),
        ],
        out_specs=pl.BlockSpec((None, b, tn), lambda l, j: (l, 0, j)),
        out_shape=jax.ShapeDtypeStruct((depth, b, n), F32),
        compiler_params=_params("parallel", "parallel"),
        name="adaln_mod",
    )(c, mod_w, mod_b.reshape(depth, 1, n))


def _row_spec(d, tiles_per_seq, k):
    return pl.BlockSpec((None, None, 1, d), lambda i: (i // tiles_per_seq, k, 0, 0))


def _inproj_kernel(x_ref, g_ref, sc_ref, sh_ref, w_ref, o_ref):
    h = _ln_mod(x_ref[...], g_ref[...], sc_ref[...], sh_ref[...]).astype(BF16)
    o_ref[...] = jnp.dot(h, w_ref[...], preferred_element_type=F32)


def _in_projection(x2, g, modl, w_bf, seq, tm=1024):
    t, d = x2.shape
    n = w_bf.shape[1]
    tps = seq // tm
    return pl.pallas_call(
        _inproj_kernel,
        grid=(t // tm,),
        in_specs=[
            pl.BlockSpec((tm, d), lambda i: (i, 0)),
            pl.BlockSpec((1, d), lambda i: (0, 0)),
            _row_spec(d, tps, 1),
            _row_spec(d, tps, 0),
            pl.BlockSpec((d, n), lambda i: (0, 0)),
        ],
        out_specs=pl.BlockSpec((tm, n), lambda i: (i, 0)),
        out_shape=jax.ShapeDtypeStruct((t, n), F32),
        compiler_params=_params("parallel"),
        name="in_proj",
    )(x2, g, modl, modl, w_bf)


def _bias_table(rpb):
    heads, nr, nc = rpb.shape
    assert nr == 2 * WIN_ROWS - 1 and nc == 2 * WIN_COLS - 1
    n_cb = GRID_W // COL_BLOCK
    qcol = (np.arange(n_cb) * COL_BLOCK)[:, None, None] + np.arange(COL_BLOCK)[None, :, None]
    kcol = np.array([_key_col0(c) for c in range(n_cb)])[:, None, None] + np.arange(KEY_COLS)[None, None, :]
    cs = np.clip(qcol - WIN_COLS // 2, 0, GRID_W - WIN_COLS)
    col_ok = (kcol >= cs) & (kcol < cs + WIN_COLS)
    col_sel = col_ok[..., None] & (np.arange(nc) == (kcol - qcol + WIN_COLS - 1)[..., None])
    place = [[(u, 0) for u in range(ROWS_PER_STEP)],
             [(WIN_ROWS // 2 + u, u) for u in range(ROWS_PER_STEP)],
             [(BAND_ROWS - ROWS_PER_STEP + u, BAND_ROWS - WIN_ROWS) for u in range(ROWS_PER_STEP)]]
    row_sel = np.zeros((len(place), ROWS_PER_STEP, BAND_ROWS, nr), bool)
    for x, per_u in enumerate(place):
        for u, (off, lo) in enumerate(per_u):
            for i in range(lo, lo + WIN_ROWS):
                row_sel[x, u, i, i - off + WIN_ROWS - 1] = True
    row_ok = row_sel.any(axis=-1)
    n_case, nq, nk = len(place), ROWS_PER_STEP * COL_BLOCK, BAND_ROWS * KEY_COLS
    cols = jnp.einsum('hab,cqkb->hcqak', rpb, col_sel.astype(np.float32), precision=HIGHEST)
    depth = -(-nr * KEY_COLS // LANES) * LANES
    cols = jnp.pad(cols.reshape(heads * n_cb * COL_BLOCK, nr * KEY_COLS), ((0, 0), (0, depth - nr * KEY_COLS)))
    sel = row_sel.transpose(0, 3, 1, 2)[:, :, None, :, :, None] & np.eye(KEY_COLS, dtype=bool)[None, None, :, None, None, :]
    sel = np.pad(sel.reshape(n_case, nr * KEY_COLS, ROWS_PER_STEP * nk), ((0, 0), (0, depth - nr * KEY_COLS), (0, 0)))
    ok = row_ok[:, None, :, None, :, None] & col_ok[None, :, None, :, None, :]
    mask = np.where(ok, 0.0, NEG_BIAS).astype(np.float32).reshape(n_case, n_cb, nq, nk)
    return pl.pallas_call(
        functools.partial(_bias_kernel, n_u=ROWS_PER_STEP),
        grid=(n_case,),
        in_specs=[pl.BlockSpec(cols.shape, lambda x: (0, 0)),
                  pl.BlockSpec((None,) + sel.shape[1:], lambda x: (x, 0, 0)),
                  pl.BlockSpec((None, n_cb, nq, nk), lambda x: (x, 0, 0, 0))],
        out_specs=pl.BlockSpec((heads, None, n_cb, nq, nk), lambda x: (0, x, 0, 0, 0)),
        out_shape=jax.ShapeDtypeStruct((heads, n_case, n_cb, nq, nk), F32),
        compiler_params=_params("parallel"),
        name="bias_table",
    )(cols, jnp.asarray(sel, BF16), jnp.asarray(mask))


def _bias_kernel(cols_ref, sel_ref, mask_ref, o_ref, *, n_u):
    heads, n_cb, nq, nk = o_ref.shape
    nqc = nq // n_u
    t = cols_ref[...]
    t0 = t.astype(BF16)
    r = t - t0.astype(F32)
    t1 = r.astype(BF16)
    t2 = (r - t1.astype(F32)).astype(BF16)
    sel = sel_ref[...]
    res = (jnp.dot(t0, sel, preferred_element_type=F32) + jnp.dot(t1, sel, preferred_element_type=F32)
           + jnp.dot(t2, sel, preferred_element_type=F32))
    for h in range(heads):
        for c in range(n_cb):
            rows = slice((h * n_cb + c) * nqc, (h * n_cb + c + 1) * nqc)
            for u in range(n_u):
                o_ref[h, c, u * nqc:(u + 1) * nqc, :] = (res[rows, u * nk:(u + 1) * nk]
                                                         + mask_ref[c, u * nqc:(u + 1) * nqc, :])


def _key_col0(c):
    return min(max(c * COL_BLOCK - WIN_COLS // 2, 0), GRID_W - KEY_COLS)


def _mixers_kernel(q_ref, k_ref, v_ref, bg_ref, cg_ref, xv_ref, qkg_ref, tbl_ref, cw_ref,
                   attn_ref, conv_ref, qn_s, kn_s, qc_s, kc_s, vc_s, pad_s, *, rows):
    seq = q_ref.shape[0]
    lane = lax.broadcasted_iota(jnp.int32, (1, LANES), 1)
    first = lane < HEAD_DIM

    def head_norm(x, g):
        xx = x * x
        s0 = jnp.sum(jnp.where(first, xx, 0.0), axis=-1, keepdims=True)
        s1 = jnp.sum(jnp.where(first, 0.0, xx), axis=-1, keepdims=True)
        ms = jnp.where(first, s0, s1) * (1.0 / HEAD_DIM)
        return x * lax.rsqrt(ms + RMS_EPS) * g

    qn_s[...] = head_norm(q_ref[...], qkg_ref[0:1, :]) * (1.0 / math.sqrt(HEAD_DIM))
    kn_s[...] = head_norm(k_ref[...], qkg_ref[1:2, :])

    n_cb = GRID_W // COL_BLOCK
    for c in range(n_cb):
        k0 = _key_col0(c)
        for r in range(rows):
            qp = qn_s[r * GRID_W + c * COL_BLOCK:r * GRID_W + (c + 1) * COL_BLOCK, :]
            qc_s[0, c, r * COL_BLOCK:(r + 1) * COL_BLOCK, :] = jnp.where(first, qp, 0.0).astype(BF16)
            qc_s[1, c, r * COL_BLOCK:(r + 1) * COL_BLOCK, :] = jnp.where(first, 0.0, qp).astype(BF16)
            keys = slice(r * GRID_W + k0, r * GRID_W + k0 + KEY_COLS)
            dst = slice(r * KEY_COLS, (r + 1) * KEY_COLS)
            kc_s[c, dst, :] = kn_s[keys, :].astype(BF16)
            vp = v_ref[keys, :]
            vc_s[0, c, dst, :] = jnp.where(first, vp, 1.0).astype(BF16)
            vc_s[1, c, dst, :] = jnp.where(first, 1.0, vp).astype(BF16)

    n_groups = rows // ROWS_PER_STEP
    nq = ROWS_PER_STEP * COL_BLOCK
    for g in range(n_groups):
        b0 = min(max(g * ROWS_PER_STEP - WIN_ROWS // 2, 0), rows - BAND_ROWS)
        case = 0 if g == 0 else (2 if g == n_groups - 1 else 1)
        patch = slice(b0 * KEY_COLS, (b0 + BAND_ROWS) * KEY_COLS)
        blocks = [(c, h) for c in range(n_cb) for h in range(2)]
        s = jnp.concatenate(
            [lax.dot_general(qc_s[h, c, g * nq:(g + 1) * nq, :], kc_s[c, patch, :], _NT,
                             preferred_element_type=F32) + tbl_ref[h, case, c] for c, h in blocks], axis=0)
        p = jnp.exp(s - jnp.max(s, axis=-1, keepdims=True)).astype(BF16)
        pv = jnp.concatenate(
            [jnp.dot(p[i * nq:(i + 1) * nq, :], vc_s[h, c, patch, :], preferred_element_type=F32)
             for i, (c, h) in enumerate(blocks)], axis=0)
        o = pv / pltpu.roll(pv, HEAD_DIM, axis=1)
        for c in range(n_cb):
            out = jnp.where(first, o[2 * c * nq:(2 * c + 1) * nq, :], o[(2 * c + 1) * nq:(2 * c + 2) * nq, :])
            for u in range(ROWS_PER_STEP):
                row0 = (g * ROWS_PER_STEP + u) * GRID_W + c * COL_BLOCK
                attn_ref[row0:row0 + COL_BLOCK, :] = out[u * COL_BLOCK:(u + 1) * COL_BLOCK, :].astype(BF16)

    zeros = jnp.zeros((8, LANES), F32)
    pad_s[0:8, :] = zeros
    pad_s[seq + 8:seq + 16, :] = zeros
    pad_s[8:seq + 8, :] = cg_ref[...] * xv_ref[...]
    z = (pad_s[7:seq + 7, :] * cw_ref[0:1, :] + pad_s[8:seq + 8, :] * cw_ref[1:2, :]
         + pad_s[9:seq + 9, :] * cw_ref[2:3, :])
    conv_ref[...] = (bg_ref[...] * z).astype(BF16)


def _mixers(proj3, qk_g, tbl, layer, conv_w):
    b, seq, n = proj3.shape
    width = n // 6
    nblk = width // LANES
    rows = seq // GRID_W
    assert rows >= BAND_ROWS + ROWS_PER_STEP and rows % ROWS_PER_STEP == 0
    assert LANES == 2 * HEAD_DIM and GRID_W % COL_BLOCK == 0
    qkg = jnp.tile(qk_g, (1, LANES // HEAD_DIM))
    n_cb = GRID_W // COL_BLOCK

    def col(k):
        return pl.BlockSpec((None, seq, LANES), lambda i, j, k=k: (i, 0, k * nblk + j))

    out_spec = pl.BlockSpec((None, seq, LANES), lambda i, j: (i, 0, j))
    hp = LANES // HEAD_DIM
    return pl.pallas_call(
        functools.partial(_mixers_kernel, rows=rows),
        grid=(b, nblk),
        in_specs=[col(0), col(1), col(2), col(3), col(4), col(5),
                  pl.BlockSpec((2, LANES), lambda i, j: (0, 0)),
                  pl.BlockSpec((hp,) + tbl.shape[1:], lambda i, j: (layer * nblk + j, 0, 0, 0, 0)),
                  pl.BlockSpec((conv_w.shape[0], LANES), lambda i, j: (0, j))],
        out_specs=[out_spec, out_spec],
        out_shape=[jax.ShapeDtypeStruct((b, seq, width), BF16)] * 2,
        scratch_shapes=[pltpu.VMEM((seq, LANES), F32), pltpu.VMEM((seq, LANES), F32),
                        pltpu.VMEM((hp, n_cb, rows * COL_BLOCK, LANES), BF16),
                        pltpu.VMEM((n_cb, rows * KEY_COLS, LANES), BF16),
                        pltpu.VMEM((hp, n_cb, rows * KEY_COLS, LANES), BF16),
                        pltpu.VMEM((seq + 16, LANES), F32)],
        compiler_params=_params("parallel", "parallel"),
        name="attn_conv",
    )(proj3, proj3, proj3, proj3, proj3, proj3, qkg, tbl, conv_w)


def _route(x, g_ref, sc_ref, sh_ref, rw_ref, h_ref, aff_ref):
    h = _ln_mod(x, g_ref[...], sc_ref[...], sh_ref[...])
    tm, d = h.shape
    nch = d // LANES
    n_exp = aff_ref.shape[0]
    for j in range(nch):
        h_ref[pl.ds(j, tm, stride=nch), :] = h[:, j * LANES:(j + 1) * LANES]
    h0 = h.astype(BF16)
    h1 = (h - h0.astype(F32)).astype(BF16)
    lg = jnp.dot(h0, rw_ref[...], preferred_element_type=F32)
    lg = (lg[:, :LANES] + lg[:, LANES:]) + jnp.dot(h1, rw_ref[:, :LANES], preferred_element_type=F32)
    lane = lax.broadcasted_iota(jnp.int32, (1, LANES), 1)
    lg = jnp.where(lane < n_exp, lg, NEG_BIAS)
    p = jnp.exp(lg - jnp.max(lg, axis=-1, keepdims=True))
    aff = p / jnp.sum(p, axis=-1, keepdims=True)
    aff_ref[...] = aff.T[:n_exp, :]


def _router_weights(router_w):
    w = jnp.pad(router_w, ((0, 0), (0, LANES - router_w.shape[1])))
    w0 = w.astype(BF16)
    w1 = (w - w0.astype(F32)).astype(BF16)
    return jnp.concatenate([w0, w1], axis=1)


def _route_specs(b, seq, d, e, tm):
    nch = d // LANES
    in_specs = [pl.BlockSpec((1, d), lambda i, j: (0, 0)),
                pl.BlockSpec((None, None, 1, d), lambda i, j: (i, 4, 0, 0)),
                pl.BlockSpec((None, None, 1, d), lambda i, j: (i, 3, 0, 0)),
                pl.BlockSpec((d, 2 * LANES), lambda i, j: (0, 0))]
    out_specs = [pl.BlockSpec((None, tm * nch, LANES), lambda i, j: (i, j, 0)),
                 pl.BlockSpec((None, e, tm), lambda i, j: (i, 0, j))]
    out_shape = [jax.ShapeDtypeStruct((b, seq * nch, LANES), F32),
                 jax.ShapeDtypeStruct((b, e, seq), F32)]
    return in_specs, out_specs, out_shape


def _outproj_kernel(a_ref, c_ref, wa_ref, wc_ref, x_ref, gate_ref, g2_ref, sc2_ref, sh2_ref, rw_ref,
                    o_ref, h_ref, aff_ref):
    m = jnp.dot(a_ref[...], wa_ref[...], preferred_element_type=F32)
    m = m + jnp.dot(c_ref[...], wc_ref[...], preferred_element_type=F32)
    x = x_ref[...] + gate_ref[...] * m
    o_ref[...] = x
    _route(x, g2_ref, sc2_ref, sh2_ref, rw_ref, h_ref, aff_ref)


def _out_projection(attn3, conv3, w_bf, x3, modl, g2, rw01, n_exp, tm=512):
    b, seq, d = x3.shape
    wa = attn3.shape[2]
    wc = conv3.shape[2]
    r_in, r_out, r_shape = _route_specs(b, seq, d, n_exp, tm)
    tile = pl.BlockSpec((None, tm, d), lambda i, j: (i, j, 0))
    return pl.pallas_call(
        _outproj_kernel,
        grid=(b, seq // tm),
        in_specs=[
            pl.BlockSpec((None, tm, wa), lambda i, j: (i, j, 0)),
            pl.BlockSpec((None, tm, wc), lambda i, j: (i, j, 0)),
            pl.BlockSpec((wa, d), lambda i, j: (0, 0)),
            pl.BlockSpec((wc, d), lambda i, j: (1, 0)),
            tile,
            pl.BlockSpec((None, None, 1, d), lambda i, j: (i, 2, 0, 0)),
        ] + r_in,
        out_specs=[tile] + r_out,
        out_shape=[jax.ShapeDtypeStruct((b, seq, d), F32)] + r_shape,
        compiler_params=_params("parallel", "parallel"),
        name="out_proj",
    )(attn3, conv3, w_bf, w_bf, x3, modl, g2, modl, modl, rw01)


def _dft_tables(n):
    jk = (np.arange(n)[:, None] * np.arange(n)[None, :]) % n
    ang = 2.0 * np.pi * jk.astype(np.float64) / n
    return np.cos(ang).astype(np.float32), np.sin(ang).astype(np.float32)


def _fnet_chan_kernel(x_ref, g_ref, sc_ref, sh_ref, cs_ref, a_ref, b_ref, *, gw):
    h = _ln_mod(x_ref[...], g_ref[...], sc_ref[...], sh_ref[...]).astype(BF16)
    for g in range(h.shape[1] // gw):
        ab = jnp.dot(h[:, g * gw:(g + 1) * gw], cs_ref[...], preferred_element_type=F32)
        a_ref[:, g * gw:(g + 1) * gw] = ab[:, :gw].astype(BF16)
        b_ref[:, g * gw:(g + 1) * gw] = ab[:, gw:].astype(BF16)


def _fnet_seq_kernel(cs_ref, ss_ref, a_ref, b_ref, w_ref, x_ref, gate_ref, g2_ref, sc2_ref, sh2_ref,
                     rw_ref, o_ref, h_ref, aff_ref, *, norm):
    f = jnp.dot(cs_ref[...], a_ref[...], preferred_element_type=F32)
    f = f - jnp.dot(ss_ref[...], b_ref[...], preferred_element_type=F32)
    m = jnp.dot((f * norm).astype(BF16), w_ref[...], preferred_element_type=F32)
    x = x_ref[...] + gate_ref[...] * m
    o_ref[...] = x
    _route(x, g2_ref, sc2_ref, sh2_ref, rw_ref, h_ref, aff_ref)


def _fnet_layer(x3, g, modl, w_bf, g2, rw01, n_exp, tm=512):
    b, seq, d = x3.shape
    gw = d // FOURIER_GROUPS
    cc, sc = _dft_tables(gw)
    cs_chan = jnp.concatenate([jnp.asarray(cc), jnp.asarray(sc)], axis=1).astype(BF16)
    cseq, sseq = _dft_tables(seq)
    cseq = jnp.asarray(cseq).astype(BF16)
    sseq = jnp.asarray(sseq).astype(BF16)
    x2 = x3.reshape(b * seq, d)
    tps = seq // tm
    a, bm = pl.pallas_call(
        functools.partial(_fnet_chan_kernel, gw=gw),
        grid=(b * seq // tm,),
        in_specs=[
            pl.BlockSpec((tm, d), lambda i: (i, 0)),
            pl.BlockSpec((1, d), lambda i: (0, 0)),
            _row_spec(d, tps, 1),
            _row_spec(d, tps, 0),
            pl.BlockSpec((gw, 2 * gw), lambda i: (0, 0)),
        ],
        out_specs=[pl.BlockSpec((tm, d), lambda i: (i, 0))] * 2,
        out_shape=[jax.ShapeDtypeStruct((b * seq, d), BF16)] * 2,
        compiler_params=_params("parallel"),
        name="fnet_chan",
    )(x2, g, modl, modl, cs_chan)
    a3 = a.reshape(b, seq, d)
    b3 = bm.reshape(b, seq, d)
    norm = 1.0 / math.sqrt(seq * gw)
    r_in, r_out, r_shape = _route_specs(b, seq, d, n_exp, tm)
    tile = pl.BlockSpec((None, tm, d), lambda i, j: (i, j, 0))
    return pl.pallas_call(
        functools.partial(_fnet_seq_kernel, norm=norm),
        grid=(b, seq // tm),
        in_specs=[
            pl.BlockSpec((tm, seq), lambda i, j: (j, 0)),
            pl.BlockSpec((tm, seq), lambda i, j: (j, 0)),
            pl.BlockSpec((None, seq, d), lambda i, j: (i, 0, 0)),
            pl.BlockSpec((None, seq, d), lambda i, j: (i, 0, 0)),
            pl.BlockSpec((d, d), lambda i, j: (0, 0)),
            tile,
            pl.BlockSpec((None, None, 1, d), lambda i, j: (i, 2, 0, 0)),
        ] + r_in,
        out_specs=[tile] + r_out,
        out_shape=[jax.ShapeDtypeStruct((b, seq, d), F32)] + r_shape,
        compiler_params=_params("parallel", "parallel"),
        name="fnet_seq",
    )(cseq, sseq, a3, b3, w_bf, x3, modl, g2, modl, modl, rw01)


def _topk_kernel(aff_ref, tok_ref, gs_ref, tri_s, *, cap, row_scale):
    n_exp, seq = aff_ref.shape

    @pl.when(pl.program_id(0) == 0)
    def _():
        r = lax.broadcasted_iota(jnp.int32, (seq, seq), 0)
        c = lax.broadcasted_iota(jnp.int32, (seq, seq), 1)
        tri_s[...] = jnp.where(r < c, 1.0, 0.0).astype(BF16)

    a = aff_ref[...]

    def enough(cand):
        cnt = jnp.sum((a >= pltpu.bitcast(cand, F32)).astype(jnp.int32), axis=1, keepdims=True)
        return cnt >= cap

    def search(i, t):
        hi = lax.shift_left(jnp.int32(1), 29 - 2 * i)
        lo = lax.shift_left(jnp.int32(1), 28 - 2 * i)
        ok_hi, ok_lo, ok_both = enough(t | hi), enough(t | lo), enough(t | hi | lo)
        return jnp.where(ok_hi, jnp.where(ok_both, t | hi | lo, t | hi), jnp.where(ok_lo, t | lo, t))

    top = jnp.full((a.shape[0], 1), 1 << 30, jnp.int32)
    thr = lax.fori_loop(0, 15, search, jnp.where(enough(top), top, 0))
    thr = pltpu.bitcast(thr, F32)
    gt = a > thr
    eq = a == thr
    need = cap - jnp.sum(gt.astype(jnp.int32), axis=1, keepdims=True)
    tri = tri_s[...]
    eq_before = jnp.dot(jnp.where(eq, 1.0, 0.0).astype(BF16), tri, preferred_element_type=F32)
    sel = gt | (eq & (eq_before < need.astype(F32)))
    before = jnp.dot(jnp.where(sel, 1.0, 0.0).astype(BF16), tri, preferred_element_type=F32)
    rank = jnp.where(sel, before.astype(jnp.int32), -1)

    digit = lax.broadcasted_iota(jnp.int32, (SLOT_RADIX, seq), 0)
    tok_ids = lax.broadcasted_iota(jnp.int32, (1, seq), 1)
    t_hi = lax.shift_right_logical(tok_ids, 6).astype(F32)
    t_lo = (tok_ids & 63).astype(F32)
    for e in range(n_exp):
        r = rank[e:e + 1, :]
        hi = jnp.where(digit == lax.shift_right_arithmetic(r, SLOT_SHIFT), 1.0, 0.0).astype(BF16)
        lo = digit == (r & (SLOT_RADIX - 1))
        g = a[e:e + 1, :]
        g0 = g.astype(BF16).astype(F32)
        g1 = (g - g0).astype(BF16).astype(F32)
        g2 = (g - g0) - g1
        vals = jnp.concatenate([jnp.where(lo, v, 0.0) for v in (t_hi, t_lo, g0, g1, g2)], axis=0)
        res = lax.dot_general(hi, vals.astype(BF16), _NT, preferred_element_type=F32)
        part = [res[:, k * SLOT_RADIX:(k + 1) * SLOT_RADIX] for k in range(5)]
        tok_ref[e] = ((part[0] * 64.0 + part[1]) * float(row_scale)).astype(jnp.int32)
        gs_ref[e] = (part[2] + part[3]) + part[4]


def _topk(aff, cap, row_scale):
    b, e, seq = aff.shape
    n_hi = cap // SLOT_RADIX
    assert cap % SLOT_RADIX == 0 and n_hi <= SLOT_RADIX and seq <= 64 * 64
    out_spec = pl.BlockSpec((None, e, SLOT_RADIX, SLOT_RADIX), lambda i: (i, 0, 0, 0))
    tok, gs = pl.pallas_call(
        functools.partial(_topk_kernel, cap=cap, row_scale=row_scale),
        grid=(b,),
        in_specs=[pl.BlockSpec((None, e, seq), lambda i: (i, 0, 0))],
        out_specs=[out_spec, out_spec],
        out_shape=[jax.ShapeDtypeStruct((b, e, SLOT_RADIX, SLOT_RADIX), jnp.int32),
                   jax.ShapeDtypeStruct((b, e, SLOT_RADIX, SLOT_RADIX), F32)],
        scratch_shapes=[pltpu.VMEM((seq, seq), BF16)],
        compiler_params=_params("arbitrary"),
        name="topk",
    )(aff)
    return tok[:, :, :n_hi].reshape(b, e, cap), gs[:, :, :n_hi].reshape(b, e, cap)


def _gather_kernel(tok_ref, h_ref, o_ref, x_s, *, stride):
    n_exp, cap = tok_ref.shape
    nch = x_s.shape[0] // stride

    def expert(e, carry):
        for i in range(cap):
            t = pl.multiple_of(tok_ref[e, i], nch)
            x_s[pl.ds(i, nch, stride=stride), :] = h_ref[pl.ds(t, nch), :]
        xe = jnp.concatenate([x_s[j * stride:j * stride + cap, :] for j in range(nch)], axis=-1)
        o_ref[e] = xe.astype(BF16)
        return carry

    lax.fori_loop(0, n_exp, expert, 0)


def _gather_tokens(h_slab, tok, d, stride):
    b, rows, _ = h_slab.shape
    _, e, cap = tok.shape
    nch = d // LANES
    return pl.pallas_call(
        functools.partial(_gather_kernel, stride=stride),
        grid=(b,),
        in_specs=[
            pl.BlockSpec((None, e, cap), lambda i: (i, 0, 0), memory_space=pltpu.SMEM),
            pl.BlockSpec((None, rows, LANES), lambda i: (i, 0, 0)),
        ],
        out_specs=pl.BlockSpec((e, None, cap, d), lambda i: (0, i, 0, 0)),
        out_shape=jax.ShapeDtypeStruct((e, b, cap, d), BF16),
        scratch_shapes=[pltpu.VMEM((nch * stride, LANES), F32)],
        compiler_params=_params("parallel"),
        name="moe_gather",
    )(tok, h_slab)


def _ffn_kernel(x_ref, gs_ref, wg_ref, wu_ref, wd_ref, y_ref, wg_s, wu_s, wd_s, *, cap, ff_chunk):
    @pl.when(pl.program_id(1) == 0)
    def _():
        wg_s[...] = wg_ref[...].astype(BF16)
        wu_s[...] = wu_ref[...].astype(BF16)
        wd_s[...] = wd_ref[...].astype(BF16)

    x = x_ref[...]
    y = None
    for c in range(wg_s.shape[1] // ff_chunk):
        cols = slice(c * ff_chunk, (c + 1) * ff_chunk)
        a = jnp.dot(x, wg_s[:, cols], preferred_element_type=F32)
        u = jnp.dot(x, wu_s[:, cols], preferred_element_type=F32)
        hm = (a * jax.nn.sigmoid(a) * u).astype(BF16)
        yc = jnp.dot(hm, wd_s[cols, :], preferred_element_type=F32)
        y = yc if y is None else y + yc
    g = jnp.broadcast_to(gs_ref[...], (LANES, x.shape[0])).T
    bpb, nch, stride, _ = y_ref.shape
    for bb in range(bpb):
        rows = slice(bb * cap, (bb + 1) * cap)
        for j in range(nch):
            y_ref[bb, j, 0:cap, :] = y[rows, j * LANES:(j + 1) * LANES] * g[rows, :]
        y_ref[bb, :, cap:, :] = jnp.zeros((nch, stride - cap, LANES), F32)


def _expert_ffn(xe, gs_col, layer, wg, wu, wd, cap, stride, tm):
    e, m, d = xe.shape
    ff = wg.shape[3]
    nch = d // LANES
    bpb = tm // cap
    return pl.pallas_call(
        functools.partial(_ffn_kernel, cap=cap, ff_chunk=min(512, ff)),
        grid=(e, m // tm),
        in_specs=[
            pl.BlockSpec((None, tm, d), lambda i, j: (i, j, 0)),
            pl.BlockSpec((None, 1, tm), lambda i, j: (i, 0, j)),
            pl.BlockSpec((None, None, d, ff), lambda i, j: (layer, i, 0, 0)),
            pl.BlockSpec((None, None, d, ff), lambda i, j: (layer, i, 0, 0)),
            pl.BlockSpec((None, None, ff, d), lambda i, j: (layer, i, 0, 0)),
        ],
        out_specs=pl.BlockSpec((None, bpb, nch, stride, LANES), lambda i, j: (i, j, 0, 0, 0)),
        out_shape=jax.ShapeDtypeStruct((e, m // cap, nch, stride, LANES), F32),
        scratch_shapes=[pltpu.VMEM((d, ff), BF16), pltpu.VMEM((d, ff), BF16), pltpu.VMEM((ff, d), BF16)],
        compiler_params=_params("parallel", "arbitrary"),
        name="moe_ffn",
    )(xe, gs_col, wg, wu, wd)


def _scatter_kernel(tok_ref, y_ref, x_ref, gate_ref, o_ref, acc_s, *, stride, n_steps):
    group, _, cap = tok_ref.shape
    ts, d = x_ref.shape
    nch = d // LANES
    k = pl.program_id(1)

    @pl.when(k == 0)
    def _():
        acc_s[...] = jnp.zeros_like(acc_s)

    @pl.when(k < n_steps)
    def _():
        def expert(ee, carry):
            def slab(i):
                return pl.ds(pl.multiple_of(tok_ref[ee, 0, i], nch), nch)

            for i0 in range(0, cap, SCATTER_BATCH):
                vals = [acc_s[slab(i), :] + y_ref[ee, pl.ds(i, nch, stride=stride), :]
                        for i in range(i0, i0 + SCATTER_BATCH)]
                for i, v in zip(range(i0, i0 + SCATTER_BATCH), vals):
                    acc_s[slab(i), :] = v
            return carry

        lax.fori_loop(0, group, expert, 0)

    @pl.when(k >= n_steps)
    def _():
        base = pl.multiple_of((k - n_steps) * (ts * nch), ts * nch)
        for j in range(nch):
            lanes = slice(j * LANES, (j + 1) * LANES)
            o_ref[:, lanes] = (x_ref[:, lanes]
                               + gate_ref[:, lanes] * acc_s[pl.ds(base + j, ts, stride=nch), :])


def _scatter_residual(y_cm, tok, x3, modl, stride):
    e, b, _, _ = y_cm.shape
    _, seq, d = x3.shape
    cap = tok.shape[-1]
    nch = d // LANES
    assert cap % SCATTER_BATCH == 0
    ts = min(1024, seq)
    group = SCATTER_EXPERTS
    assert e % group == 0
    n_steps = e // group
    smem_rows = pl.BlockSpec((None, group, 1, cap), lambda i, k: (i, jnp.minimum(k, n_steps - 1), 0, 0),
                             memory_space=pltpu.SMEM)
    tile = pl.BlockSpec((None, ts, d), lambda i, k: (i, jnp.maximum(k - n_steps, 0), 0))
    return pl.pallas_call(
        functools.partial(_scatter_kernel, stride=stride, n_steps=n_steps),
        grid=(b, n_steps + seq // ts),
        in_specs=[
            smem_rows,
            pl.BlockSpec((group, None, nch * stride, LANES),
                         lambda i, k: (jnp.minimum(k, n_steps - 1), i, 0, 0)),
            tile,
            pl.BlockSpec((None, None, 1, d), lambda i, k: (i, 5, 0, 0)),
        ],
        out_specs=tile,
        out_shape=jax.ShapeDtypeStruct((b, seq, d), F32),
        scratch_shapes=[pltpu.VMEM((seq * nch, LANES), F32)],
        compiler_params=_params("parallel", "arbitrary"),
        name="moe_scatter",
    )(tok.reshape(b, e, 1, cap), y_cm, x3, modl)


def _moe_layer(x3, h_slab, aff, modl, layer, wg, wu, wd):
    b, seq, d = x3.shape
    e = aff.shape[1]
    cap = max(1, CAPACITY_FACTOR * seq // e)
    stride = cap + 8
    tok, gs = _topk(aff, cap, d // LANES)
    xe = _gather_tokens(h_slab, tok, d, stride)
    gs_col = gs.transpose(1, 0, 2).reshape(e, 1, b * cap)
    y = _expert_ffn(xe.reshape(e, b * cap, d), gs_col, layer, wg, wu, wd, cap, stride,
                    tm=min(1024, b * cap))
    return _scatter_residual(y.reshape(e, b, -1, LANES), tok, x3, modl, stride)


def kernel(x, c, mod_w, mod_b, norm_g, mix_w_in, qk_g, rpb, conv_w, mix_w_out, fnet_w_out,
           router_w, exp_w_gate, exp_w_up, exp_w_down):
    b, seq, d = x.shape
    depth = mod_w.shape[0]
    mod = _modulation(c, mod_w, mod_b).reshape(depth, b, N_MOD, 1, d)
    tbl = _bias_table(rpb.reshape((-1,) + rpb.shape[2:]))
    for l in range(depth):
        modl = mod[l]
        g1 = norm_g[l, 0].reshape(1, d)
        g2 = norm_g[l, 1].reshape(1, d)
        rw01 = _router_weights(router_w[l])
        n_exp = router_w.shape[2]
        if l % 2 == 0:
            j = l // 2
            proj = _in_projection(x.reshape(b * seq, d), g1, modl, mix_w_in[j].astype(BF16), seq)
            attn, conv = _mixers(proj.reshape(b, seq, -1), qk_g[j], tbl, j, conv_w[j])
            x, h_slab, aff = _out_projection(attn, conv, mix_w_out[j].astype(BF16), x, modl,
                                             g2, rw01, n_exp)
        else:
            x, h_slab, aff = _fnet_layer(x, g1, modl, fnet_w_out[l // 2].astype(BF16), g2, rw01, n_exp)
        x = _moe_layer(x, h_slab, aff, modl, l, exp_w_gate, exp_w_up, exp_w_down)
    return x
```

```python
import functools
import math

import numpy as np
import jax
import jax.numpy as jnp
from jax import lax
from jax.experimental import pallas as pl
from jax.experimental.pallas import tpu as pltpu

F32 = jnp.float32
BF16 = jnp.bfloat16
HIGHEST = lax.Precision.HIGHEST

GRID_W = 64
WIN_ROWS = 8
WIN_COLS = 16
HEAD_DIM = 64
FOURIER_GROUPS = 4
N_MOD = 6
CAPACITY_FACTOR = 2
RMS_EPS = 1e-6
NEG_BIAS = -1e30
LANES = 128
ROWS_PER_STEP = WIN_ROWS // 2
BAND_ROWS = WIN_ROWS + ROWS_PER_STEP
COL_BLOCK = WIN_COLS
KEY_COLS = 2 * WIN_COLS
SCATTER_BATCH = 8
SCATTER_EXPERTS = 4
SLOT_SHIFT = 4
SLOT_RADIX = 1 << SLOT_SHIFT
VMEM_LIMIT = 56 * 1024 * 1024

_NT = (((1,), (1,)), ((), ()))
_TN = (((0,), (0,)), ((), ()))


def _params(*sem):
    return pltpu.CompilerParams(dimension_semantics=sem, vmem_limit_bytes=VMEM_LIMIT)


def _ln_mod(x, g, sc, sh):
    y = x * lax.rsqrt(jnp.mean(x * x, axis=-1, keepdims=True) + RMS_EPS)
    return (y * g) * (1.0 + sc) + sh


def _mod_kernel(c_ref, w_ref, b_ref, o_ref):
    c = c_ref[...]
    s = c * jax.nn.sigmoid(c)
    o_ref[...] = jnp.dot(s, w_ref[...], precision=HIGHEST, preferred_element_type=F32) + b_ref[...]


def _modulation(c, mod_w, mod_b):
    depth, d, n = mod_w.shape
    b = c.shape[0]
    tn = 1536
    return pl.pallas_call(
        _mod_kernel,
        grid=(depth, n // tn),
        in_specs=[
            pl.BlockSpec((b, d), lambda l, j: (0, 0)),
            pl.BlockSpec((None, d, tn), lambda l, j: (l, 0, j)),
            pl.BlockSpec((None, 1, tn), lambda l, j: (l, 0, j)),
        ],
        out_specs=pl.BlockSpec((None, b, tn), lambda l, j: (l, 0, j)),
        out_shape=jax.ShapeDtypeStruct((depth, b, n), F32),
        compiler_params=_params("parallel", "parallel"),
        name="adaln_mod",
    )(c, mod_w, mod_b.reshape(depth, 1, n))


def _row_spec(d, tiles_per_seq, k):
    return pl.BlockSpec((None, None, 1, d), lambda i: (i // tiles_per_seq, k, 0, 0))


def _inproj_kernel(x_ref, g_ref, sc_ref, sh_ref, w_ref, o_ref):
    h = _ln_mod(x_ref[...], g_ref[...], sc_ref[...], sh_ref[...]).astype(BF16)
    o_ref[...] = jnp.dot(h, w_ref[...], preferred_element_type=F32)


def _in_projection(x2, g, modl, w_bf, seq, tm=1024):
    t, d = x2.shape
    n = w_bf.shape[1]
    tps = seq // tm
    return pl.pallas_call(
        _inproj_kernel,
        grid=(t // tm,),
        in_specs=[
            pl.BlockSpec((tm, d), lambda i: (i, 0)),
            pl.BlockSpec((1, d), lambda i: (0, 0)),
            _row_spec(d, tps, 1),
            _row_spec(d, tps, 0),
            pl.BlockSpec((d, n), lambda i: (0, 0)),
        ],
        out_specs=pl.BlockSpec((tm, n), lambda i: (i, 0)),
        out_shape=jax.ShapeDtypeStruct((t, n), F32),
        compiler_params=_params("parallel"),
        name="in_proj",
    )(x2, g, modl, modl, w_bf)


def _bias_table(rpb):
    heads, nr, nc = rpb.shape
    assert nr == 2 * WIN_ROWS - 1 and nc == 2 * WIN_COLS - 1
    n_cb = GRID_W // COL_BLOCK
    qcol = (np.arange(n_cb) * COL_BLOCK)[:, None, None] + np.arange(COL_BLOCK)[None, :, None]
    kcol = np.array([_key_col0(c) for c in range(n_cb)])[:, None, None] + np.arange(KEY_COLS)[None, None, :]
    cs = np.clip(qcol - WIN_COLS // 2, 0, GRID_W - WIN_COLS)
    col_ok = (kcol >= cs) & (kcol < cs + WIN_COLS)
    col_sel = col_ok[..., None] & (np.arange(nc) == (kcol - qcol + WIN_COLS - 1)[..., None])
    place = [[(u, 0) for u in range(ROWS_PER_STEP)],
             [(WIN_ROWS // 2 + u, u) for u in range(ROWS_PER_STEP)],
             [(BAND_ROWS - ROWS_PER_STEP + u, BAND_ROWS - WIN_ROWS) for u in range(ROWS_PER_STEP)]]
    row_sel = np.zeros((len(place), ROWS_PER_STEP, BAND_ROWS, nr), bool)
    for x, per_u in enumerate(place):
        for u, (off, lo) in enumerate(per_u):
            for i in range(lo, lo + WIN_ROWS):
                row_sel[x, u, i, i - off + WIN_ROWS - 1] = True
    row_ok = row_sel.any(axis=-1)
    n_case, nq, nk = len(place), ROWS_PER_STEP * COL_BLOCK, BAND_ROWS * KEY_COLS
    cols = jnp.einsum('hab,cqkb->hcqak', rpb, col_sel.astype(np.float32), precision=HIGHEST)
    depth = -(-nr * KEY_COLS // LANES) * LANES
    cols = jnp.pad(cols.reshape(heads * n_cb * COL_BLOCK, nr * KEY_COLS), ((0, 0), (0, depth - nr * KEY_COLS)))
    sel = row_sel.transpose(0, 3, 1, 2)[:, :, None, :, :, None] & np.eye(KEY_COLS, dtype=bool)[None, None, :, None, None, :]
    sel = np.pad(sel.reshape(n_case, nr * KEY_COLS, ROWS_PER_STEP * nk), ((0, 0), (0, depth - nr * KEY_COLS), (0, 0)))
    ok = row_ok[:, None, :, None, :, None] & col_ok[None, :, None, :, None, :]
    mask = np.where(ok, 0.0, NEG_BIAS).astype(np.float32).reshape(n_case, n_cb, nq, nk)
    return pl.pallas_call(
        functools.partial(_bias_kernel, n_u=ROWS_PER_STEP),
        grid=(n_case,),
        in_specs=[pl.BlockSpec(cols.shape, lambda x: (0, 0)),
                  pl.BlockSpec((None,) + sel.shape[1:], lambda x: (x, 0, 0)),
                  pl.BlockSpec((None, n_cb, nq, nk), lambda x: (x, 0, 0, 0))],
        out_specs=pl.BlockSpec((heads, None, n_cb, nq, nk), lambda x: (0, x, 0, 0, 0)),
        out_shape=jax.ShapeDtypeStruct((heads, n_case, n_cb, nq, nk), F32),
        compiler_params=_params("parallel"),
        name="bias_table",
    )(cols, jnp.asarray(sel, BF16), jnp.asarray(mask))


def _bias_kernel(cols_ref, sel_ref, mask_ref, o_ref, *, n_u):
    heads, n_cb, nq, nk = o_ref.shape
    nqc = nq // n_u
    t = cols_ref[...]
    t0 = t.astype(BF16)
    r = t - t0.astype(F32)
    t1 = r.astype(BF16)
    t2 = (r - t1.astype(F32)).astype(BF16)
    sel = sel_ref[...]
    res = (jnp.dot(t0, sel, preferred_element_type=F32) + jnp.dot(t1, sel, preferred_element_type=F32)
           + jnp.dot(t2, sel, preferred_element_type=F32))
    for h in range(heads):
        for c in range(n_cb):
            rows = slice((h * n_cb + c) * nqc, (h * n_cb + c + 1) * nqc)
            for u in range(n_u):
                o_ref[h, c, u * nqc:(u + 1) * nqc, :] = (res[rows, u * nk:(u + 1) * nk]
                                                         + mask_ref[c, u * nqc:(u + 1) * nqc, :])


def _key_col0(c):
    return min(max(c * COL_BLOCK - WIN_COLS // 2, 0), GRID_W - KEY_COLS)


def _mixers_kernel(q_ref, k_ref, v_ref, bg_ref, cg_ref, xv_ref, qkg_ref, tbl_ref, cw_ref,
                   attn_ref, conv_ref, qn_s, kn_s, qc_s, kc_s, vc_s, pad_s, *, rows):
    seq = q_ref.shape[0]
    lane = lax.broadcasted_iota(jnp.int32, (1, LANES), 1)
    first = lane < HEAD_DIM

    def head_norm(x, g):
        xx = x * x
        s0 = jnp.sum(jnp.where(first, xx, 0.0), axis=-1, keepdims=True)
        s1 = jnp.sum(jnp.where(first, 0.0, xx), axis=-1, keepdims=True)
        ms = jnp.where(first, s0, s1) * (1.0 / HEAD_DIM)
        return x * lax.rsqrt(ms + RMS_EPS) * g

    qn_s[...] = head_norm(q_ref[...], qkg_ref[0:1, :]) * (1.0 / math.sqrt(HEAD_DIM))
    kn_s[...] = head_norm(k_ref[...], qkg_ref[1:2, :])

    n_cb = GRID_W // COL_BLOCK
    for c in range(n_cb):
        k0 = _key_col0(c)
        for r in range(rows):
            qp = qn_s[r * GRID_W + c * COL_BLOCK:r * GRID_W + (c + 1) * COL_BLOCK, :]
            qc_s[0, c, r * COL_BLOCK:(r + 1) * COL_BLOCK, :] = jnp.where(first, qp, 0.0).astype(BF16)
            qc_s[1, c, r * COL_BLOCK:(r + 1) * COL_BLOCK, :] = jnp.where(first, 0.0, qp).astype(BF16)
            keys = slice(r * GRID_W + k0, r * GRID_W + k0 + KEY_COLS)
            dst = slice(r * KEY_COLS, (r + 1) * KEY_COLS)
            kc_s[c, dst, :] = kn_s[keys, :].astype(BF16)
            vp = v_ref[keys, :]
            vc_s[0, c, dst, :] = jnp.where(first, vp, 1.0).astype(BF16)
            vc_s[1, c, dst, :] = jnp.where(first, 1.0, vp).astype(BF16)

    n_groups = rows // ROWS_PER_STEP
    nq = ROWS_PER_STEP * COL_BLOCK
    for g in range(n_groups):
        b0 = min(max(g * ROWS_PER_STEP - WIN_ROWS // 2, 0), rows - BAND_ROWS)
        case = 0 if g == 0 else (2 if g == n_groups - 1 else 1)
        patch = slice(b0 * KEY_COLS, (b0 + BAND_ROWS) * KEY_COLS)
        blocks = [(c, h) for c in range(n_cb) for h in range(2)]
        s = jnp.concatenate(
            [lax.dot_general(qc_s[h, c, g * nq:(g + 1) * nq, :], kc_s[c, patch, :], _NT,
                             preferred_element_type=F32) + tbl_ref[h, case, c] for c, h in blocks], axis=0)
        p = jnp.exp(s - jnp.max(s, axis=-1, keepdims=True)).astype(BF16)
        pv = jnp.concatenate(
            [jnp.dot(p[i * nq:(i + 1) * nq, :], vc_s[h, c, patch, :], preferred_element_type=F32)
             for i, (c, h) in enumerate(blocks)], axis=0)
        o = pv / pltpu.roll(pv, HEAD_DIM, axis=1)
        for c in range(n_cb):
            out = jnp.where(first, o[2 * c * nq:(2 * c + 1) * nq, :], o[(2 * c + 1) * nq:(2 * c + 2) * nq, :])
            for u in range(ROWS_PER_STEP):
                row0 = (g * ROWS_PER_STEP + u) * GRID_W + c * COL_BLOCK
                attn_ref[row0:row0 + COL_BLOCK, :] = out[u * COL_BLOCK:(u + 1) * COL_BLOCK, :].astype(BF16)

    zeros = jnp.zeros((8, LANES), F32)
    pad_s[0:8, :] = zeros
    pad_s[seq + 8:seq + 16, :] = zeros
    pad_s[8:seq + 8, :] = cg_ref[...] * xv_ref[...]
    z = (pad_s[7:seq + 7, :] * cw_ref[0:1, :] + pad_s[8:seq + 8, :] * cw_ref[1:2, :]
         + pad_s[9:seq + 9, :] * cw_ref[2:3, :])
    conv_ref[...] = (bg_ref[...] * z).astype(BF16)


def _mixers(proj3, qk_g, tbl, layer, conv_w):
    b, seq, n = proj3.shape
    width = n // 6
    nblk = width // LANES
    rows = seq // GRID_W
    assert rows >= BAND_ROWS + ROWS_PER_STEP and rows % ROWS_PER_STEP == 0
    assert LANES == 2 * HEAD_DIM and GRID_W % COL_BLOCK == 0
    qkg = jnp.tile(qk_g, (1, LANES // HEAD_DIM))
    n_cb = GRID_W // COL_BLOCK

    def col(k):
        return pl.BlockSpec((None, seq, LANES), lambda i, j, k=k: (i, 0, k * nblk + j))

    out_spec = pl.BlockSpec((None, seq, LANES), lambda i, j: (i, 0, j))
    hp = LANES // HEAD_DIM
    return pl.pallas_call(
        functools.partial(_mixers_kernel, rows=rows),
        grid=(b, nblk),
        in_specs=[col(0), col(1), col(2), col(3), col(4), col(5),
                  pl.BlockSpec((2, LANES), lambda i, j: (0, 0)),
                  pl.BlockSpec((hp,) + tbl.shape[1:], lambda i, j: (layer * nblk + j, 0, 0, 0, 0)),
                  pl.BlockSpec((conv_w.shape[0], LANES), lambda i, j: (0, j))],
        out_specs=[out_spec, out_spec],
        out_shape=[jax.ShapeDtypeStruct((b, seq, width), BF16)] * 2,
        scratch_shapes=[pltpu.VMEM((seq, LANES), F32), pltpu.VMEM((seq, LANES), F32),
                        pltpu.VMEM((hp, n_cb, rows * COL_BLOCK, LANES), BF16),
                        pltpu.VMEM((n_cb, rows * KEY_COLS, LANES), BF16),
                        pltpu.VMEM((hp, n_cb, rows * KEY_COLS, LANES), BF16),
                        pltpu.VMEM((seq + 16, LANES), F32)],
        compiler_params=_params("parallel", "parallel"),
        name="attn_conv",
    )(proj3, proj3, proj3, proj3, proj3, proj3, qkg, tbl, conv_w)


def _route(x, g_ref, sc_ref, sh_ref, rw_ref, h_ref, aff_ref):
    h = _ln_mod(x, g_ref[...], sc_ref[...], sh_ref[...])
    tm, d = h.shape
    nch = d // LANES
    n_exp = aff_ref.shape[0]
    for j in range(nch):
        h_ref[pl.ds(j, tm, stride=nch), :] = h[:, j * LANES:(j + 1) * LANES]
    h0 = h.astype(BF16)
    h1 = (h - h0.astype(F32)).astype(BF16)
    lg = jnp.dot(h0, rw_ref[...], preferred_element_type=F32)
    lg = (lg[:, :LANES] + lg[:, LANES:]) + jnp.dot(h1, rw_ref[:, :LANES], preferred_element_type=F32)
    lane = lax.broadcasted_iota(jnp.int32, (1, LANES), 1)
    lg = jnp.where(lane < n_exp, lg, NEG_BIAS)
    p = jnp.exp(lg - jnp.max(lg, axis=-1, keepdims=True))
    aff = p / jnp.sum(p, axis=-1, keepdims=True)
    aff_ref[...] = aff.T[:n_exp, :]


def _router_weights(router_w):
    w = jnp.pad(router_w, ((0, 0), (0, LANES - router_w.shape[1])))
    w0 = w.astype(BF16)
    w1 = (w - w0.astype(F32)).astype(BF16)
    return jnp.concatenate([w0, w1], axis=1)


def _route_specs(b, seq, d, e, tm):
    nch = d // LANES
    in_specs = [pl.BlockSpec((1, d), lambda i, j: (0, 0)),
                pl.BlockSpec((None, None, 1, d), lambda i, j: (i, 4, 0, 0)),
                pl.BlockSpec((None, None, 1, d), lambda i, j: (i, 3, 0, 0)),
                pl.BlockSpec((d, 2 * LANES), lambda i, j: (0, 0))]
    out_specs = [pl.BlockSpec((None, tm * nch, LANES), lambda i, j: (i, j, 0)),
                 pl.BlockSpec((None, e, tm), lambda i, j: (i, 0, j))]
    out_shape = [jax.ShapeDtypeStruct((b, seq * nch, LANES), F32),
                 jax.ShapeDtypeStruct((b, e, seq), F32)]
    return in_specs, out_specs, out_shape


def _outproj_kernel(a_ref, c_ref, wa_ref, wc_ref, x_ref, gate_ref, g2_ref, sc2_ref, sh2_ref, rw_ref,
                    o_ref, h_ref, aff_ref):
    m = jnp.dot(a_ref[...], wa_ref[...], preferred_element_type=F32)
    m = m + jnp.dot(c_ref[...], wc_ref[...], preferred_element_type=F32)
    x = x_ref[...] + gate_ref[...] * m
    o_ref[...] = x
    _route(x, g2_ref, sc2_ref, sh2_ref, rw_ref, h_ref, aff_ref)


def _out_projection(attn3, conv3, w_bf, x3, modl, g2, rw01, n_exp, tm=512):
    b, seq, d = x3.shape
    wa = attn3.shape[2]
    wc = conv3.shape[2]
    r_in, r_out, r_shape = _route_specs(b, seq, d, n_exp, tm)
    tile = pl.BlockSpec((None, tm, d), lambda i, j: (i, j, 0))
    return pl.pallas_call(
        _outproj_kernel,
        grid=(b, seq // tm),
        in_specs=[
            pl.BlockSpec((None, tm, wa), lambda i, j: (i, j, 0)),
            pl.BlockSpec((None, tm, wc), lambda i, j: (i, j, 0)),
            pl.BlockSpec((wa, d), lambda i, j: (0, 0)),
            pl.BlockSpec((wc, d), lambda i, j: (1, 0)),
            tile,
            pl.BlockSpec((None, None, 1, d), lambda i, j: (i, 2, 0, 0)),
        ] + r_in,
        out_specs=[tile] + r_out,
        out_shape=[jax.ShapeDtypeStruct((b, seq, d), F32)] + r_shape,
        compiler_params=_params("parallel", "parallel"),
        name="out_proj",
    )(attn3, conv3, w_bf, w_bf, x3, modl, g2, modl, modl, rw01)


def _dft_tables(n):
    jk = (np.arange(n)[:, None] * np.arange(n)[None, :]) % n
    ang = 2.0 * np.pi * jk.astype(np.float64) / n
    return np.cos(ang).astype(np.float32), np.sin(ang).astype(np.float32)


def _fnet_kernel(x_ref, g_ref, sc_ref, sh_ref, chan_ref, cs_ref, ss_ref, w_ref, gate_ref, g2_ref, sc2_ref,
                 sh2_ref, rw_ref, o_ref, h_ref, aff_ref, a_s, b_s, *, norm, gw, chunk):
    seq, d = x_ref.shape
    tm = o_ref.shape[0]
    j = pl.program_id(1)

    @pl.when(j == 0)
    def _():
        for r0 in range(0, seq, chunk):
            rows = slice(r0, r0 + chunk)
            h = _ln_mod(x_ref[rows, :], g_ref[...], sc_ref[...], sh_ref[...]).astype(BF16)
            for g in range(d // gw):
                cols = slice(g * gw, (g + 1) * gw)
                ab = jnp.dot(h[:, cols], chan_ref[...], preferred_element_type=F32)
                a_s[rows, cols] = ab[:, :gw].astype(BF16)
                b_s[rows, cols] = ab[:, gw:].astype(BF16)

    f = jnp.dot(cs_ref[...], a_s[...], preferred_element_type=F32)
    f = f - jnp.dot(ss_ref[...], b_s[...], preferred_element_type=F32)
    m = jnp.dot((f * norm).astype(BF16), w_ref[...], preferred_element_type=F32)
    x = x_ref[pl.ds(pl.multiple_of(j * tm, tm), tm), :] + gate_ref[...] * m
    o_ref[...] = x
    _route(x, g2_ref, sc2_ref, sh2_ref, rw_ref, h_ref, aff_ref)


def _fnet_layer(x3, g, modl, w_bf, g2, rw01, n_exp, tm=512):
    b, seq, d = x3.shape
    gw = d // FOURIER_GROUPS
    cc, sc = _dft_tables(gw)
    cs_chan = jnp.concatenate([jnp.asarray(cc), jnp.asarray(sc)], axis=1).astype(BF16)
    cseq, sseq = _dft_tables(seq)
    cseq = jnp.asarray(cseq).astype(BF16)
    sseq = jnp.asarray(sseq).astype(BF16)
    norm = 1.0 / math.sqrt(seq * gw)
    r_in, r_out, r_shape = _route_specs(b, seq, d, n_exp, tm)
    mod_row = lambda k: pl.BlockSpec((None, None, 1, d), lambda i, j: (i, k, 0, 0))
    return pl.pallas_call(
        functools.partial(_fnet_kernel, norm=norm, gw=gw, chunk=min(512, seq)),
        grid=(b, seq // tm),
        in_specs=[
            pl.BlockSpec((None, seq, d), lambda i, j: (i, 0, 0)),
            pl.BlockSpec((1, d), lambda i, j: (0, 0)),
            mod_row(1), mod_row(0),
            pl.BlockSpec((gw, 2 * gw), lambda i, j: (0, 0)),
            pl.BlockSpec((tm, seq), lambda i, j: (j, 0)),
            pl.BlockSpec((tm, seq), lambda i, j: (j, 0)),
            pl.BlockSpec((d, d), lambda i, j: (0, 0)),
            mod_row(2),
        ] + r_in,
        out_specs=[pl.BlockSpec((None, tm, d), lambda i, j: (i, j, 0))] + r_out,
        out_shape=[jax.ShapeDtypeStruct((b, seq, d), F32)] + r_shape,
        scratch_shapes=[pltpu.VMEM((seq, d), BF16), pltpu.VMEM((seq, d), BF16)],
        compiler_params=_params("parallel", "arbitrary"),
        name="fnet",
    )(x3, g, modl, modl, cs_chan, cseq, sseq, w_bf, modl, g2, modl, modl, rw01)


def _topk_kernel(aff_ref, tok_ref, gs_ref, tri_s, *, cap, row_scale):
    n_exp, seq = aff_ref.shape

    @pl.when(pl.program_id(0) == 0)
    def _():
        r = lax.broadcasted_iota(jnp.int32, (seq, seq), 0)
        c = lax.broadcasted_iota(jnp.int32, (seq, seq), 1)
        tri_s[...] = jnp.where(r < c, 1.0, 0.0).astype(BF16)

    a = aff_ref[...]

    def enough(cand):
        cnt = jnp.sum((a >= pltpu.bitcast(cand, F32)).astype(jnp.int32), axis=1, keepdims=True)
        return cnt >= cap

    def search(i, t):
        hi = lax.shift_left(jnp.int32(1), 29 - 2 * i)
        lo = lax.shift_left(jnp.int32(1), 28 - 2 * i)
        ok_hi, ok_lo, ok_both = enough(t | hi), enough(t | lo), enough(t | hi | lo)
        return jnp.where(ok_hi, jnp.where(ok_both, t | hi | lo, t | hi), jnp.where(ok_lo, t | lo, t))

    top = jnp.full((a.shape[0], 1), 1 << 30, jnp.int32)
    thr = lax.fori_loop(0, 15, search, jnp.where(enough(top), top, 0))
    thr = pltpu.bitcast(thr, F32)
    gt = a > thr
    eq = a == thr
    need = cap - jnp.sum(gt.astype(jnp.int32), axis=1, keepdims=True)
    tri = tri_s[...]
    eq_before = jnp.dot(jnp.where(eq, 1.0, 0.0).astype(BF16), tri, preferred_element_type=F32)
    sel = gt | (eq & (eq_before < need.astype(F32)))
    before = jnp.dot(jnp.where(sel, 1.0, 0.0).astype(BF16), tri, preferred_element_type=F32)
    rank = jnp.where(sel, before.astype(jnp.int32), -1)

    digit = lax.broadcasted_iota(jnp.int32, (SLOT_RADIX, seq), 0)
    tok_ids = lax.broadcasted_iota(jnp.int32, (1, seq), 1)
    t_hi = lax.shift_right_logical(tok_ids, 6).astype(F32)
    t_lo = (tok_ids & 63).astype(F32)
    for e in range(n_exp):
        r = rank[e:e + 1, :]
        hi = jnp.where(digit == lax.shift_right_arithmetic(r, SLOT_SHIFT), 1.0, 0.0).astype(BF16)
        lo = digit == (r & (SLOT_RADIX - 1))
        g = a[e:e + 1, :]
        g0 = g.astype(BF16).astype(F32)
        g1 = (g - g0).astype(BF16).astype(F32)
        g2 = (g - g0) - g1
        vals = jnp.concatenate([jnp.where(lo, v, 0.0) for v in (t_hi, t_lo, g0, g1, g2)], axis=0)
        res = lax.dot_general(hi, vals.astype(BF16), _NT, preferred_element_type=F32)
        part = [res[:, k * SLOT_RADIX:(k + 1) * SLOT_RADIX] for k in range(5)]
        tok_ref[e] = ((part[0] * 64.0 + part[1]) * float(row_scale)).astype(jnp.int32)
        gs_ref[e] = (part[2] + part[3]) + part[4]


def _topk(aff, cap, row_scale):
    b, e, seq = aff.shape
    n_hi = cap // SLOT_RADIX
    assert cap % SLOT_RADIX == 0 and n_hi <= SLOT_RADIX and seq <= 64 * 64
    out_spec = pl.BlockSpec((None, e, SLOT_RADIX, SLOT_RADIX), lambda i: (i, 0, 0, 0))
    tok, gs = pl.pallas_call(
        functools.partial(_topk_kernel, cap=cap, row_scale=row_scale),
        grid=(b,),
        in_specs=[pl.BlockSpec((None, e, seq), lambda i: (i, 0, 0))],
        out_specs=[out_spec, out_spec],
        out_shape=[jax.ShapeDtypeStruct((b, e, SLOT_RADIX, SLOT_RADIX), jnp.int32),
                   jax.ShapeDtypeStruct((b, e, SLOT_RADIX, SLOT_RADIX), F32)],
        scratch_shapes=[pltpu.VMEM((seq, seq), BF16)],
        compiler_params=_params("arbitrary"),
        name="topk",
    )(aff)
    return tok[:, :, :n_hi].reshape(b, e, cap), gs[:, :, :n_hi].reshape(b, e, cap)


def _gather_kernel(tok_ref, h_ref, o_ref, x_s, *, stride):
    n_exp, cap = tok_ref.shape
    nch = x_s.shape[0] // stride

    def expert(e, carry):
        for i in range(cap):
            t = pl.multiple_of(tok_ref[e, i], nch)
            x_s[pl.ds(i, nch, stride=stride), :] = h_ref[pl.ds(t, nch), :]
        xe = jnp.concatenate([x_s[j * stride:j * stride + cap, :] for j in range(nch)], axis=-1)
        o_ref[e] = xe.astype(BF16)
        return carry

    lax.fori_loop(0, n_exp, expert, 0)


def _gather_tokens(h_slab, tok, d, stride):
    b, rows, _ = h_slab.shape
    _, e, cap = tok.shape
    nch = d // LANES
    return pl.pallas_call(
        functools.partial(_gather_kernel, stride=stride),
        grid=(b,),
        in_specs=[
            pl.BlockSpec((None, e, cap), lambda i: (i, 0, 0), memory_space=pltpu.SMEM),
            pl.BlockSpec((None, rows, LANES), lambda i: (i, 0, 0)),
        ],
        out_specs=pl.BlockSpec((e, None, cap, d), lambda i: (0, i, 0, 0)),
        out_shape=jax.ShapeDtypeStruct((e, b, cap, d), BF16),
        scratch_shapes=[pltpu.VMEM((nch * stride, LANES), F32)],
        compiler_params=_params("parallel"),
        name="moe_gather",
    )(tok, h_slab)


def _ffn_kernel(x_ref, gs_ref, wg_ref, wu_ref, wd_ref, y_ref, wg_s, wu_s, wd_s, *, cap, ff_chunk):
    @pl.when(pl.program_id(1) == 0)
    def _():
        wg_s[...] = wg_ref[...].astype(BF16)
        wu_s[...] = wu_ref[...].astype(BF16)
        wd_s[...] = wd_ref[...].astype(BF16)

    x = x_ref[...]
    y = None
    for c in range(wg_s.shape[1] // ff_chunk):
        cols = slice(c * ff_chunk, (c + 1) * ff_chunk)
        a = jnp.dot(x, wg_s[:, cols], preferred_element_type=F32)
        u = jnp.dot(x, wu_s[:, cols], preferred_element_type=F32)
        hm = (a * jax.nn.sigmoid(a) * u).astype(BF16)
        yc = jnp.dot(hm, wd_s[cols, :], preferred_element_type=F32)
        y = yc if y is None else y + yc
    g = jnp.broadcast_to(gs_ref[...], (LANES, x.shape[0])).T
    bpb, nch, stride, _ = y_ref.shape
    for bb in range(bpb):
        rows = slice(bb * cap, (bb + 1) * cap)
        for j in range(nch):
            y_ref[bb, j, 0:cap, :] = y[rows, j * LANES:(j + 1) * LANES] * g[rows, :]
        y_ref[bb, :, cap:, :] = jnp.zeros((nch, stride - cap, LANES), F32)


def _expert_ffn(xe, gs_col, layer, wg, wu, wd, cap, stride, tm):
    e, m, d = xe.shape
    ff = wg.shape[3]
    nch = d // LANES
    bpb = tm // cap
    return pl.pallas_call(
        functools.partial(_ffn_kernel, cap=cap, ff_chunk=min(512, ff)),
        grid=(e, m // tm),
        in_specs=[
            pl.BlockSpec((None, tm, d), lambda i, j: (i, j, 0)),
            pl.BlockSpec((None, 1, tm), lambda i, j: (i, 0, j)),
            pl.BlockSpec((None, None, d, ff), lambda i, j: (layer, i, 0, 0)),
            pl.BlockSpec((None, None, d, ff), lambda i, j: (layer, i, 0, 0)),
            pl.BlockSpec((None, None, ff, d), lambda i, j: (layer, i, 0, 0)),
        ],
        out_specs=pl.BlockSpec((None, bpb, nch, stride, LANES), lambda i, j: (i, j, 0, 0, 0)),
        out_shape=jax.ShapeDtypeStruct((e, m // cap, nch, stride, LANES), F32),
        scratch_shapes=[pltpu.VMEM((d, ff), BF16), pltpu.VMEM((d, ff), BF16), pltpu.VMEM((ff, d), BF16)],
        compiler_params=_params("parallel", "arbitrary"),
        name="moe_ffn",
    )(xe, gs_col, wg, wu, wd)


def _scatter_kernel(tok_ref, y_ref, x_ref, gate_ref, o_ref, acc_s, *, stride, n_steps):
    group, _, cap = tok_ref.shape
    ts, d = x_ref.shape
    nch = d // LANES
    k = pl.program_id(1)

    @pl.when(k == 0)
    def _():
        acc_s[...] = jnp.zeros_like(acc_s)

    @pl.when(k < n_steps)
    def _():
        def expert(ee, carry):
            def slab(i):
                return pl.ds(pl.multiple_of(tok_ref[ee, 0, i], nch), nch)

            for i0 in range(0, cap, SCATTER_BATCH):
                vals = [acc_s[slab(i), :] + y_ref[ee, pl.ds(i, nch, stride=stride), :]
                        for i in range(i0, i0 + SCATTER_BATCH)]
                for i, v in zip(range(i0, i0 + SCATTER_BATCH), vals):
                    acc_s[slab(i), :] = v
            return carry

        lax.fori_loop(0, group, expert, 0)

    @pl.when(k >= n_steps)
    def _():
        base = pl.multiple_of((k - n_steps) * (ts * nch), ts * nch)
        for j in range(nch):
            lanes = slice(j * LANES, (j + 1) * LANES)
            o_ref[:, lanes] = (x_ref[:, lanes]
                               + gate_ref[:, lanes] * acc_s[pl.ds(base + j, ts, stride=nch), :])


def _scatter_residual(y_cm, tok, x3, modl, stride):
    e, b, _, _ = y_cm.shape
    _, seq, d = x3.shape
    cap = tok.shape[-1]
    nch = d // LANES
    assert cap % SCATTER_BATCH == 0
    ts = min(1024, seq)
    group = SCATTER_EXPERTS
    assert e % group == 0
    n_steps = e // group
    smem_rows = pl.BlockSpec((None, group, 1, cap), lambda i, k: (i, jnp.minimum(k, n_steps - 1), 0, 0),
                             memory_space=pltpu.SMEM)
    tile = pl.BlockSpec((None, ts, d), lambda i, k: (i, jnp.maximum(k - n_steps, 0), 0))
    return pl.pallas_call(
        functools.partial(_scatter_kernel, stride=stride, n_steps=n_steps),
        grid=(b, n_steps + seq // ts),
        in_specs=[
            smem_rows,
            pl.BlockSpec((group, None, nch * stride, LANES),
                         lambda i, k: (jnp.minimum(k, n_steps - 1), i, 0, 0)),
            tile,
            pl.BlockSpec((None, None, 1, d), lambda i, k: (i, 5, 0, 0)),
        ],
        out_specs=tile,
        out_shape=jax.ShapeDtypeStruct((b, seq, d), F32),
        scratch_shapes=[pltpu.VMEM((seq * nch, LANES), F32)],
        compiler_params=_params("parallel", "arbitrary"),
        name="moe_scatter",
    )(tok.reshape(b, e, 1, cap), y_cm, x3, modl)


def _moe_layer(x3, h_slab, aff, modl, layer, wg, wu, wd):
    b, seq, d = x3.shape
    e = aff.shape[1]
    cap = max(1, CAPACITY_FACTOR * seq // e)
    stride = cap + 8
    tok, gs = _topk(aff, cap, d // LANES)
    xe = _gather_tokens(h_slab, tok, d, stride)
    gs_col = gs.transpose(1, 0, 2).reshape(e, 1, b * cap)
    y = _expert_ffn(xe.reshape(e, b * cap, d), gs_col, layer, wg, wu, wd, cap, stride,
                    tm=min(1024, b * cap))
    return _scatter_residual(y.reshape(e, b, -1, LANES), tok, x3, modl, stride)


def kernel(x, c, mod_w, mod_b, norm_g, mix_w_in, qk_g, rpb, conv_w, mix_w_out, fnet_w_out,
           router_w, exp_w_gate, exp_w_up, exp_w_down):
    b, seq, d = x.shape
    depth = mod_w.shape[0]
    mod = _modulation(c, mod_w, mod_b).reshape(depth, b, N_MOD, 1, d)
    tbl = _bias_table(rpb.reshape((-1,) + rpb.shape[2:]))
    for l in range(depth):
        modl = mod[l]
        g1 = norm_g[l, 0].reshape(1, d)
        g2 = norm_g[l, 1].reshape(1, d)
        rw01 = _router_weights(router_w[l])
        n_exp = router_w.shape[2]
        if l % 2 == 0:
            j = l // 2
            proj = _in_projection(x.reshape(b * seq, d), g1, modl, mix_w_in[j].astype(BF16), seq)
            attn, conv = _mixers(proj.reshape(b, seq, -1), qk_g[j], tbl, j, conv_w[j])
            x, h_slab, aff = _out_projection(attn, conv, mix_w_out[j].astype(BF16), x, modl,
                                             g2, rw01, n_exp)
        else:
            x, h_slab, aff = _fnet_layer(x, g1, modl, fnet_w_out[l // 2].astype(BF16), g2, rw01, n_exp)
        x = _moe_layer(x, h_slab, aff, modl, l, exp_w_gate, exp_w_up, exp_w_down)
    return x
```

```python
import functools
import math

import numpy as np
import jax
import jax.numpy as jnp
from jax import lax
from jax.experimental import pallas as pl
from jax.experimental.pallas import tpu as pltpu

F32 = jnp.float32
BF16 = jnp.bfloat16
HIGHEST = lax.Precision.HIGHEST

GRID_W = 64
WIN_ROWS = 8
WIN_COLS = 16
HEAD_DIM = 64
FOURIER_GROUPS = 4
N_MOD = 6
CAPACITY_FACTOR = 2
RMS_EPS = 1e-6
NEG_BIAS = -1e30
LANES = 128
ROWS_PER_STEP = WIN_ROWS // 2
BAND_ROWS = WIN_ROWS + ROWS_PER_STEP
COL_BLOCK = WIN_COLS
KEY_COLS = 2 * WIN_COLS
SCATTER_BATCH = 8
SCATTER_EXPERTS = 4
SLOT_SHIFT = 4
SLOT_RADIX = 1 << SLOT_SHIFT
VMEM_LIMIT = 56 * 1024 * 1024

_NT = (((1,), (1,)), ((), ()))


def _params(*sem):
    return pltpu.CompilerParams(dimension_semantics=sem, vmem_limit_bytes=VMEM_LIMIT)


def _ln_mod(x, g, sc, sh):
    y = x * lax.rsqrt(jnp.mean(x * x, axis=-1, keepdims=True) + RMS_EPS)
    return (y * g) * (1.0 + sc) + sh


def _mod_kernel(c_ref, w_ref, b_ref, o_ref):
    c = c_ref[...]
    s = c * jax.nn.sigmoid(c)
    o_ref[...] = jnp.dot(s, w_ref[...], precision=HIGHEST, preferred_element_type=F32) + b_ref[...]


def _modulation(c, mod_w, mod_b):
    depth, d, n = mod_w.shape
    b = c.shape[0]
    tn = 1536
    return pl.pallas_call(
        _mod_kernel,
        grid=(depth, n // tn),
        in_specs=[
            pl.BlockSpec((b, d), lambda l, j: (0, 0)),
            pl.BlockSpec((None, d, tn), lambda l, j: (l, 0, j)),
            pl.BlockSpec((None, 1, tn), lambda l, j: (l, 0, j)),
        ],
        out_specs=pl.BlockSpec((None, b, tn), lambda l, j: (l, 0, j)),
        out_shape=jax.ShapeDtypeStruct((depth, b, n), F32),
        compiler_params=_params("parallel", "parallel"),
        name="adaln_mod",
    )(c, mod_w, mod_b.reshape(depth, 1, n))


def _row_spec(d, tiles_per_seq, k):
    return pl.BlockSpec((None, None, 1, d), lambda i: (i // tiles_per_seq, k, 0, 0))


def _inproj_kernel(x_ref, g_ref, sc_ref, sh_ref, w_ref, o_ref):
    h = _ln_mod(x_ref[...], g_ref[...], sc_ref[...], sh_ref[...]).astype(BF16)
    o_ref[...] = jnp.dot(h, w_ref[...], preferred_element_type=F32)


def _in_projection(x2, g, modl, w_bf, seq, tm=1024):
    t, d = x2.shape
    n = w_bf.shape[1]
    tps = seq // tm
    return pl.pallas_call(
        _inproj_kernel,
        grid=(t // tm,),
        in_specs=[
            pl.BlockSpec((tm, d), lambda i: (i, 0)),
            pl.BlockSpec((1, d), lambda i: (0, 0)),
            _row_spec(d, tps, 1),
            _row_spec(d, tps, 0),
            pl.BlockSpec((d, n), lambda i: (0, 0)),
        ],
        out_specs=pl.BlockSpec((tm, n), lambda i: (i, 0)),
        out_shape=jax.ShapeDtypeStruct((t, n), F32),
        compiler_params=_params("parallel"),
        name="in_proj",
    )(x2, g, modl, modl, w_bf)


def _bias_table(rpb):
    heads, nr, nc = rpb.shape
    assert nr == 2 * WIN_ROWS - 1 and nc == 2 * WIN_COLS - 1
    n_cb = GRID_W // COL_BLOCK
    qcol = (np.arange(n_cb) * COL_BLOCK)[:, None, None] + np.arange(COL_BLOCK)[None, :, None]
    kcol = np.array([_key_col0(c) for c in range(n_cb)])[:, None, None] + np.arange(KEY_COLS)[None, None, :]
    cs = np.clip(qcol - WIN_COLS // 2, 0, GRID_W - WIN_COLS)
    col_ok = (kcol >= cs) & (kcol < cs + WIN_COLS)
    col_sel = col_ok[..., None] & (np.arange(nc) == (kcol - qcol + WIN_COLS - 1)[..., None])
    place = [[(u, 0) for u in range(ROWS_PER_STEP)],
             [(WIN_ROWS // 2 + u, u) for u in range(ROWS_PER_STEP)],
             [(BAND_ROWS - ROWS_PER_STEP + u, BAND_ROWS - WIN_ROWS) for u in range(ROWS_PER_STEP)]]
    row_sel = np.zeros((len(place), ROWS_PER_STEP, BAND_ROWS, nr), bool)
    for x, per_u in enumerate(place):
        for u, (off, lo) in enumerate(per_u):
            for i in range(lo, lo + WIN_ROWS):
                row_sel[x, u, i, i - off + WIN_ROWS - 1] = True
    row_ok = row_sel.any(axis=-1)
    n_case, nq, nk = len(place), ROWS_PER_STEP * COL_BLOCK, BAND_ROWS * KEY_COLS
    cols = jnp.einsum('hab,cqkb->hcqak', rpb, col_sel.astype(np.float32), precision=HIGHEST)
    depth = -(-nr * KEY_COLS // LANES) * LANES
    cols = jnp.pad(cols.reshape(heads * n_cb * COL_BLOCK, nr * KEY_COLS), ((0, 0), (0, depth - nr * KEY_COLS)))
    sel = row_sel.transpose(0, 3, 1, 2)[:, :, None, :, :, None] & np.eye(KEY_COLS, dtype=bool)[None, None, :, None, None, :]
    sel = np.pad(sel.reshape(n_case, nr * KEY_COLS, ROWS_PER_STEP * nk), ((0, 0), (0, depth - nr * KEY_COLS), (0, 0)))
    ok = row_ok[:, None, :, None, :, None] & col_ok[None, :, None, :, None, :]
    mask = np.where(ok, 0.0, NEG_BIAS).astype(np.float32).reshape(n_case, n_cb, nq, nk)
    return pl.pallas_call(
        functools.partial(_bias_kernel, n_u=ROWS_PER_STEP),
        grid=(n_case,),
        in_specs=[pl.BlockSpec(cols.shape, lambda x: (0, 0)),
                  pl.BlockSpec((None,) + sel.shape[1:], lambda x: (x, 0, 0)),
                  pl.BlockSpec((None, n_cb, nq, nk), lambda x: (x, 0, 0, 0))],
        out_specs=pl.BlockSpec((heads, None, n_cb, nq, nk), lambda x: (0, x, 0, 0, 0)),
        out_shape=jax.ShapeDtypeStruct((heads, n_case, n_cb, nq, nk), F32),
        compiler_params=_params("parallel"),
        name="bias_table",
    )(cols, jnp.asarray(sel, BF16), jnp.asarray(mask))


def _bias_kernel(cols_ref, sel_ref, mask_ref, o_ref, *, n_u):
    heads, n_cb, nq, nk = o_ref.shape
    nqc = nq // n_u
    t = cols_ref[...]
    t0 = t.astype(BF16)
    r = t - t0.astype(F32)
    t1 = r.astype(BF16)
    t2 = (r - t1.astype(F32)).astype(BF16)
    sel = sel_ref[...]
    res = (jnp.dot(t0, sel, preferred_element_type=F32) + jnp.dot(t1, sel, preferred_element_type=F32)
           + jnp.dot(t2, sel, preferred_element_type=F32))
    for h in range(heads):
        for c in range(n_cb):
            rows = slice((h * n_cb + c) * nqc, (h * n_cb + c + 1) * nqc)
            for u in range(n_u):
                o_ref[h, c, u * nqc:(u + 1) * nqc, :] = (res[rows, u * nk:(u + 1) * nk]
                                                         + mask_ref[c, u * nqc:(u + 1) * nqc, :])


def _key_col0(c):
    return min(max(c * COL_BLOCK - WIN_COLS // 2, 0), GRID_W - KEY_COLS)


def _mixers_kernel(q_ref, k_ref, v_ref, bg_ref, cg_ref, xv_ref, qkg_ref, tbl_ref, cw_ref,
                   attn_ref, conv_ref, qn_s, kn_s, qc_s, kc_s, vc_s, pad_s, *, rows):
    seq = q_ref.shape[0]
    lane = lax.broadcasted_iota(jnp.int32, (1, LANES), 1)
    first = lane < HEAD_DIM

    def head_norm(x, g):
        xx = x * x
        s0 = jnp.sum(jnp.where(first, xx, 0.0), axis=-1, keepdims=True)
        s1 = jnp.sum(jnp.where(first, 0.0, xx), axis=-1, keepdims=True)
        ms = jnp.where(first, s0, s1) * (1.0 / HEAD_DIM)
        return x * lax.rsqrt(ms + RMS_EPS) * g

    qn_s[...] = head_norm(q_ref[...], qkg_ref[0:1, :]) * (1.0 / math.sqrt(HEAD_DIM))
    kn_s[...] = head_norm(k_ref[...], qkg_ref[1:2, :])

    n_cb = GRID_W // COL_BLOCK
    for c in range(n_cb):
        k0 = _key_col0(c)
        for r in range(rows):
            qp = qn_s[r * GRID_W + c * COL_BLOCK:r * GRID_W + (c + 1) * COL_BLOCK, :]
            qc_s[0, c, r * COL_BLOCK:(r + 1) * COL_BLOCK, :] = jnp.where(first, qp, 0.0).astype(BF16)
            qc_s[1, c, r * COL_BLOCK:(r + 1) * COL_BLOCK, :] = jnp.where(first, 0.0, qp).astype(BF16)
            keys = slice(r * GRID_W + k0, r * GRID_W + k0 + KEY_COLS)
            dst = slice(r * KEY_COLS, (r + 1) * KEY_COLS)
            kc_s[c, dst, :] = kn_s[keys, :].astype(BF16)
            vp = v_ref[keys, :]
            vc_s[0, c, dst, :] = jnp.where(first, vp, 1.0).astype(BF16)
            vc_s[1, c, dst, :] = jnp.where(first, 1.0, vp).astype(BF16)

    n_groups = rows // ROWS_PER_STEP
    nq = ROWS_PER_STEP * COL_BLOCK
    for g in range(n_groups):
        b0 = min(max(g * ROWS_PER_STEP - WIN_ROWS // 2, 0), rows - BAND_ROWS)
        case = 0 if g == 0 else (2 if g == n_groups - 1 else 1)
        patch = slice(b0 * KEY_COLS, (b0 + BAND_ROWS) * KEY_COLS)
        blocks = [(c, h) for c in range(n_cb) for h in range(2)]
        s = jnp.concatenate(
            [lax.dot_general(qc_s[h, c, g * nq:(g + 1) * nq, :], kc_s[c, patch, :], _NT,
                             preferred_element_type=F32) + tbl_ref[h, case, c] for c, h in blocks], axis=0)
        p = jnp.exp(s - jnp.max(s, axis=-1, keepdims=True)).astype(BF16)
        pv = jnp.concatenate(
            [jnp.dot(p[i * nq:(i + 1) * nq, :], vc_s[h, c, patch, :], preferred_element_type=F32)
             for i, (c, h) in enumerate(blocks)], axis=0)
        o = pv / pltpu.roll(pv, HEAD_DIM, axis=1)
        for c in range(n_cb):
            out = jnp.where(first, o[2 * c * nq:(2 * c + 1) * nq, :], o[(2 * c + 1) * nq:(2 * c + 2) * nq, :])
            for u in range(ROWS_PER_STEP):
                row0 = (g * ROWS_PER_STEP + u) * GRID_W + c * COL_BLOCK
                attn_ref[row0:row0 + COL_BLOCK, :] = out[u * COL_BLOCK:(u + 1) * COL_BLOCK, :].astype(BF16)

    zeros = jnp.zeros((8, LANES), F32)
    pad_s[0:8, :] = zeros
    pad_s[seq + 8:seq + 16, :] = zeros
    pad_s[8:seq + 8, :] = cg_ref[...] * xv_ref[...]
    z = (pad_s[7:seq + 7, :] * cw_ref[0:1, :] + pad_s[8:seq + 8, :] * cw_ref[1:2, :]
         + pad_s[9:seq + 9, :] * cw_ref[2:3, :])
    conv_ref[...] = (bg_ref[...] * z).astype(BF16)


def _mixers(proj3, qk_g, tbl, layer, conv_w):
    b, seq, n = proj3.shape
    width = n // 6
    nblk = width // LANES
    rows = seq // GRID_W
    assert rows >= BAND_ROWS + ROWS_PER_STEP and rows % ROWS_PER_STEP == 0
    assert LANES == 2 * HEAD_DIM and GRID_W % COL_BLOCK == 0
    qkg = jnp.tile(qk_g, (1, LANES // HEAD_DIM))
    n_cb = GRID_W // COL_BLOCK

    def col(k):
        return pl.BlockSpec((None, seq, LANES), lambda i, j, k=k: (i, 0, k * nblk + j))

    out_spec = pl.BlockSpec((None, seq, LANES), lambda i, j: (i, 0, j))
    hp = LANES // HEAD_DIM
    return pl.pallas_call(
        functools.partial(_mixers_kernel, rows=rows),
        grid=(b, nblk),
        in_specs=[col(0), col(1), col(2), col(3), col(4), col(5),
                  pl.BlockSpec((2, LANES), lambda i, j: (0, 0)),
                  pl.BlockSpec((hp,) + tbl.shape[1:], lambda i, j: (layer * nblk + j, 0, 0, 0, 0)),
                  pl.BlockSpec((conv_w.shape[0], LANES), lambda i, j: (0, j))],
        out_specs=[out_spec, out_spec],
        out_shape=[jax.ShapeDtypeStruct((b, seq, width), BF16)] * 2,
        scratch_shapes=[pltpu.VMEM((seq, LANES), F32), pltpu.VMEM((seq, LANES), F32),
                        pltpu.VMEM((hp, n_cb, rows * COL_BLOCK, LANES), BF16),
                        pltpu.VMEM((n_cb, rows * KEY_COLS, LANES), BF16),
                        pltpu.VMEM((hp, n_cb, rows * KEY_COLS, LANES), BF16),
                        pltpu.VMEM((seq + 16, LANES), F32)],
        compiler_params=_params("parallel", "parallel"),
        name="attn_conv",
    )(proj3, proj3, proj3, proj3, proj3, proj3, qkg, tbl, conv_w)


def _route(x, g_ref, sc_ref, sh_ref, rw_ref, h_ref, aff_ref):
    h = _ln_mod(x, g_ref[...], sc_ref[...], sh_ref[...])
    tm, d = h.shape
    nch = d // LANES
    n_exp = aff_ref.shape[0]
    for j in range(nch):
        h_ref[pl.ds(j, tm, stride=nch), :] = h[:, j * LANES:(j + 1) * LANES]
    h0 = h.astype(BF16)
    h1 = (h - h0.astype(F32)).astype(BF16)
    lg = jnp.dot(h0, rw_ref[...], preferred_element_type=F32)
    lg = (lg[:, :LANES] + lg[:, LANES:]) + jnp.dot(h1, rw_ref[:, :LANES], preferred_element_type=F32)
    lane = lax.broadcasted_iota(jnp.int32, (1, LANES), 1)
    lg = jnp.where(lane < n_exp, lg, NEG_BIAS)
    p = jnp.exp(lg - jnp.max(lg, axis=-1, keepdims=True))
    aff = p / jnp.sum(p, axis=-1, keepdims=True)
    aff_ref[...] = aff.T[:n_exp, :]


def _router_weights(router_w):
    w = jnp.pad(router_w, ((0, 0), (0, LANES - router_w.shape[1])))
    w0 = w.astype(BF16)
    w1 = (w - w0.astype(F32)).astype(BF16)
    return jnp.concatenate([w0, w1], axis=1)


def _route_specs(b, seq, d, e, tm):
    nch = d // LANES
    in_specs = [pl.BlockSpec((1, d), lambda i, j: (0, 0)),
                pl.BlockSpec((None, None, 1, d), lambda i, j: (i, 4, 0, 0)),
                pl.BlockSpec((None, None, 1, d), lambda i, j: (i, 3, 0, 0)),
                pl.BlockSpec((d, 2 * LANES), lambda i, j: (0, 0))]
    out_specs = [pl.BlockSpec((None, tm * nch, LANES), lambda i, j: (i, j, 0)),
                 pl.BlockSpec((None, e, tm), lambda i, j: (i, 0, j))]
    out_shape = [jax.ShapeDtypeStruct((b, seq * nch, LANES), F32),
                 jax.ShapeDtypeStruct((b, e, seq), F32)]
    return in_specs, out_specs, out_shape


def _outproj_kernel(a_ref, c_ref, wa_ref, wc_ref, x_ref, gate_ref, g2_ref, sc2_ref, sh2_ref, rw_ref,
                    o_ref, h_ref, aff_ref):
    m = jnp.dot(a_ref[...], wa_ref[...], preferred_element_type=F32)
    m = m + jnp.dot(c_ref[...], wc_ref[...], preferred_element_type=F32)
    x = x_ref[...] + gate_ref[...] * m
    o_ref[...] = x
    _route(x, g2_ref, sc2_ref, sh2_ref, rw_ref, h_ref, aff_ref)


def _out_projection(attn3, conv3, w_bf, x3, modl, g2, rw01, n_exp, tm=512):
    b, seq, d = x3.shape
    wa = attn3.shape[2]
    wc = conv3.shape[2]
    r_in, r_out, r_shape = _route_specs(b, seq, d, n_exp, tm)
    tile = pl.BlockSpec((None, tm, d), lambda i, j: (i, j, 0))
    return pl.pallas_call(
        _outproj_kernel,
        grid=(b, seq // tm),
        in_specs=[
            pl.BlockSpec((None, tm, wa), lambda i, j: (i, j, 0)),
            pl.BlockSpec((None, tm, wc), lambda i, j: (i, j, 0)),
            pl.BlockSpec((wa, d), lambda i, j: (0, 0)),
            pl.BlockSpec((wc, d), lambda i, j: (1, 0)),
            tile,
            pl.BlockSpec((None, None, 1, d), lambda i, j: (i, 2, 0, 0)),
        ] + r_in,
        out_specs=[tile] + r_out,
        out_shape=[jax.ShapeDtypeStruct((b, seq, d), F32)] + r_shape,
        compiler_params=_params("parallel", "parallel"),
        name="out_proj",
    )(attn3, conv3, w_bf, w_bf, x3, modl, g2, modl, modl, rw01)


def _dft_tables(n):
    jk = (np.arange(n)[:, None] * np.arange(n)[None, :]) % n
    ang = 2.0 * np.pi * jk.astype(np.float64) / n
    return np.cos(ang).astype(np.float32), np.sin(ang).astype(np.float32)


def _fnet_kernel(x_ref, g_ref, sc_ref, sh_ref, chan_ref, cs_ref, ss_ref, w_ref, gate_ref, g2_ref, sc2_ref,
                 sh2_ref, rw_ref, o_ref, h_ref, aff_ref, a_s, b_s, *, norm, gw, chunk):
    seq, d = x_ref.shape
    tm = o_ref.shape[0]
    j = pl.program_id(1)

    @pl.when(j == 0)
    def _():
        for r0 in range(0, seq, chunk):
            rows = slice(r0, r0 + chunk)
            h = _ln_mod(x_ref[rows, :], g_ref[...], sc_ref[...], sh_ref[...]).astype(BF16)
            for g in range(d // gw):
                cols = slice(g * gw, (g + 1) * gw)
                ab = jnp.dot(h[:, cols], chan_ref[...], preferred_element_type=F32)
                a_s[rows, cols] = ab[:, :gw].astype(BF16)
                b_s[rows, cols] = ab[:, gw:].astype(BF16)

    f = jnp.dot(cs_ref[...], a_s[...], preferred_element_type=F32)
    f = f - jnp.dot(ss_ref[...], b_s[...], preferred_element_type=F32)
    m = jnp.dot((f * norm).astype(BF16), w_ref[...], preferred_element_type=F32)
    x = x_ref[pl.ds(pl.multiple_of(j * tm, tm), tm), :] + gate_ref[...] * m
    o_ref[...] = x
    _route(x, g2_ref, sc2_ref, sh2_ref, rw_ref, h_ref, aff_ref)


def _fnet_layer(x3, g, modl, w_bf, g2, rw01, n_exp, tm=512):
    b, seq, d = x3.shape
    gw = d // FOURIER_GROUPS
    cc, sc = _dft_tables(gw)
    cs_chan = jnp.concatenate([jnp.asarray(cc), jnp.asarray(sc)], axis=1).astype(BF16)
    cseq, sseq = _dft_tables(seq)
    cseq = jnp.asarray(cseq).astype(BF16)
    sseq = jnp.asarray(sseq).astype(BF16)
    norm = 1.0 / math.sqrt(seq * gw)
    r_in, r_out, r_shape = _route_specs(b, seq, d, n_exp, tm)
    mod_row = lambda k: pl.BlockSpec((None, None, 1, d), lambda i, j: (i, k, 0, 0))
    return pl.pallas_call(
        functools.partial(_fnet_kernel, norm=norm, gw=gw, chunk=min(512, seq)),
        grid=(b, seq // tm),
        in_specs=[
            pl.BlockSpec((None, seq, d), lambda i, j: (i, 0, 0)),
            pl.BlockSpec((1, d), lambda i, j: (0, 0)),
            mod_row(1), mod_row(0),
            pl.BlockSpec((gw, 2 * gw), lambda i, j: (0, 0)),
            pl.BlockSpec((tm, seq), lambda i, j: (j, 0)),
            pl.BlockSpec((tm, seq), lambda i, j: (j, 0)),
            pl.BlockSpec((d, d), lambda i, j: (0, 0)),
            mod_row(2),
        ] + r_in,
        out_specs=[pl.BlockSpec((None, tm, d), lambda i, j: (i, j, 0))] + r_out,
        out_shape=[jax.ShapeDtypeStruct((b, seq, d), F32)] + r_shape,
        scratch_shapes=[pltpu.VMEM((seq, d), BF16), pltpu.VMEM((seq, d), BF16)],
        compiler_params=_params("parallel", "arbitrary"),
        name="fnet",
    )(x3, g, modl, modl, cs_chan, cseq, sseq, w_bf, modl, g2, modl, modl, rw01)


def _topk_kernel(aff_ref, tok_ref, gs_ref, tri_s, *, cap, row_scale):
    n_exp, seq = aff_ref.shape

    @pl.when(pl.program_id(0) == 0)
    def _():
        r = lax.broadcasted_iota(jnp.int32, (seq, seq), 0)
        c = lax.broadcasted_iota(jnp.int32, (seq, seq), 1)
        tri_s[...] = jnp.where(r < c, 1.0, 0.0).astype(BF16)

    a = aff_ref[...]

    def enough(cand):
        cnt = jnp.sum((a >= pltpu.bitcast(cand, F32)).astype(jnp.int32), axis=1, keepdims=True)
        return cnt >= cap

    def search(i, t):
        hi = lax.shift_left(jnp.int32(1), 29 - 2 * i)
        lo = lax.shift_left(jnp.int32(1), 28 - 2 * i)
        ok_hi, ok_lo, ok_both = enough(t | hi), enough(t | lo), enough(t | hi | lo)
        return jnp.where(ok_hi, jnp.where(ok_both, t | hi | lo, t | hi), jnp.where(ok_lo, t | lo, t))

    top = jnp.full((a.shape[0], 1), 1 << 30, jnp.int32)
    thr = lax.fori_loop(0, 15, search, jnp.where(enough(top), top, 0))
    thr = pltpu.bitcast(thr, F32)
    gt = a > thr
    eq = a == thr
    need = cap - jnp.sum(gt.astype(jnp.int32), axis=1, keepdims=True)
    tri = tri_s[...]
    eq_before = jnp.dot(jnp.where(eq, 1.0, 0.0).astype(BF16), tri, preferred_element_type=F32)
    sel = gt | (eq & (eq_before < need.astype(F32)))
    before = jnp.dot(jnp.where(sel, 1.0, 0.0).astype(BF16), tri, preferred_element_type=F32)
    rank = jnp.where(sel, before.astype(jnp.int32), -1)

    digit = lax.broadcasted_iota(jnp.int32, (SLOT_RADIX, seq), 0)
    tok_ids = lax.broadcasted_iota(jnp.int32, (1, seq), 1)
    t_hi = lax.shift_right_logical(tok_ids, 6).astype(F32)
    t_lo = (tok_ids & 63).astype(F32)
    for e in range(n_exp):
        r = rank[e:e + 1, :]
        hi = jnp.where(digit == lax.shift_right_arithmetic(r, SLOT_SHIFT), 1.0, 0.0).astype(BF16)
        lo = digit == (r & (SLOT_RADIX - 1))
        g = a[e:e + 1, :]
        g0 = g.astype(BF16).astype(F32)
        g1 = (g - g0).astype(BF16).astype(F32)
        g2 = (g - g0) - g1
        vals = jnp.concatenate([jnp.where(lo, v, 0.0) for v in (t_hi, t_lo, g0, g1, g2)], axis=0)
        res = lax.dot_general(hi, vals.astype(BF16), _NT, preferred_element_type=F32)
        part = [res[:, k * SLOT_RADIX:(k + 1) * SLOT_RADIX] for k in range(5)]
        tok_ref[e] = ((part[0] * 64.0 + part[1]) * float(row_scale)).astype(jnp.int32)
        gs_ref[e] = (part[2] + part[3]) + part[4]


def _topk(aff, cap, row_scale):
    b, e, seq = aff.shape
    n_hi = cap // SLOT_RADIX
    assert cap % SLOT_RADIX == 0 and n_hi <= SLOT_RADIX and seq <= 64 * 64
    out_spec = pl.BlockSpec((None, e, SLOT_RADIX, SLOT_RADIX), lambda i: (i, 0, 0, 0))
    tok, gs = pl.pallas_call(
        functools.partial(_topk_kernel, cap=cap, row_scale=row_scale),
        grid=(b,),
        in_specs=[pl.BlockSpec((None, e, seq), lambda i: (i, 0, 0))],
        out_specs=[out_spec, out_spec],
        out_shape=[jax.ShapeDtypeStruct((b, e, SLOT_RADIX, SLOT_RADIX), jnp.int32),
                   jax.ShapeDtypeStruct((b, e, SLOT_RADIX, SLOT_RADIX), F32)],
        scratch_shapes=[pltpu.VMEM((seq, seq), BF16)],
        compiler_params=_params("arbitrary"),
        name="topk",
    )(aff)
    return tok[:, :, :n_hi].reshape(b, e, cap), gs[:, :, :n_hi].reshape(b, e, cap)


def _gather_kernel(tok_ref, h_ref, o_ref, x_s, *, stride):
    n_exp, cap = tok_ref.shape
    nch = x_s.shape[0] // stride

    def expert(e, carry):
        for i in range(cap):
            t = pl.multiple_of(tok_ref[e, i], nch)
            x_s[pl.ds(i, nch, stride=stride), :] = h_ref[pl.ds(t, nch), :]
        xe = jnp.concatenate([x_s[j * stride:j * stride + cap, :] for j in range(nch)], axis=-1)
        o_ref[e] = xe.astype(BF16)
        return carry

    lax.fori_loop(0, n_exp, expert, 0)


def _gather_tokens(h_slab, tok, d, stride):
    b, rows, _ = h_slab.shape
    _, e, cap = tok.shape
    nch = d // LANES
    return pl.pallas_call(
        functools.partial(_gather_kernel, stride=stride),
        grid=(b,),
        in_specs=[
            pl.BlockSpec((None, e, cap), lambda i: (i, 0, 0), memory_space=pltpu.SMEM),
            pl.BlockSpec((None, rows, LANES), lambda i: (i, 0, 0)),
        ],
        out_specs=pl.BlockSpec((e, None, cap, d), lambda i: (0, i, 0, 0)),
        out_shape=jax.ShapeDtypeStruct((e, b, cap, d), BF16),
        scratch_shapes=[pltpu.VMEM((nch * stride, LANES), F32)],
        compiler_params=_params("parallel"),
        name="moe_gather",
    )(tok, h_slab)


def _ffn_kernel(x_ref, gs_ref, wg_ref, wu_ref, wd_ref, y_ref, wg_s, wu_s, wd_s, *, cap, ff_chunk):
    @pl.when(pl.program_id(1) == 0)
    def _():
        wg_s[...] = wg_ref[...].astype(BF16)
        wu_s[...] = wu_ref[...].astype(BF16)
        wd_s[...] = wd_ref[...].astype(BF16)

    x = x_ref[...]
    y = None
    for c in range(wg_s.shape[1] // ff_chunk):
        cols = slice(c * ff_chunk, (c + 1) * ff_chunk)
        a = jnp.dot(x, wg_s[:, cols], preferred_element_type=F32)
        u = jnp.dot(x, wu_s[:, cols], preferred_element_type=F32)
        hm = (a * jax.nn.sigmoid(a) * u).astype(BF16)
        yc = jnp.dot(hm, wd_s[cols, :], preferred_element_type=F32)
        y = yc if y is None else y + yc
    g = jnp.broadcast_to(gs_ref[...], (LANES, x.shape[0])).T
    bpb, nch, stride, _ = y_ref.shape
    for bb in range(bpb):
        rows = slice(bb * cap, (bb + 1) * cap)
        for j in range(nch):
            y_ref[bb, j, 0:cap, :] = y[rows, j * LANES:(j + 1) * LANES] * g[rows, :]
        y_ref[bb, :, cap:, :] = jnp.zeros((nch, stride - cap, LANES), F32)


def _expert_ffn(xe, gs_col, layer, wg, wu, wd, cap, stride, tm):
    e, m, d = xe.shape
    ff = wg.shape[3]
    nch = d // LANES
    bpb = tm // cap
    return pl.pallas_call(
        functools.partial(_ffn_kernel, cap=cap, ff_chunk=min(512, ff)),
        grid=(e, m // tm),
        in_specs=[
            pl.BlockSpec((None, tm, d), lambda i, j: (i, j, 0)),
            pl.BlockSpec((None, 1, tm), lambda i, j: (i, 0, j)),
            pl.BlockSpec((None, None, d, ff), lambda i, j: (layer, i, 0, 0)),
            pl.BlockSpec((None, None, d, ff), lambda i, j: (layer, i, 0, 0)),
            pl.BlockSpec((None, None, ff, d), lambda i, j: (layer, i, 0, 0)),
        ],
        out_specs=pl.BlockSpec((None, bpb, nch, stride, LANES), lambda i, j: (i, j, 0, 0, 0)),
        out_shape=jax.ShapeDtypeStruct((e, m // cap, nch, stride, LANES), F32),
        scratch_shapes=[pltpu.VMEM((d, ff), BF16), pltpu.VMEM((d, ff), BF16), pltpu.VMEM((ff, d), BF16)],
        compiler_params=_params("parallel", "arbitrary"),
        name="moe_ffn",
    )(xe, gs_col, wg, wu, wd)


def _scatter_kernel(tok_ref, y_ref, x_ref, gate_ref, o_ref, acc_s, *, stride, n_steps):
    group, _, cap = tok_ref.shape
    ts, d = x_ref.shape
    nch = d // LANES
    k = pl.program_id(1)

    @pl.when(k == 0)
    def _():
        acc_s[...] = jnp.zeros_like(acc_s)

    @pl.when(k < n_steps)
    def _():
        def expert(ee, carry):
            def slab(i):
                return pl.ds(pl.multiple_of(tok_ref[ee, 0, i], nch), nch)

            for i0 in range(0, cap, SCATTER_BATCH):
                vals = [acc_s[slab(i), :] + y_ref[ee, pl.ds(i, nch, stride=stride), :]
                        for i in range(i0, i0 + SCATTER_BATCH)]
                for i, v in zip(range(i0, i0 + SCATTER_BATCH), vals):
                    acc_s[slab(i), :] = v
            return carry

        lax.fori_loop(0, group, expert, 0)

    @pl.when(k >= n_steps)
    def _():
        base = pl.multiple_of((k - n_steps) * (ts * nch), ts * nch)
        for j in range(nch):
            lanes = slice(j * LANES, (j + 1) * LANES)
            o_ref[:, lanes] = (x_ref[:, lanes]
                               + gate_ref[:, lanes] * acc_s[pl.ds(base + j, ts, stride=nch), :])


def _scatter_residual(y_cm, tok, x3, modl, stride):
    e, b, _, _ = y_cm.shape
    _, seq, d = x3.shape
    cap = tok.shape[-1]
    nch = d // LANES
    assert cap % SCATTER_BATCH == 0
    ts = min(1024, seq)
    group = SCATTER_EXPERTS
    assert e % group == 0
    n_steps = e // group
    smem_rows = pl.BlockSpec((None, group, 1, cap), lambda i, k: (i, jnp.minimum(k, n_steps - 1), 0, 0),
                             memory_space=pltpu.SMEM)
    tile = pl.BlockSpec((None, ts, d), lambda i, k: (i, jnp.maximum(k - n_steps, 0), 0))
    return pl.pallas_call(
        functools.partial(_scatter_kernel, stride=stride, n_steps=n_steps),
        grid=(b, n_steps + seq // ts),
        in_specs=[
            smem_rows,
            pl.BlockSpec((group, None, nch * stride, LANES),
                         lambda i, k: (jnp.minimum(k, n_steps - 1), i, 0, 0)),
            tile,
            pl.BlockSpec((None, None, 1, d), lambda i, k: (i, 5, 0, 0)),
        ],
        out_specs=tile,
        out_shape=jax.ShapeDtypeStruct((b, seq, d), F32),
        scratch_shapes=[pltpu.VMEM((seq * nch, LANES), F32)],
        compiler_params=_params("parallel", "arbitrary"),
        name="moe_scatter",
    )(tok.reshape(b, e, 1, cap), y_cm, x3, modl)


def _moe_layer(x3, h_slab, aff, modl, layer, wg, wu, wd):
    b, seq, d = x3.shape
    e = aff.shape[1]
    cap = max(1, CAPACITY_FACTOR * seq // e)
    stride = cap + 8
    tok, gs = _topk(aff, cap, d // LANES)
    xe = _gather_tokens(h_slab, tok, d, stride)
    gs_col = gs.transpose(1, 0, 2).reshape(e, 1, b * cap)
    y = _expert_ffn(xe.reshape(e, b * cap, d), gs_col, layer, wg, wu, wd, cap, stride,
                    tm=min(1024, b * cap))
    return _scatter_residual(y.reshape(e, b, -1, LANES), tok, x3, modl, stride)


def kernel(x, c, mod_w, mod_b, norm_g, mix_w_in, qk_g, rpb, conv_w, mix_w_out, fnet_w_out,
           router_w, exp_w_gate, exp_w_up, exp_w_down):
    b, seq, d = x.shape
    depth = mod_w.shape[0]
    mod = _modulation(c, mod_w, mod_b).reshape(depth, b, N_MOD, 1, d)
    tbl = _bias_table(rpb.reshape((-1,) + rpb.shape[2:]))
    for l in range(depth):
        modl = mod[l]
        g1 = norm_g[l, 0].reshape(1, d)
        g2 = norm_g[l, 1].reshape(1, d)
        rw01 = _router_weights(router_w[l])
        n_exp = router_w.shape[2]
        if l % 2 == 0:
            j = l // 2
            proj = _in_projection(x.reshape(b * seq, d), g1, modl, mix_w_in[j].astype(BF16), seq)
            attn, conv = _mixers(proj.reshape(b, seq, -1), qk_g[j], tbl, j, conv_w[j])
            x, h_slab, aff = _out_projection(attn, conv, mix_w_out[j].astype(BF16), x, modl,
                                             g2, rw01, n_exp)
        else:
            x, h_slab, aff = _fnet_layer(x, g1, modl, fnet_w_out[l // 2].astype(BF16), g2, rw01, n_exp)
        x = _moe_layer(x, h_slab, aff, modl, l, exp_w_gate, exp_w_up, exp_w_down)
    return x
```

```python
import functools
import math

import numpy as np
import jax
import jax.numpy as jnp
from jax import lax
from jax.experimental import pallas as pl
from jax.experimental.pallas import tpu as pltpu

F32 = jnp.float32
BF16 = jnp.bfloat16
HIGHEST = lax.Precision.HIGHEST

GRID_W = 64
WIN_ROWS = 8
WIN_COLS = 16
HEAD_DIM = 64
FOURIER_GROUPS = 4
N_MOD = 6
CAPACITY_FACTOR = 2
RMS_EPS = 1e-6
NEG_BIAS = -1e30
LANES = 128
ROWS_PER_STEP = WIN_ROWS // 2
BAND_ROWS = WIN_ROWS + ROWS_PER_STEP
COL_BLOCK = WIN_COLS
KEY_COLS = 2 * WIN_COLS
GROUPS_PER_STAGE = 2
SCATTER_BATCH = 8
SCATTER_EXPERTS = 4
SLOT_SHIFT = 4
SLOT_RADIX = 1 << SLOT_SHIFT
VMEM_LIMIT = 56 * 1024 * 1024

_NT = (((1,), (1,)), ((), ()))


def _params(*sem):
    return pltpu.CompilerParams(dimension_semantics=sem, vmem_limit_bytes=VMEM_LIMIT)


def _ln_mod(x, g, sc, sh):
    y = x * lax.rsqrt(jnp.mean(x * x, axis=-1, keepdims=True) + RMS_EPS)
    return (y * g) * (1.0 + sc) + sh


def _mod_kernel(c_ref, w_ref, b_ref, o_ref):
    c = c_ref[...]
    s = c * jax.nn.sigmoid(c)
    o_ref[...] = jnp.dot(s, w_ref[...], precision=HIGHEST, preferred_element_type=F32) + b_ref[...]


def _modulation(c, mod_w, mod_b):
    depth, d, n = mod_w.shape
    b = c.shape[0]
    tn = 1536
    return pl.pallas_call(
        _mod_kernel,
        grid=(depth, n // tn),
        in_specs=[
            pl.BlockSpec((b, d), lambda l, j: (0, 0)),
            pl.BlockSpec((None, d, tn), lambda l, j: (l, 0, j)),
            pl.BlockSpec((None, 1, tn), lambda l, j: (l, 0, j)),
        ],
        out_specs=pl.BlockSpec((None, b, tn), lambda l, j: (l, 0, j)),
        out_shape=jax.ShapeDtypeStruct((depth, b, n), F32),
        compiler_params=_params("parallel", "parallel"),
        name="adaln_mod",
    )(c, mod_w, mod_b.reshape(depth, 1, n))


def _row_spec(d, tiles_per_seq, k):
    return pl.BlockSpec((None, None, 1, d), lambda i: (i // tiles_per_seq, k, 0, 0))


def _inproj_kernel(x_ref, g_ref, sc_ref, sh_ref, w_ref, o_ref):
    h = _ln_mod(x_ref[...], g_ref[...], sc_ref[...], sh_ref[...]).astype(BF16)
    o_ref[...] = jnp.dot(h, w_ref[...], preferred_element_type=F32)


def _in_projection(x2, g, modl, w_bf, seq, tm=1024):
    t, d = x2.shape
    n = w_bf.shape[1]
    tps = seq // tm
    return pl.pallas_call(
        _inproj_kernel,
        grid=(t // tm,),
        in_specs=[
            pl.BlockSpec((tm, d), lambda i: (i, 0)),
            pl.BlockSpec((1, d), lambda i: (0, 0)),
            _row_spec(d, tps, 1),
            _row_spec(d, tps, 0),
            pl.BlockSpec((d, n), lambda i: (0, 0)),
        ],
        out_specs=pl.BlockSpec((tm, n), lambda i: (i, 0)),
        out_shape=jax.ShapeDtypeStruct((t, n), F32),
        compiler_params=_params("parallel"),
        name="in_proj",
    )(x2, g, modl, modl, w_bf)


def _bias_table(rpb):
    heads, nr, nc = rpb.shape
    assert nr == 2 * WIN_ROWS - 1 and nc == 2 * WIN_COLS - 1
    n_cb = GRID_W // COL_BLOCK
    qcol = (np.arange(n_cb) * COL_BLOCK)[:, None, None] + np.arange(COL_BLOCK)[None, :, None]
    kcol = np.array([_key_col0(c) for c in range(n_cb)])[:, None, None] + np.arange(KEY_COLS)[None, None, :]
    cs = np.clip(qcol - WIN_COLS // 2, 0, GRID_W - WIN_COLS)
    col_ok = (kcol >= cs) & (kcol < cs + WIN_COLS)
    col_sel = col_ok[..., None] & (np.arange(nc) == (kcol - qcol + WIN_COLS - 1)[..., None])
    place = [[(u, 0) for u in range(ROWS_PER_STEP)],
             [(WIN_ROWS // 2 + u, u) for u in range(ROWS_PER_STEP)],
             [(BAND_ROWS - ROWS_PER_STEP + u, BAND_ROWS - WIN_ROWS) for u in range(ROWS_PER_STEP)]]
    row_sel = np.zeros((len(place), ROWS_PER_STEP, BAND_ROWS, nr), bool)
    for x, per_u in enumerate(place):
        for u, (off, lo) in enumerate(per_u):
            for i in range(lo, lo + WIN_ROWS):
                row_sel[x, u, i, i - off + WIN_ROWS - 1] = True
    row_ok = row_sel.any(axis=-1)
    n_case, nq, nk = len(place), ROWS_PER_STEP * COL_BLOCK, BAND_ROWS * KEY_COLS
    cols = jnp.einsum('hab,cqkb->hcqak', rpb, col_sel.astype(np.float32), precision=HIGHEST)
    depth = -(-nr * KEY_COLS // LANES) * LANES
    cols = jnp.pad(cols.reshape(heads * n_cb * COL_BLOCK, nr * KEY_COLS), ((0, 0), (0, depth - nr * KEY_COLS)))
    sel = row_sel.transpose(0, 3, 1, 2)[:, :, None, :, :, None] & np.eye(KEY_COLS, dtype=bool)[None, None, :, None, None, :]
    sel = np.pad(sel.reshape(n_case, nr * KEY_COLS, ROWS_PER_STEP * nk), ((0, 0), (0, depth - nr * KEY_COLS), (0, 0)))
    ok = row_ok[:, None, :, None, :, None] & col_ok[None, :, None, :, None, :]
    mask = np.where(ok, 0.0, NEG_BIAS).astype(np.float32).reshape(n_case, n_cb, nq, nk)
    return pl.pallas_call(
        functools.partial(_bias_kernel, n_u=ROWS_PER_STEP),
        grid=(n_case,),
        in_specs=[pl.BlockSpec(cols.shape, lambda x: (0, 0)),
                  pl.BlockSpec((None,) + sel.shape[1:], lambda x: (x, 0, 0)),
                  pl.BlockSpec((None, n_cb, nq, nk), lambda x: (x, 0, 0, 0))],
        out_specs=pl.BlockSpec((heads, None, n_cb, nq, nk), lambda x: (0, x, 0, 0, 0)),
        out_shape=jax.ShapeDtypeStruct((heads, n_case, n_cb, nq, nk), F32),
        compiler_params=_params("parallel"),
        name="bias_table",
    )(cols, jnp.asarray(sel, BF16), jnp.asarray(mask))


def _bias_kernel(cols_ref, sel_ref, mask_ref, o_ref, *, n_u):
    heads, n_cb, nq, nk = o_ref.shape
    nqc = nq // n_u
    t = cols_ref[...]
    t0 = t.astype(BF16)
    r = t - t0.astype(F32)
    t1 = r.astype(BF16)
    t2 = (r - t1.astype(F32)).astype(BF16)
    sel = sel_ref[...]
    res = (jnp.dot(t0, sel, preferred_element_type=F32) + jnp.dot(t1, sel, preferred_element_type=F32)
           + jnp.dot(t2, sel, preferred_element_type=F32))
    for h in range(heads):
        for c in range(n_cb):
            rows = slice((h * n_cb + c) * nqc, (h * n_cb + c + 1) * nqc)
            for u in range(n_u):
                o_ref[h, c, u * nqc:(u + 1) * nqc, :] = (res[rows, u * nk:(u + 1) * nk]
                                                         + mask_ref[c, u * nqc:(u + 1) * nqc, :])


def _key_col0(c):
    return min(max(c * COL_BLOCK - WIN_COLS // 2, 0), GRID_W - KEY_COLS)


def _mixers_kernel(q_ref, k_ref, v_ref, bg_ref, cg_ref, xv_ref, qkg_ref, tbl_ref, cw_ref,
                   attn_ref, conv_ref, qn_s, kn_s, qc_s, kc_s, vc_s, pad_s, *, rows):
    seq = q_ref.shape[0]
    lane = lax.broadcasted_iota(jnp.int32, (1, LANES), 1)
    first = lane < HEAD_DIM

    def head_norm(x, g):
        xx = x * x
        s0 = jnp.sum(jnp.where(first, xx, 0.0), axis=-1, keepdims=True)
        s1 = jnp.sum(jnp.where(first, 0.0, xx), axis=-1, keepdims=True)
        ms = jnp.where(first, s0, s1) * (1.0 / HEAD_DIM)
        return x * lax.rsqrt(ms + RMS_EPS) * g

    qn_s[...] = head_norm(q_ref[...], qkg_ref[0:1, :]) * (1.0 / math.sqrt(HEAD_DIM))
    kn_s[...] = head_norm(k_ref[...], qkg_ref[1:2, :])

    n_cb = GRID_W // COL_BLOCK
    for c in range(n_cb):
        k0 = _key_col0(c)
        for r in range(rows):
            qp = qn_s[r * GRID_W + c * COL_BLOCK:r * GRID_W + (c + 1) * COL_BLOCK, :]
            qc_s[0, c, r * COL_BLOCK:(r + 1) * COL_BLOCK, :] = jnp.where(first, qp, 0.0).astype(BF16)
            qc_s[1, c, r * COL_BLOCK:(r + 1) * COL_BLOCK, :] = jnp.where(first, 0.0, qp).astype(BF16)
            keys = slice(r * GRID_W + k0, r * GRID_W + k0 + KEY_COLS)
            dst = slice(r * KEY_COLS, (r + 1) * KEY_COLS)
            kc_s[c, dst, :] = kn_s[keys, :].astype(BF16)
            vp = v_ref[keys, :]
            vc_s[0, c, dst, :] = jnp.where(first, vp, 1.0).astype(BF16)
            vc_s[1, c, dst, :] = jnp.where(first, 1.0, vp).astype(BF16)

    n_groups = rows // ROWS_PER_STEP
    nq = ROWS_PER_STEP * COL_BLOCK
    def place(g):
        b0 = min(max(g * ROWS_PER_STEP - WIN_ROWS // 2, 0), rows - BAND_ROWS)
        case = 0 if g == 0 else (2 if g == n_groups - 1 else 1)
        return case, slice(b0 * KEY_COLS, (b0 + BAND_ROWS) * KEY_COLS)

    for g0 in range(0, n_groups, GROUPS_PER_STAGE):
        blocks = [(g, c, h) for g in range(g0, g0 + GROUPS_PER_STAGE) for c in range(n_cb) for h in range(2)]
        s = jnp.concatenate(
            [lax.dot_general(qc_s[h, c, g * nq:(g + 1) * nq, :], kc_s[c, place(g)[1], :], _NT,
                             preferred_element_type=F32) + tbl_ref[h, place(g)[0], c]
             for g, c, h in blocks], axis=0)
        p = jnp.exp(s - jnp.max(s, axis=-1, keepdims=True)).astype(BF16)
        pv = jnp.concatenate(
            [jnp.dot(p[i * nq:(i + 1) * nq, :], vc_s[h, c, place(g)[1], :], preferred_element_type=F32)
             for i, (g, c, h) in enumerate(blocks)], axis=0)
        o = pv / pltpu.roll(pv, HEAD_DIM, axis=1)
        for k in range(len(blocks) // 2):
            g, c, _ = blocks[2 * k]
            out = jnp.where(first, o[2 * k * nq:(2 * k + 1) * nq, :], o[(2 * k + 1) * nq:(2 * k + 2) * nq, :])
            for u in range(ROWS_PER_STEP):
                row0 = (g * ROWS_PER_STEP + u) * GRID_W + c * COL_BLOCK
                attn_ref[row0:row0 + COL_BLOCK, :] = out[u * COL_BLOCK:(u + 1) * COL_BLOCK, :].astype(BF16)

    zeros = jnp.zeros((8, LANES), F32)
    pad_s[0:8, :] = zeros
    pad_s[seq + 8:seq + 16, :] = zeros
    pad_s[8:seq + 8, :] = cg_ref[...] * xv_ref[...]
    z = (pad_s[7:seq + 7, :] * cw_ref[0:1, :] + pad_s[8:seq + 8, :] * cw_ref[1:2, :]
         + pad_s[9:seq + 9, :] * cw_ref[2:3, :])
    conv_ref[...] = (bg_ref[...] * z).astype(BF16)


def _mixers(proj3, qk_g, tbl, layer, conv_w):
    b, seq, n = proj3.shape
    width = n // 6
    nblk = width // LANES
    rows = seq // GRID_W
    assert rows >= BAND_ROWS + ROWS_PER_STEP and rows % ROWS_PER_STEP == 0
    assert LANES == 2 * HEAD_DIM and GRID_W % COL_BLOCK == 0
    qkg = jnp.tile(qk_g, (1, LANES // HEAD_DIM))
    n_cb = GRID_W // COL_BLOCK

    def col(k):
        return pl.BlockSpec((None, seq, LANES), lambda i, j, k=k: (i, 0, k * nblk + j))

    out_spec = pl.BlockSpec((None, seq, LANES), lambda i, j: (i, 0, j))
    hp = LANES // HEAD_DIM
    return pl.pallas_call(
        functools.partial(_mixers_kernel, rows=rows),
        grid=(b, nblk),
        in_specs=[col(0), col(1), col(2), col(3), col(4), col(5),
                  pl.BlockSpec((2, LANES), lambda i, j: (0, 0)),
                  pl.BlockSpec((hp,) + tbl.shape[1:], lambda i, j: (layer * nblk + j, 0, 0, 0, 0)),
                  pl.BlockSpec((conv_w.shape[0], LANES), lambda i, j: (0, j))],
        out_specs=[out_spec, out_spec],
        out_shape=[jax.ShapeDtypeStruct((b, seq, width), BF16)] * 2,
        scratch_shapes=[pltpu.VMEM((seq, LANES), F32), pltpu.VMEM((seq, LANES), F32),
                        pltpu.VMEM((hp, n_cb, rows * COL_BLOCK, LANES), BF16),
                        pltpu.VMEM((n_cb, rows * KEY_COLS, LANES), BF16),
                        pltpu.VMEM((hp, n_cb, rows * KEY_COLS, LANES), BF16),
                        pltpu.VMEM((seq + 16, LANES), F32)],
        compiler_params=_params("parallel", "parallel"),
        name="attn_conv",
    )(proj3, proj3, proj3, proj3, proj3, proj3, qkg, tbl, conv_w)


def _route(x, g_ref, sc_ref, sh_ref, rw_ref, h_ref, aff_ref):
    h = _ln_mod(x, g_ref[...], sc_ref[...], sh_ref[...])
    tm, d = h.shape
    nch = d // LANES
    n_exp = aff_ref.shape[0]
    for j in range(nch):
        h_ref[pl.ds(j, tm, stride=nch), :] = h[:, j * LANES:(j + 1) * LANES]
    h0 = h.astype(BF16)
    h1 = (h - h0.astype(F32)).astype(BF16)
    lg = jnp.dot(h0, rw_ref[...], preferred_element_type=F32)
    lg = (lg[:, :LANES] + lg[:, LANES:]) + jnp.dot(h1, rw_ref[:, :LANES], preferred_element_type=F32)
    lane = lax.broadcasted_iota(jnp.int32, (1, LANES), 1)
    lg = jnp.where(lane < n_exp, lg, NEG_BIAS)
    p = jnp.exp(lg - jnp.max(lg, axis=-1, keepdims=True))
    aff = p / jnp.sum(p, axis=-1, keepdims=True)
    aff_ref[...] = aff.T[:n_exp, :]


def _router_weights(router_w):
    w = jnp.pad(router_w, ((0, 0), (0, LANES - router_w.shape[1])))
    w0 = w.astype(BF16)
    w1 = (w - w0.astype(F32)).astype(BF16)
    return jnp.concatenate([w0, w1], axis=1)


def _route_specs(b, seq, d, e, tm):
    nch = d // LANES
    in_specs = [pl.BlockSpec((1, d), lambda i, j: (0, 0)),
                pl.BlockSpec((None, None, 1, d), lambda i, j: (i, 4, 0, 0)),
                pl.BlockSpec((None, None, 1, d), lambda i, j: (i, 3, 0, 0)),
                pl.BlockSpec((d, 2 * LANES), lambda i, j: (0, 0))]
    out_specs = [pl.BlockSpec((None, tm * nch, LANES), lambda i, j: (i, j, 0)),
                 pl.BlockSpec((None, e, tm), lambda i, j: (i, 0, j))]
    out_shape = [jax.ShapeDtypeStruct((b, seq * nch, LANES), F32),
                 jax.ShapeDtypeStruct((b, e, seq), F32)]
    return in_specs, out_specs, out_shape


def _outproj_kernel(a_ref, c_ref, wa_ref, wc_ref, x_ref, gate_ref, g2_ref, sc2_ref, sh2_ref, rw_ref,
                    o_ref, h_ref, aff_ref):
    m = jnp.dot(a_ref[...], wa_ref[...], preferred_element_type=F32)
    m = m + jnp.dot(c_ref[...], wc_ref[...], preferred_element_type=F32)
    x = x_ref[...] + gate_ref[...] * m
    o_ref[...] = x
    _route(x, g2_ref, sc2_ref, sh2_ref, rw_ref, h_ref, aff_ref)


def _out_projection(attn3, conv3, w_bf, x3, modl, g2, rw01, n_exp, tm=512):
    b, seq, d = x3.shape
    wa = attn3.shape[2]
    wc = conv3.shape[2]
    r_in, r_out, r_shape = _route_specs(b, seq, d, n_exp, tm)
    tile = pl.BlockSpec((None, tm, d), lambda i, j: (i, j, 0))
    return pl.pallas_call(
        _outproj_kernel,
        grid=(b, seq // tm),
        in_specs=[
            pl.BlockSpec((None, tm, wa), lambda i, j: (i, j, 0)),
            pl.BlockSpec((None, tm, wc), lambda i, j: (i, j, 0)),
            pl.BlockSpec((wa, d), lambda i, j: (0, 0)),
            pl.BlockSpec((wc, d), lambda i, j: (1, 0)),
            tile,
            pl.BlockSpec((None, None, 1, d), lambda i, j: (i, 2, 0, 0)),
        ] + r_in,
        out_specs=[tile] + r_out,
        out_shape=[jax.ShapeDtypeStruct((b, seq, d), F32)] + r_shape,
        compiler_params=_params("parallel", "parallel"),
        name="out_proj",
    )(attn3, conv3, w_bf, w_bf, x3, modl, g2, modl, modl, rw01)


def _dft_tables(n):
    jk = (np.arange(n)[:, None] * np.arange(n)[None, :]) % n
    ang = 2.0 * np.pi * jk.astype(np.float64) / n
    return np.cos(ang).astype(np.float32), np.sin(ang).astype(np.float32)


def _fnet_kernel(x_ref, g_ref, sc_ref, sh_ref, chan_ref, cs_ref, ss_ref, w_ref, gate_ref, g2_ref, sc2_ref,
                 sh2_ref, rw_ref, o_ref, h_ref, aff_ref, a_s, b_s, *, norm, gw, chunk):
    seq, d = x_ref.shape
    tm = o_ref.shape[0]
    j = pl.program_id(1)

    @pl.when(j == 0)
    def _():
        for r0 in range(0, seq, chunk):
            rows = slice(r0, r0 + chunk)
            h = _ln_mod(x_ref[rows, :], g_ref[...], sc_ref[...], sh_ref[...]).astype(BF16)
            for g in range(d // gw):
                cols = slice(g * gw, (g + 1) * gw)
                ab = jnp.dot(h[:, cols], chan_ref[...], preferred_element_type=F32)
                a_s[rows, cols] = ab[:, :gw].astype(BF16)
                b_s[rows, cols] = ab[:, gw:].astype(BF16)

    f = jnp.dot(cs_ref[...], a_s[...], preferred_element_type=F32)
    f = f - jnp.dot(ss_ref[...], b_s[...], preferred_element_type=F32)
    m = jnp.dot((f * norm).astype(BF16), w_ref[...], preferred_element_type=F32)
    x = x_ref[pl.ds(pl.multiple_of(j * tm, tm), tm), :] + gate_ref[...] * m
    o_ref[...] = x
    _route(x, g2_ref, sc2_ref, sh2_ref, rw_ref, h_ref, aff_ref)


def _fnet_layer(x3, g, modl, w_bf, g2, rw01, n_exp, tm=512):
    b, seq, d = x3.shape
    gw = d // FOURIER_GROUPS
    cc, sc = _dft_tables(gw)
    cs_chan = jnp.concatenate([jnp.asarray(cc), jnp.asarray(sc)], axis=1).astype(BF16)
    cseq, sseq = _dft_tables(seq)
    cseq = jnp.asarray(cseq).astype(BF16)
    sseq = jnp.asarray(sseq).astype(BF16)
    norm = 1.0 / math.sqrt(seq * gw)
    r_in, r_out, r_shape = _route_specs(b, seq, d, n_exp, tm)
    mod_row = lambda k: pl.BlockSpec((None, None, 1, d), lambda i, j: (i, k, 0, 0))
    return pl.pallas_call(
        functools.partial(_fnet_kernel, norm=norm, gw=gw, chunk=min(512, seq)),
        grid=(b, seq // tm),
        in_specs=[
            pl.BlockSpec((None, seq, d), lambda i, j: (i, 0, 0)),
            pl.BlockSpec((1, d), lambda i, j: (0, 0)),
            mod_row(1), mod_row(0),
            pl.BlockSpec((gw, 2 * gw), lambda i, j: (0, 0)),
            pl.BlockSpec((tm, seq), lambda i, j: (j, 0)),
            pl.BlockSpec((tm, seq), lambda i, j: (j, 0)),
            pl.BlockSpec((d, d), lambda i, j: (0, 0)),
            mod_row(2),
        ] + r_in,
        out_specs=[pl.BlockSpec((None, tm, d), lambda i, j: (i, j, 0))] + r_out,
        out_shape=[jax.ShapeDtypeStruct((b, seq, d), F32)] + r_shape,
        scratch_shapes=[pltpu.VMEM((seq, d), BF16), pltpu.VMEM((seq, d), BF16)],
        compiler_params=_params("parallel", "arbitrary"),
        name="fnet",
    )(x3, g, modl, modl, cs_chan, cseq, sseq, w_bf, modl, g2, modl, modl, rw01)


def _topk_kernel(aff_ref, tok_ref, gs_ref, tri_s, *, cap, row_scale):
    n_exp, seq = aff_ref.shape

    @pl.when(pl.program_id(0) == 0)
    def _():
        r = lax.broadcasted_iota(jnp.int32, (seq, seq), 0)
        c = lax.broadcasted_iota(jnp.int32, (seq, seq), 1)
        tri_s[...] = jnp.where(r < c, 1.0, 0.0).astype(BF16)

    a = aff_ref[...]

    def enough(cand):
        cnt = jnp.sum((a >= pltpu.bitcast(cand, F32)).astype(jnp.int32), axis=1, keepdims=True)
        return cnt >= cap

    def search(i, t):
        hi = lax.shift_left(jnp.int32(1), 29 - 2 * i)
        lo = lax.shift_left(jnp.int32(1), 28 - 2 * i)
        ok_hi, ok_lo, ok_both = enough(t | hi), enough(t | lo), enough(t | hi | lo)
        return jnp.where(ok_hi, jnp.where(ok_both, t | hi | lo, t | hi), jnp.where(ok_lo, t | lo, t))

    top = jnp.full((a.shape[0], 1), 1 << 30, jnp.int32)
    thr = lax.fori_loop(0, 15, search, jnp.where(enough(top), top, 0))
    thr = pltpu.bitcast(thr, F32)
    gt = a > thr
    eq = a == thr
    need = cap - jnp.sum(gt.astype(jnp.int32), axis=1, keepdims=True)
    tri = tri_s[...]
    eq_before = jnp.dot(jnp.where(eq, 1.0, 0.0).astype(BF16), tri, preferred_element_type=F32)
    sel = gt | (eq & (eq_before < need.astype(F32)))
    before = jnp.dot(jnp.where(sel, 1.0, 0.0).astype(BF16), tri, preferred_element_type=F32)
    rank = jnp.where(sel, before.astype(jnp.int32), -1)

    digit = lax.broadcasted_iota(jnp.int32, (SLOT_RADIX, seq), 0)
    tok_ids = lax.broadcasted_iota(jnp.int32, (1, seq), 1)
    t_hi = lax.shift_right_logical(tok_ids, 6).astype(F32)
    t_lo = (tok_ids & 63).astype(F32)
    for e in range(n_exp):
        r = rank[e:e + 1, :]
        hi = jnp.where(digit == lax.shift_right_arithmetic(r, SLOT_SHIFT), 1.0, 0.0).astype(BF16)
        lo = digit == (r & (SLOT_RADIX - 1))
        g = a[e:e + 1, :]
        g0 = g.astype(BF16).astype(F32)
        g1 = (g - g0).astype(BF16).astype(F32)
        g2 = (g - g0) - g1
        vals = jnp.concatenate([jnp.where(lo, v, 0.0) for v in (t_hi, t_lo, g0, g1, g2)], axis=0)
        res = lax.dot_general(hi, vals.astype(BF16), _NT, preferred_element_type=F32)
        part = [res[:, k * SLOT_RADIX:(k + 1) * SLOT_RADIX] for k in range(5)]
        tok_ref[e] = ((part[0] * 64.0 + part[1]) * float(row_scale)).astype(jnp.int32)
        gs_ref[e] = (part[2] + part[3]) + part[4]


def _topk(aff, cap, row_scale):
    b, e, seq = aff.shape
    n_hi = cap // SLOT_RADIX
    assert cap % SLOT_RADIX == 0 and n_hi <= SLOT_RADIX and seq <= 64 * 64
    out_spec = pl.BlockSpec((None, e, SLOT_RADIX, SLOT_RADIX), lambda i: (i, 0, 0, 0))
    tok, gs = pl.pallas_call(
        functools.partial(_topk_kernel, cap=cap, row_scale=row_scale),
        grid=(b,),
        in_specs=[pl.BlockSpec((None, e, seq), lambda i: (i, 0, 0))],
        out_specs=[out_spec, out_spec],
        out_shape=[jax.ShapeDtypeStruct((b, e, SLOT_RADIX, SLOT_RADIX), jnp.int32),
                   jax.ShapeDtypeStruct((b, e, SLOT_RADIX, SLOT_RADIX), F32)],
        scratch_shapes=[pltpu.VMEM((seq, seq), BF16)],
        compiler_params=_params("arbitrary"),
        name="topk",
    )(aff)
    return tok[:, :, :n_hi].reshape(b, e, cap), gs[:, :, :n_hi].reshape(b, e, cap)


def _gather_kernel(tok_ref, h_ref, o_ref, x_s, *, stride):
    n_exp, cap = tok_ref.shape
    nch = x_s.shape[0] // stride

    def expert(e, carry):
        for i in range(cap):
            t = pl.multiple_of(tok_ref[e, i], nch)
            x_s[pl.ds(i, nch, stride=stride), :] = h_ref[pl.ds(t, nch), :]
        xe = jnp.concatenate([x_s[j * stride:j * stride + cap, :] for j in range(nch)], axis=-1)
        o_ref[e] = xe.astype(BF16)
        return carry

    lax.fori_loop(0, n_exp, expert, 0)


def _gather_tokens(h_slab, tok, d, stride):
    b, rows, _ = h_slab.shape
    _, e, cap = tok.shape
    nch = d // LANES
    return pl.pallas_call(
        functools.partial(_gather_kernel, stride=stride),
        grid=(b,),
        in_specs=[
            pl.BlockSpec((None, e, cap), lambda i: (i, 0, 0), memory_space=pltpu.SMEM),
            pl.BlockSpec((None, rows, LANES), lambda i: (i, 0, 0)),
        ],
        out_specs=pl.BlockSpec((e, None, cap, d), lambda i: (0, i, 0, 0)),
        out_shape=jax.ShapeDtypeStruct((e, b, cap, d), BF16),
        scratch_shapes=[pltpu.VMEM((nch * stride, LANES), F32)],
        compiler_params=_params("parallel"),
        name="moe_gather",
    )(tok, h_slab)


def _ffn_kernel(x_ref, gs_ref, wg_ref, wu_ref, wd_ref, y_ref, wg_s, wu_s, wd_s, *, cap, ff_chunk):
    @pl.when(pl.program_id(1) == 0)
    def _():
        wg_s[...] = wg_ref[...].astype(BF16)
        wu_s[...] = wu_ref[...].astype(BF16)
        wd_s[...] = wd_ref[...].astype(BF16)

    x = x_ref[...]
    y = None
    for c in range(wg_s.shape[1] // ff_chunk):
        cols = slice(c * ff_chunk, (c + 1) * ff_chunk)
        a = jnp.dot(x, wg_s[:, cols], preferred_element_type=F32)
        u = jnp.dot(x, wu_s[:, cols], preferred_element_type=F32)
        hm = (a * jax.nn.sigmoid(a) * u).astype(BF16)
        yc = jnp.dot(hm, wd_s[cols, :], preferred_element_type=F32)
        y = yc if y is None else y + yc
    g = jnp.broadcast_to(gs_ref[...], (LANES, x.shape[0])).T
    bpb, nch, stride, _ = y_ref.shape
    for bb in range(bpb):
        rows = slice(bb * cap, (bb + 1) * cap)
        for j in range(nch):
            y_ref[bb, j, 0:cap, :] = y[rows, j * LANES:(j + 1) * LANES] * g[rows, :]
        y_ref[bb, :, cap:, :] = jnp.zeros((nch, stride - cap, LANES), F32)


def _expert_ffn(xe, gs_col, layer, wg, wu, wd, cap, stride, tm):
    e, m, d = xe.shape
    ff = wg.shape[3]
    nch = d // LANES
    bpb = tm // cap
    return pl.pallas_call(
        functools.partial(_ffn_kernel, cap=cap, ff_chunk=min(512, ff)),
        grid=(e, m // tm),
        in_specs=[
            pl.BlockSpec((None, tm, d), lambda i, j: (i, j, 0)),
            pl.BlockSpec((None, 1, tm), lambda i, j: (i, 0, j)),
            pl.BlockSpec((None, None, d, ff), lambda i, j: (layer, i, 0, 0)),
            pl.BlockSpec((None, None, d, ff), lambda i, j: (layer, i, 0, 0)),
            pl.BlockSpec((None, None, ff, d), lambda i, j: (layer, i, 0, 0)),
        ],
        out_specs=pl.BlockSpec((None, bpb, nch, stride, LANES), lambda i, j: (i, j, 0, 0, 0)),
        out_shape=jax.ShapeDtypeStruct((e, m // cap, nch, stride, LANES), F32),
        scratch_shapes=[pltpu.VMEM((d, ff), BF16), pltpu.VMEM((d, ff), BF16), pltpu.VMEM((ff, d), BF16)],
        compiler_params=_params("parallel", "arbitrary"),
        name="moe_ffn",
    )(xe, gs_col, wg, wu, wd)


def _scatter_kernel(tok_ref, y_ref, x_ref, gate_ref, o_ref, acc_s, *, stride, n_steps):
    group, _, cap = tok_ref.shape
    ts, d = x_ref.shape
    nch = d // LANES
    k = pl.program_id(1)

    @pl.when(k == 0)
    def _():
        acc_s[...] = jnp.zeros_like(acc_s)

    @pl.when(k < n_steps)
    def _():
        def expert(ee, carry):
            def slab(i):
                return pl.ds(pl.multiple_of(tok_ref[ee, 0, i], nch), nch)

            for i0 in range(0, cap, SCATTER_BATCH):
                vals = [acc_s[slab(i), :] + y_ref[ee, pl.ds(i, nch, stride=stride), :]
                        for i in range(i0, i0 + SCATTER_BATCH)]
                for i, v in zip(range(i0, i0 + SCATTER_BATCH), vals):
                    acc_s[slab(i), :] = v
            return carry

        lax.fori_loop(0, group, expert, 0)

    @pl.when(k >= n_steps)
    def _():
        base = pl.multiple_of((k - n_steps) * (ts * nch), ts * nch)
        for j in range(nch):
            lanes = slice(j * LANES, (j + 1) * LANES)
            o_ref[:, lanes] = (x_ref[:, lanes]
                               + gate_ref[:, lanes] * acc_s[pl.ds(base + j, ts, stride=nch), :])


def _scatter_residual(y_cm, tok, x3, modl, stride):
    e, b, _, _ = y_cm.shape
    _, seq, d = x3.shape
    cap = tok.shape[-1]
    nch = d // LANES
    assert cap % SCATTER_BATCH == 0
    ts = min(1024, seq)
    group = SCATTER_EXPERTS
    assert e % group == 0
    n_steps = e // group
    smem_rows = pl.BlockSpec((None, group, 1, cap), lambda i, k: (i, jnp.minimum(k, n_steps - 1), 0, 0),
                             memory_space=pltpu.SMEM)
    tile = pl.BlockSpec((None, ts, d), lambda i, k: (i, jnp.maximum(k - n_steps, 0), 0))
    return pl.pallas_call(
        functools.partial(_scatter_kernel, stride=stride, n_steps=n_steps),
        grid=(b, n_steps + seq // ts),
        in_specs=[
            smem_rows,
            pl.BlockSpec((group, None, nch * stride, LANES),
                         lambda i, k: (jnp.minimum(k, n_steps - 1), i, 0, 0)),
            tile,
            pl.BlockSpec((None, None, 1, d), lambda i, k: (i, 5, 0, 0)),
        ],
        out_specs=tile,
        out_shape=jax.ShapeDtypeStruct((b, seq, d), F32),
        scratch_shapes=[pltpu.VMEM((seq * nch, LANES), F32)],
        compiler_params=_params("parallel", "arbitrary"),
        name="moe_scatter",
    )(tok.reshape(b, e, 1, cap), y_cm, x3, modl)


def _moe_layer(x3, h_slab, aff, modl, layer, wg, wu, wd):
    b, seq, d = x3.shape
    e = aff.shape[1]
    cap = max(1, CAPACITY_FACTOR * seq // e)
    stride = cap + 8
    tok, gs = _topk(aff, cap, d // LANES)
    xe = _gather_tokens(h_slab, tok, d, stride)
    gs_col = gs.transpose(1, 0, 2).reshape(e, 1, b * cap)
    y = _expert_ffn(xe.reshape(e, b * cap, d), gs_col, layer, wg, wu, wd, cap, stride,
                    tm=min(1024, b * cap))
    return _scatter_residual(y.reshape(e, b, -1, LANES), tok, x3, modl, stride)


def kernel(x, c, mod_w, mod_b, norm_g, mix_w_in, qk_g, rpb, conv_w, mix_w_out, fnet_w_out,
           router_w, exp_w_gate, exp_w_up, exp_w_down):
    b, seq, d = x.shape
    depth = mod_w.shape[0]
    mod = _modulation(c, mod_w, mod_b).reshape(depth, b, N_MOD, 1, d)
    tbl = _bias_table(rpb.reshape((-1,) + rpb.shape[2:]))
    for l in range(depth):
        modl = mod[l]
        g1 = norm_g[l, 0].reshape(1, d)
        g2 = norm_g[l, 1].reshape(1, d)
        rw01 = _router_weights(router_w[l])
        n_exp = router_w.shape[2]
        if l % 2 == 0:
            j = l // 2
            proj = _in_projection(x.reshape(b * seq, d), g1, modl, mix_w_in[j].astype(BF16), seq)
            attn, conv = _mixers(proj.reshape(b, seq, -1), qk_g[j], tbl, j, conv_w[j])
            x, h_slab, aff = _out_projection(attn, conv, mix_w_out[j].astype(BF16), x, modl,
                                             g2, rw01, n_exp)
        else:
            x, h_slab, aff = _fnet_layer(x, g1, modl, fnet_w_out[l // 2].astype(BF16), g2, rw01, n_exp)
        x = _moe_layer(x, h_slab, aff, modl, l, exp_w_gate, exp_w_up, exp_w_down)
    return x
```
